```python
import math
import jax
import jax.numpy as jnp
from jax import lax
import numpy as np

D_MODEL = 1024
BATCH = 32
SEQ = 2048
DEPTH = 2

GRID_W = 64
CTX_LEN = 256
EPS = 1e-6

D_MIX = D_MODEL
W_CONV = D_MIX // 4
W_POOL = D_MIX // 4
W_SSM = D_MIX // 4
W_MLA = D_MIX - W_CONV - W_POOL - W_SSM

CONV_K = 31

POOL_WINDOWS = (2, 4, 8, 16)
POOL_CH = W_POOL // len(POOL_WINDOWS)

SSM_CH = 16
SSM_GROUPS = W_SSM // SSM_CH
SSM_STATE = 64

MLA_V = 64
MLA_HEADS = W_MLA // MLA_V
MLA_NOPE = 64
MLA_ROPE = 32
MLA_QK = MLA_NOPE + MLA_ROPE
MLA_Q_RANK = 192
MLA_KV_RANK = 128
ROPE_AXIS = MLA_ROPE // 2
ROPE_BASE = 10000.0
Q_BLOCK = 128

N_EXPERTS = 32
TOP_K = 4
D_EXPERT = D_MODEL
SWIGLU_LIMIT = 7.0
SWIGLU_ALPHA = 1.702
MOE_BLOCK = 512

OFF_POOL = 2 * W_CONV
OFF_Q = OFF_POOL + W_POOL
OFF_SSM = OFF_Q + MLA_Q_RANK
OFF_KV = OFF_SSM + W_SSM
OFF_KPE = OFF_KV + MLA_KV_RANK
N_IN = OFF_KPE + MLA_ROPE
SPLIT_AT = (OFF_POOL, OFF_Q, OFF_SSM, OFF_KV, OFF_KPE)

kernel_name = 'hybrid_dit_conv_pool_s5_mla_moe'


def rmsnorm(x, g):
    xf = x.astype(jnp.float32)
    y = xf * lax.rsqrt(jnp.mean(xf * xf, axis=-1, keepdims=True) + EPS)
    return y.astype(x.dtype) * g


def layernorm(x, g, b):
    xf = x.astype(jnp.float32)
    mu = jnp.mean(xf, axis=-1, keepdims=True)
    var = jnp.mean(jnp.square(xf - mu), axis=-1, keepdims=True)
    return ((xf - mu) * lax.rsqrt(var + EPS)).astype(x.dtype) * g + b


def conv_module(p, dw, dw_b, ln_g, ln_b, pw):
    val, gate = jnp.split(p, 2, axis=-1)
    u = val * jax.nn.sigmoid(gate)
    u = lax.conv_general_dilated(
        u, dw[:, None, :], window_strides=(1,),
        padding=[(CONV_K // 2, CONV_K // 2)],
        dimension_numbers=('NWC', 'WIO', 'NWC'),
        feature_group_count=u.shape[-1]) + dw_b
    u = jax.nn.silu(layernorm(u, ln_g, ln_b))
    return u @ pw


def multiscale_pool(u, w, scale):
    bsz, length, _ = u.shape
    n_g = len(POOL_WINDOWS)
    ug = u.reshape(bsz, length, n_g, POOL_CH)
    cs = jnp.concatenate([jnp.zeros((bsz, 1, n_g, POOL_CH), jnp.float32),
                          jnp.cumsum(ug.astype(jnp.float32), axis=1)], axis=1)
    t = jnp.arange(length)
    groups = []
    for g, win in enumerate(POOL_WINDOWS):
        lo = jnp.maximum(t - win // 2, 0)
        hi = jnp.minimum(t + win - 1 - win // 2, length - 1)
        csg = cs[:, :, g]
        cnt = (hi - lo + 1).astype(jnp.float32)[None, :, None]
        mean = (csg[:, hi + 1] - csg[:, lo]) / cnt
        groups.append(mean.astype(u.dtype) - ug[:, :, g])
    pooled = jnp.stack(groups, axis=2)
    y = jnp.einsum('blgc,gcd->blgd', pooled, w)
    return y.reshape(bsz, length, W_POOL) * scale


def s5_discretise(a_re, a_im, log_dt, b_re, b_im):
    a_re = jnp.minimum(a_re.astype(jnp.float32), -1e-4)
    a_im = a_im.astype(jnp.float32)
    dt = jnp.exp(log_dt.astype(jnp.float32))[:, None]
    mag = jnp.exp(a_re * dt)
    ab_re = mag * jnp.cos(a_im * dt)
    ab_im = mag * jnp.sin(a_im * dt)
    den = a_re * a_re + a_im * a_im
    f_re = ((ab_re - 1.0) * a_re + ab_im * a_im) / den
    f_im = (ab_im * a_re - (ab_re - 1.0) * a_im) / den
    b_re = b_re.astype(jnp.float32)
    b_im = b_im.astype(jnp.float32)
    bb_re = f_re[..., None] * b_re - f_im[..., None] * b_im
    bb_im = f_re[..., None] * b_im + f_im[..., None] * b_re
    return ab_re, ab_im, bb_re, bb_im


def _complex_affine_combine(e1, e2):
    a1r, a1i, b1r, b1i = e1
    a2r, a2i, b2r, b2i = e2
    return (a1r * a2r - a1i * a2i, a1r * a2i + a1i * a2r,
            a2r * b1r - a2i * b1i + b2r, a2r * b1i + a2i * b1r + b2i)


def diag_scan(ab_re, ab_im, bb_re, bb_im, u, h0_re, h0_im, reverse):
    bu_re = jnp.einsum('blgc,gpc->blgp', u, bb_re)
    bu_im = jnp.einsum('blgc,gpc->blgp', u, bb_im)
    first = u.shape[1] - 1 if reverse else 0
    bu_re = bu_re.at[:, first].add(ab_re * h0_re - ab_im * h0_im)
    bu_im = bu_im.at[:, first].add(ab_re * h0_im + ab_im * h0_re)
    a_re = jnp.broadcast_to(ab_re, bu_re.shape)
    a_im = jnp.broadcast_to(ab_im, bu_im.shape)
    _, _, h_re, h_im = lax.associative_scan(
        _complex_affine_combine, (a_re, a_im, bu_re, bu_im), reverse=reverse, axis=1)
    return h_re, h_im


def s5_readout(h_re, h_im, c_re, c_im):
    return (jnp.einsum('blgp,gcp->blgc', h_re, c_re.astype(jnp.float32))
            - jnp.einsum('blgp,gcp->blgc', h_im, c_im.astype(jnp.float32)))


def s5_mixer(ux, uc, a_re, a_im, log_dt, b_re, b_im, c_re, c_im, d, glu_w, glu_b, need_ctx):
    def grouped(u):
        return u.astype(jnp.float32).reshape(u.shape[0], u.shape[1], SSM_GROUPS, SSM_CH)
    gx, gc = grouped(ux), grouped(uc)
    zero = jnp.zeros((ux.shape[0], SSM_GROUPS, SSM_STATE), jnp.float32)
    d_g = d.astype(jnp.float32).reshape(SSM_GROUPS, SSM_CH)
    yx = d_g * gx
    yc = d_g * gc if need_ctx else None
    for direction in range(2):
        reverse = direction == 1
        ab_re, ab_im, bb_re, bb_im = s5_discretise(
            a_re[direction], a_im[direction], log_dt[direction], b_re[direction], b_im[direction])
        hc_re, hc_im = diag_scan(ab_re, ab_im, bb_re, bb_im, gc, zero, zero, reverse)
        end = 0 if reverse else -1
        hx_re, hx_im = diag_scan(ab_re, ab_im, bb_re, bb_im, gx,
                                 hc_re[:, end], hc_im[:, end], reverse)
        yx = yx + s5_readout(hx_re, hx_im, c_re[direction], c_im[direction])
        if need_ctx:
            yc = yc + s5_readout(hc_re, hc_im, c_re[direction], c_im[direction])

    def glu(y, like):
        z = jax.nn.gelu(y.reshape(like.shape)).astype(like.dtype)
        return z * jax.nn.sigmoid(z @ glu_w + glu_b)
    return glu(yx, ux), (glu(yc, uc) if need_ctx else None)


def axial_rope_tables(rows, dtype):
    row = jnp.repeat(jnp.arange(rows), GRID_W).astype(jnp.float32)
    col = jnp.tile(jnp.arange(GRID_W), rows).astype(jnp.float32)
    inv = ROPE_BASE ** (-jnp.arange(0, ROPE_AXIS, 2, dtype=jnp.float32) / ROPE_AXIS)
    ang_r = row[:, None] * inv
    ang_c = col[:, None] * inv
    return tuple(t.astype(dtype)[:, None, :] for t in
                 (jnp.cos(ang_r), jnp.sin(ang_r), jnp.cos(ang_c), jnp.sin(ang_c)))


def _rotate_half(x, cos, sin):
    x1, x2 = jnp.split(x, 2, axis=-1)
    return jnp.concatenate([x1 * cos - x2 * sin, x2 * cos + x1 * sin], axis=-1)


def apply_axial_rope(t, tables):
    cos_r, sin_r, cos_c, sin_c = tables
    return jnp.concatenate([
        t[..., :MLA_NOPE],
        _rotate_half(t[..., MLA_NOPE:MLA_NOPE + ROPE_AXIS], cos_r, sin_r),
        _rotate_half(t[..., MLA_NOPE + ROPE_AXIS:], cos_c, sin_c)], axis=-1)


def mla_query(cq, q_a_g, wq_b, q_g):
    q = rmsnorm(cq, q_a_g) @ wq_b
    return rmsnorm(q.reshape(*cq.shape[:-1], MLA_HEADS, MLA_QK), q_g)


def mla_key_value(ckv, kpe, kv_a_g, wkv_b, k_g):
    lead = ckv.shape[:-1]
    kv = (rmsnorm(ckv, kv_a_g) @ wkv_b).reshape(*lead, MLA_HEADS, MLA_NOPE + MLA_V)
    k_pe = jnp.broadcast_to(kpe[..., None, :], (*lead, MLA_HEADS, MLA_ROPE))
    k = rmsnorm(jnp.concatenate([kv[..., :MLA_NOPE], k_pe], axis=-1), k_g)
    return k, kv[..., MLA_NOPE:]


def attend(q, k, v):
    s = jnp.einsum('bqhd,bkhd->bhqk', q, k).astype(jnp.float32) * (MLA_QK ** -0.5)
    p = jax.nn.softmax(s, axis=-1).astype(v.dtype)
    return jnp.einsum('bhqk,bkhv->bqhv', p, v)


def blocked_attention(q, k, v):
    bsz, length = q.shape[:2]
    qb = q.reshape(bsz, length // Q_BLOCK, Q_BLOCK, MLA_HEADS, MLA_QK).swapaxes(0, 1)
    o = lax.map(lambda qi: attend(qi, k, v), qb)
    return o.swapaxes(0, 1).reshape(bsz, length, MLA_HEADS * MLA_V)


def moe(h, router_w, router_b, w_in, b_in, w_out, b_out):
    n_tok, dim = h.shape
    logits = (h @ router_w + router_b).astype(jnp.float32)
    top_v, top_i = lax.top_k(logits, TOP_K)
    top_w = jax.nn.softmax(top_v, axis=-1)
    flat_e = top_i.reshape(-1)
    flat_t = jnp.arange(n_tok * TOP_K, dtype=jnp.int32) // TOP_K
    flat_w = top_w.reshape(-1)
    order = jnp.argsort(flat_e)
    sorted_e = flat_e[order]
    counts = jnp.bincount(flat_e, length=N_EXPERTS)
    padded = (counts + MOE_BLOCK - 1) // MOE_BLOCK * MOE_BLOCK
    pad_end = jnp.cumsum(padded)
    pad_start = pad_end - padded
    start = jnp.cumsum(counts) - counts
    dest = pad_start[sorted_e] + (jnp.arange(n_tok * TOP_K) - start[sorted_e])
    n_blocks = -(-(n_tok * TOP_K) // MOE_BLOCK) + N_EXPERTS
    n_pad = n_blocks * MOE_BLOCK
    tok_pad = jnp.full((n_pad,), n_tok, jnp.int32).at[dest].set(flat_t[order])
    w_pad = jnp.zeros((n_pad,), jnp.float32).at[dest].set(flat_w[order])
    block_e = jnp.minimum(jnp.searchsorted(pad_end, jnp.arange(n_blocks) * MOE_BLOCK, side='right'),
                          N_EXPERTS - 1)
    h_ext = jnp.concatenate([h, jnp.zeros((1, dim), h.dtype)], axis=0)

    def expert_block(args):
        idx, e = args
        xb = h_ext[idx]
        gu = xb @ w_in[e] + b_in[e]
        gate, up = jnp.split(gu, 2, axis=-1)
        gate = jnp.minimum(gate, SWIGLU_LIMIT)
        up = jnp.clip(up, -SWIGLU_LIMIT, SWIGLU_LIMIT)
        act = (up + 1.0) * gate * jax.nn.sigmoid(SWIGLU_ALPHA * gate)
        return act @ w_out[e] + b_out[e]

    y = lax.map(expert_block, (tok_pad.reshape(n_blocks, MOE_BLOCK), block_e)).reshape(n_pad, dim)
    y = y * w_pad.astype(y.dtype)[:, None]
    return jnp.zeros((n_tok + 1, dim), y.dtype).at[tok_pad].add(y)[:n_tok]


def setup_inputs(seed: int = 0) -> dict:
    key = jax.random.key(seed)
    keys = iter(jax.random.split(key, 48))

    def nrm(shape, std):
        return std * jax.random.normal(next(keys), shape, jnp.float32)

    def gain(shape):
        return 1.0 + nrm(shape, 0.02)

    L = DEPTH
    G, P = SSM_GROUPS, SSM_STATE
    state_idx = jnp.arange(P, dtype=jnp.float32)
    return {
        'x': nrm((BATCH, SEQ, D_MODEL), 1.0),
        'c': nrm((BATCH, D_MODEL), 1.0),
        'ctx': nrm((BATCH, CTX_LEN, D_MODEL), 1.0),
        'c_ctx': nrm((D_MODEL,), 1.0),
        'ada_w': nrm((L, D_MODEL, 6 * D_MODEL), 0.5 * D_MODEL ** -0.5),
        'ada_b': nrm((L, 6 * D_MODEL), 0.02),
        'norm1_g': gain((L, D_MODEL)),
        'norm2_g': gain((L, D_MODEL)),
        'w_mix_in': nrm((L, D_MODEL, N_IN), D_MODEL ** -0.5),
        'w_mix_out': nrm((L, D_MIX, D_MODEL), D_MIX ** -0.5),
        'conv_dw': nrm((L, CONV_K, W_CONV), CONV_K ** -0.5),
        'conv_dw_b': nrm((L, W_CONV), 0.02),
        'conv_ln_g': gain((L, W_CONV)),
        'conv_ln_b': nrm((L, W_CONV), 0.02),
        'conv_pw': nrm((L, W_CONV, W_CONV), W_CONV ** -0.5),
        'pool_w': nrm((L, len(POOL_WINDOWS), POOL_CH, POOL_CH), POOL_CH ** -0.5),
        'pool_scale': gain((L, W_POOL)),
        'ssm_a_re': -0.5 + nrm((L, 2, G, P), 0.01),
        'ssm_a_im': math.pi * state_idx + nrm((L, 2, G, P), 0.01),
        'ssm_log_dt': jax.random.uniform(next(keys), (L, 2, G), jnp.float32,
                                         math.log(1e-3), math.log(1e-1)),
        'ssm_b_re': nrm((L, 2, G, P, SSM_CH), (2 * SSM_CH) ** -0.5),
        'ssm_b_im': nrm((L, 2, G, P, SSM_CH), (2 * SSM_CH) ** -0.5),
        'ssm_c_re': nrm((L, 2, G, SSM_CH, P), P ** -0.5),
        'ssm_c_im': nrm((L, 2, G, SSM_CH, P), P ** -0.5),
        'ssm_d': nrm((L, W_SSM), 0.5),
        'ssm_glu_w': nrm((L, W_SSM, W_SSM), W_SSM ** -0.5),
        'ssm_glu_b': nrm((L, W_SSM), 0.02),
        'mla_q_a_g': gain((L, MLA_Q_RANK)),
        'mla_wq_b': nrm((L, MLA_Q_RANK, MLA_HEADS * MLA_QK), MLA_Q_RANK ** -0.5),
        'mla_kv_a_g': gain((L, MLA_KV_RANK)),
        'mla_wkv_b': nrm((L, MLA_KV_RANK, MLA_HEADS * (MLA_NOPE + MLA_V)), MLA_KV_RANK ** -0.5),
        'mla_q_g': gain((L, MLA_QK)),
        'mla_k_g': gain((L, MLA_QK)),
        'router_w': nrm((L, D_MODEL, N_EXPERTS), D_MODEL ** -0.5),
        'router_b': nrm((L, N_EXPERTS), 0.01),
        'exp_w_in': nrm((L, N_EXPERTS, D_MODEL, 2 * D_EXPERT), D_MODEL ** -0.5),
        'exp_b_in': nrm((L, N_EXPERTS, 2 * D_EXPERT), 0.01),
        'exp_w_out': nrm((L, N_EXPERTS, D_EXPERT, D_MODEL), D_EXPERT ** -0.5),
        'exp_b_out': nrm((L, N_EXPERTS, D_MODEL), 0.01),
    }


def reference(x, c, ctx, c_ctx, ada_w, ada_b, norm1_g, norm2_g, w_mix_in, w_mix_out,
              conv_dw, conv_dw_b, conv_ln_g, conv_ln_b, conv_pw, pool_w, pool_scale,
              ssm_a_re, ssm_a_im, ssm_log_dt, ssm_b_re, ssm_b_im, ssm_c_re, ssm_c_im,
              ssm_d, ssm_glu_w, ssm_glu_b,
              mla_q_a_g, mla_wq_b, mla_kv_a_g, mla_wkv_b, mla_q_g, mla_k_g,
              router_w, router_b, exp_w_in, exp_b_in, exp_w_out, exp_b_out):
    bsz, n_lat, dim = x.shape
    n_ctx = ctx.shape[1]
    rows = n_lat // GRID_W
    rope = axial_rope_tables(rows, x.dtype)
    silu_c = jax.nn.silu(c)
    silu_cc = jax.nn.silu(c_ctx)
    for l in range(DEPTH):
        last = l == DEPTH - 1
        mod = (silu_c @ ada_w[l] + ada_b[l])[:, None, :]
        mod_c = silu_cc @ ada_w[l] + ada_b[l]
        sh1, sc1, g1, sh2, sc2, g2 = jnp.split(mod, 6, axis=-1)
        sh1c, sc1c, g1c, sh2c, sc2c, g2c = jnp.split(mod_c, 6, axis=-1)

        hx = rmsnorm(x, norm1_g[l]) * (1.0 + sc1) + sh1
        hc = rmsnorm(ctx, norm1_g[l]) * (1.0 + sc1c) + sh1c
        x_conv, x_pool, x_q, x_ssm, x_kv, x_kpe = jnp.split(hx @ w_mix_in[l], SPLIT_AT, axis=-1)
        if last:
            c_ssm, c_kv, c_kpe = jnp.split(hc @ w_mix_in[l][:, OFF_SSM:],
                                           (W_SSM, W_SSM + MLA_KV_RANK), axis=-1)
        else:
            c_conv, c_pool, c_q, c_ssm, c_kv, c_kpe = jnp.split(hc @ w_mix_in[l], SPLIT_AT, axis=-1)

        conv_p = (conv_dw[l], conv_dw_b[l], conv_ln_g[l], conv_ln_b[l], conv_pw[l])
        ya_x = conv_module(x_conv, *conv_p)
        yp_x = multiscale_pool(x_pool, pool_w[l], pool_scale[l])
        ys_x, ys_c = s5_mixer(x_ssm, c_ssm, ssm_a_re[l], ssm_a_im[l], ssm_log_dt[l],
                              ssm_b_re[l], ssm_b_im[l], ssm_c_re[l], ssm_c_im[l],
                              ssm_d[l], ssm_glu_w[l], ssm_glu_b[l], not last)
        qx = apply_axial_rope(mla_query(x_q, mla_q_a_g[l], mla_wq_b[l], mla_q_g[l]), rope)
        kx, vx = mla_key_value(x_kv, x_kpe, mla_kv_a_g[l], mla_wkv_b[l], mla_k_g[l])
        kx = apply_axial_rope(kx, rope)
        kc, vc = mla_key_value(c_kv, c_kpe, mla_kv_a_g[l], mla_wkv_b[l], mla_k_g[l])
        ym_x = blocked_attention(qx, jnp.concatenate([kx, kc], axis=1),
                                 jnp.concatenate([vx, vc], axis=1))
        x = x + g1 * (jnp.concatenate([ya_x, yp_x, ys_x, ym_x], axis=-1) @ w_mix_out[l])
        if not last:
            ya_c = conv_module(c_conv, *conv_p)
            yp_c = multiscale_pool(c_pool, pool_w[l], pool_scale[l])
            qc = mla_query(c_q, mla_q_a_g[l], mla_wq_b[l], mla_q_g[l])
            ym_c = attend(qc, kc, vc).reshape(bsz, n_ctx, W_MLA)
            ctx = ctx + g1c * (jnp.concatenate([ya_c, yp_c, ys_c, ym_c], axis=-1) @ w_mix_out[l])

        moe_p = (router_w[l], router_b[l], exp_w_in[l], exp_b_in[l], exp_w_out[l], exp_b_out[l])
        h2x = (rmsnorm(x, norm2_g[l]) * (1.0 + sc2) + sh2).reshape(bsz * n_lat, dim)
        if last:
            x = x + g2 * moe(h2x, *moe_p).reshape(bsz, n_lat, dim)
        else:
            h2c = (rmsnorm(ctx, norm2_g[l]) * (1.0 + sc2c) + sh2c).reshape(bsz * n_ctx, dim)
            m = moe(jnp.concatenate([h2x, h2c], axis=0), *moe_p)
            x = x + g2 * m[:bsz * n_lat].reshape(bsz, n_lat, dim)
            ctx = ctx + g2c * m[bsz * n_lat:].reshape(bsz, n_ctx, dim)
    return x
```

```python
import functools
import math

import jax
import jax.numpy as jnp
from jax import lax
from jax.experimental import pallas as pl
from jax.experimental.pallas import tpu as pltpu

D_MODEL = 1024
DEPTH = 2
GRID_W = 64
EPS = 1e-6

D_MIX = D_MODEL
W_CONV = D_MIX // 4
W_POOL = D_MIX // 4
W_SSM = D_MIX // 4
W_MLA = D_MIX - W_CONV - W_POOL - W_SSM
CONV_K = 31
POOL_WINDOWS = (2, 4, 8, 16)
POOL_CH = W_POOL // len(POOL_WINDOWS)
SSM_CH = 16
SSM_GROUPS = W_SSM // SSM_CH
SSM_STATE = 64
MLA_V = 64
MLA_HEADS = W_MLA // MLA_V
MLA_NOPE = 64
MLA_ROPE = 32
MLA_QK = MLA_NOPE + MLA_ROPE
MLA_Q_RANK = 192
MLA_KV_RANK = 128
ROPE_AXIS = MLA_ROPE // 2
ROPE_BASE = 10000.0
Q_BLOCK = 128
N_EXPERTS = 32
TOP_K = 4
D_EXPERT = D_MODEL
SWIGLU_LIMIT = 7.0
SWIGLU_ALPHA = 1.702
MOE_BLOCK = 512

OFF_POOL = 2 * W_CONV
OFF_Q = OFF_POOL + W_POOL
OFF_SSM = OFF_Q + MLA_Q_RANK
OFF_KV = OFF_SSM + W_SSM
OFF_KPE = OFF_KV + MLA_KV_RANK
N_IN = OFF_KPE + MLA_ROPE
SPLIT_AT = (OFF_POOL, OFF_Q, OFF_SSM, OFF_KV, OFF_KPE)

F32 = jnp.float32
BF16 = jnp.bfloat16


def rmsnorm(x, g):
    xf = x.astype(jnp.float32)
    y = xf * lax.rsqrt(jnp.mean(xf * xf, axis=-1, keepdims=True) + EPS)
    return y.astype(x.dtype) * g


def layernorm(x, g, b):
    xf = x.astype(jnp.float32)
    mu = jnp.mean(xf, axis=-1, keepdims=True)
    var = jnp.mean(jnp.square(xf - mu), axis=-1, keepdims=True)
    return ((xf - mu) * lax.rsqrt(var + EPS)).astype(x.dtype) * g + b


def conv_module(p, dw, dw_b, ln_g, ln_b, pw):
    val, gate = jnp.split(p, 2, axis=-1)
    u = val * jax.nn.sigmoid(gate)
    u = lax.conv_general_dilated(
        u, dw[:, None, :], window_strides=(1,),
        padding=[(CONV_K // 2, CONV_K // 2)],
        dimension_numbers=('NWC', 'WIO', 'NWC'),
        feature_group_count=u.shape[-1]) + dw_b
    u = jax.nn.silu(layernorm(u, ln_g, ln_b))
    return u @ pw


def multiscale_pool(u, w, scale):
    bsz, length, _ = u.shape
    n_g = len(POOL_WINDOWS)
    ug = u.reshape(bsz, length, n_g, POOL_CH)
    cs = jnp.concatenate([jnp.zeros((bsz, 1, n_g, POOL_CH), jnp.float32),
                          jnp.cumsum(ug.astype(jnp.float32), axis=1)], axis=1)
    t = jnp.arange(length)
    groups = []
    for g, win in enumerate(POOL_WINDOWS):
        lo = jnp.maximum(t - win // 2, 0)
        hi = jnp.minimum(t + win - 1 - win // 2, length - 1)
        csg = cs[:, :, g]
        cnt = (hi - lo + 1).astype(jnp.float32)[None, :, None]
        mean = (csg[:, hi + 1] - csg[:, lo]) / cnt
        groups.append(mean.astype(u.dtype) - ug[:, :, g])
    pooled = jnp.stack(groups, axis=2)
    y = jnp.einsum('blgc,gcd->blgd', pooled, w)
    return y.reshape(bsz, length, W_POOL) * scale


def s5_discretise(a_re, a_im, log_dt, b_re, b_im):
    a_re = jnp.minimum(a_re.astype(jnp.float32), -1e-4)
    a_im = a_im.astype(jnp.float32)
    dt = jnp.exp(log_dt.astype(jnp.float32))[:, None]
    mag = jnp.exp(a_re * dt)
    ab_re = mag * jnp.cos(a_im * dt)
    ab_im = mag * jnp.sin(a_im * dt)
    den = a_re * a_re + a_im * a_im
    f_re = ((ab_re - 1.0) * a_re + ab_im * a_im) / den
    f_im = (ab_im * a_re - (ab_re - 1.0) * a_im) / den
    b_re = b_re.astype(jnp.float32)
    b_im = b_im.astype(jnp.float32)
    bb_re = f_re[..., None] * b_re - f_im[..., None] * b_im
    bb_im = f_re[..., None] * b_im + f_im[..., None] * b_re
    return ab_re, ab_im, bb_re, bb_im


def _complex_affine_combine(e1, e2):
    a1r, a1i, b1r, b1i = e1
    a2r, a2i, b2r, b2i = e2
    return (a1r * a2r - a1i * a2i, a1r * a2i + a1i * a2r,
            a2r * b1r - a2i * b1i + b2r, a2r * b1i + a2i * b1r + b2i)


def diag_scan(ab_re, ab_im, bb_re, bb_im, u, h0_re, h0_im, reverse):
    bu_re = jnp.einsum('blgc,gpc->blgp', u, bb_re)
    bu_im = jnp.einsum('blgc,gpc->blgp', u, bb_im)
    first = u.shape[1] - 1 if reverse else 0
    bu_re = bu_re.at[:, first].add(ab_re * h0_re - ab_im * h0_im)
    bu_im = bu_im.at[:, first].add(ab_re * h0_im + ab_im * h0_re)
    a_re = jnp.broadcast_to(ab_re, bu_re.shape)
    a_im = jnp.broadcast_to(ab_im, bu_im.shape)
    _, _, h_re, h_im = lax.associative_scan(
        _complex_affine_combine, (a_re, a_im, bu_re, bu_im), reverse=reverse, axis=1)
    return h_re, h_im


def s5_readout(h_re, h_im, c_re, c_im):
    return (jnp.einsum('blgp,gcp->blgc', h_re, c_re.astype(jnp.float32))
            - jnp.einsum('blgp,gcp->blgc', h_im, c_im.astype(jnp.float32)))


def s5_mixer(ux, uc, a_re, a_im, log_dt, b_re, b_im, c_re, c_im, d, glu_w, glu_b, need_ctx):
    def grouped(u):
        return u.astype(jnp.float32).reshape(u.shape[0], u.shape[1], SSM_GROUPS, SSM_CH)
    gx, gc = grouped(ux), grouped(uc)
    zero = jnp.zeros((ux.shape[0], SSM_GROUPS, SSM_STATE), jnp.float32)
    d_g = d.astype(jnp.float32).reshape(SSM_GROUPS, SSM_CH)
    yx = d_g * gx
    yc = d_g * gc if need_ctx else None
    for direction in range(2):
        reverse = direction == 1
        ab_re, ab_im, bb_re, bb_im = s5_discretise(
            a_re[direction], a_im[direction], log_dt[direction], b_re[direction], b_im[direction])
        hc_re, hc_im = diag_scan(ab_re, ab_im, bb_re, bb_im, gc, zero, zero, reverse)
        end = 0 if reverse else -1
        hx_re, hx_im = diag_scan(ab_re, ab_im, bb_re, bb_im, gx,
                                 hc_re[:, end], hc_im[:, end], reverse)
        yx = yx + s5_readout(hx_re, hx_im, c_re[direction], c_im[direction])
        if need_ctx:
            yc = yc + s5_readout(hc_re, hc_im, c_re[direction], c_im[direction])

    def glu(y, like):
        z = jax.nn.gelu(y.reshape(like.shape)).astype(like.dtype)
        return z * jax.nn.sigmoid(z @ glu_w + glu_b)
    return glu(yx, ux), (glu(yc, uc) if need_ctx else None)


def axial_rope_tables(rows, dtype):
    row = jnp.repeat(jnp.arange(rows), GRID_W).astype(jnp.float32)
    col = jnp.tile(jnp.arange(GRID_W), rows).astype(jnp.float32)
    inv = ROPE_BASE ** (-jnp.arange(0, ROPE_AXIS, 2, dtype=jnp.float32) / ROPE_AXIS)
    ang_r = row[:, None] * inv
    ang_c = col[:, None] * inv
    return tuple(t.astype(dtype)[:, None, :] for t in
                 (jnp.cos(ang_r), jnp.sin(ang_r), jnp.cos(ang_c), jnp.sin(ang_c)))


def _rotate_half(x, cos, sin):
    x1, x2 = jnp.split(x, 2, axis=-1)
    return jnp.concatenate([x1 * cos - x2 * sin, x2 * cos + x1 * sin], axis=-1)


def apply_axial_rope(t, tables):
    cos_r, sin_r, cos_c, sin_c = tables
    return jnp.concatenate([
        t[..., :MLA_NOPE],
        _rotate_half(t[..., MLA_NOPE:MLA_NOPE + ROPE_AXIS], cos_r, sin_r),
        _rotate_half(t[..., MLA_NOPE + ROPE_AXIS:], cos_c, sin_c)], axis=-1)


def mla_query(cq, q_a_g, wq_b, q_g):
    q = rmsnorm(cq, q_a_g) @ wq_b
    return rmsnorm(q.reshape(*cq.shape[:-1], MLA_HEADS, MLA_QK), q_g)


def mla_key_value(ckv, kpe, kv_a_g, wkv_b, k_g):
    lead = ckv.shape[:-1]
    kv = (rmsnorm(ckv, kv_a_g) @ wkv_b).reshape(*lead, MLA_HEADS, MLA_NOPE + MLA_V)
    k_pe = jnp.broadcast_to(kpe[..., None, :], (*lead, MLA_HEADS, MLA_ROPE))
    k = rmsnorm(jnp.concatenate([kv[..., :MLA_NOPE], k_pe], axis=-1), k_g)
    return k, kv[..., MLA_NOPE:]


def attend(q, k, v):
    s = jnp.einsum('bqhd,bkhd->bhqk', q, k).astype(jnp.float32) * (MLA_QK ** -0.5)
    p = jax.nn.softmax(s, axis=-1).astype(v.dtype)
    return jnp.einsum('bhqk,bkhv->bqhv', p, v)


def blocked_attention(q, k, v):
    bsz, length = q.shape[:2]
    qb = q.reshape(bsz, length // Q_BLOCK, Q_BLOCK, MLA_HEADS, MLA_QK).swapaxes(0, 1)
    o = lax.map(lambda qi: attend(qi, k, v), qb)
    return o.swapaxes(0, 1).reshape(bsz, length, MLA_HEADS * MLA_V)


def _expert_block_kernel(be_ref, nu_ref, x_ref, win_ref, bin_ref, wout_ref, bout_ref, o_ref):
    i = pl.program_id(0)

    @pl.when(i < nu_ref[0])
    def _():
        gu = jnp.dot(x_ref[...], win_ref[0], preferred_element_type=F32) + bin_ref[0]
        gate = jnp.minimum(gu[:, :D_EXPERT], SWIGLU_LIMIT)
        up = jnp.clip(gu[:, D_EXPERT:], -SWIGLU_LIMIT, SWIGLU_LIMIT)
        act = (up + 1.0) * gate * jax.nn.sigmoid(SWIGLU_ALPHA * gate)
        o_ref[...] = jnp.dot(act.astype(BF16), wout_ref[0], preferred_element_type=F32) + bout_ref[0]

    @pl.when(i >= nu_ref[0])
    def _():
        o_ref[...] = jnp.zeros_like(o_ref)


def expert_blocks(xs, block_e, n_used, w_in, b_in, w_out, b_out):
    n_pad, dim = xs.shape
    n_blocks = n_pad // MOE_BLOCK

    def row_map(i, be, nu):
        return (jnp.minimum(i, nu[0] - 1), 0)

    def exp_map3(i, be, nu):
        return (be[i], 0, 0)

    return pl.pallas_call(
        _expert_block_kernel,
        out_shape=jax.ShapeDtypeStruct((n_pad, dim), F32),
        grid_spec=pltpu.PrefetchScalarGridSpec(
            num_scalar_prefetch=2,
            grid=(n_blocks,),
            in_specs=[
                pl.BlockSpec((MOE_BLOCK, dim), row_map),
                pl.BlockSpec((1, dim, 2 * D_EXPERT), exp_map3),
                pl.BlockSpec((1, 1, 2 * D_EXPERT), exp_map3),
                pl.BlockSpec((1, D_EXPERT, dim), exp_map3),
                pl.BlockSpec((1, 1, dim), exp_map3),
            ],
            out_specs=pl.BlockSpec((MOE_BLOCK, dim), lambda i, be, nu: (i, 0)),
        ),
        compiler_params=pltpu.CompilerParams(dimension_semantics=("arbitrary",)),
        name="expert_blocks",
    )(block_e, n_used, xs, w_in, b_in[:, None, :], w_out, b_out[:, None, :])


def moe(h, router_w, router_b, w_in, b_in, w_out, b_out):
    n_tok, dim = h.shape
    logits = (h @ router_w + router_b).astype(jnp.float32)
    top_v, top_i = lax.top_k(logits, TOP_K)
    top_w = jax.nn.softmax(top_v, axis=-1)
    flat_e = top_i.reshape(-1)
    n_asg = n_tok * TOP_K
    flat_t = jnp.arange(n_asg, dtype=jnp.int32) // TOP_K
    order = jnp.argsort(flat_e)
    sorted_e = flat_e[order]
    counts = jnp.bincount(flat_e, length=N_EXPERTS)
    padded = (counts + MOE_BLOCK - 1) // MOE_BLOCK * MOE_BLOCK
    pad_end = jnp.cumsum(padded)
    pad_start = pad_end - padded
    start = jnp.cumsum(counts) - counts
    dest = (pad_start[sorted_e] + (jnp.arange(n_asg) - start[sorted_e])).astype(jnp.int32)
    n_blocks = -(-n_asg // MOE_BLOCK) + N_EXPERTS
    n_pad = n_blocks * MOE_BLOCK
    tok_pad = jnp.zeros((n_pad,), jnp.int32).at[dest].set(flat_t[order])
    pos = jnp.zeros((n_asg,), jnp.int32).at[order].set(dest)
    block_e = jnp.minimum(jnp.searchsorted(pad_end, jnp.arange(n_blocks) * MOE_BLOCK, side='right'),
                          N_EXPERTS - 1).astype(jnp.int32)
    n_used = (pad_end[-1] // MOE_BLOCK).astype(jnp.int32).reshape(1)
    xs = h.astype(BF16)[tok_pad]
    ys = expert_blocks(xs, block_e, n_used, w_in.astype(BF16), b_in, w_out.astype(BF16), b_out)
    yk = ys[pos].reshape(n_tok, TOP_K, dim)
    return jnp.sum(yk * top_w[:, :, None], axis=1)


def kernel(x, c, ctx, c_ctx, ada_w, ada_b, norm1_g, norm2_g, w_mix_in, w_mix_out,
           conv_dw, conv_dw_b, conv_ln_g, conv_ln_b, conv_pw, pool_w, pool_scale,
           ssm_a_re, ssm_a_im, ssm_log_dt, ssm_b_re, ssm_b_im, ssm_c_re, ssm_c_im,
           ssm_d, ssm_glu_w, ssm_glu_b,
           mla_q_a_g, mla_wq_b, mla_kv_a_g, mla_wkv_b, mla_q_g, mla_k_g,
           router_w, router_b, exp_w_in, exp_b_in, exp_w_out, exp_b_out):
    bsz, n_lat, dim = x.shape
    n_ctx = ctx.shape[1]
    rows = n_lat // GRID_W
    rope = axial_rope_tables(rows, x.dtype)
    silu_c = jax.nn.silu(c)
    silu_cc = jax.nn.silu(c_ctx)
    for l in range(DEPTH):
        last = l == DEPTH - 1
        mod = (silu_c @ ada_w[l] + ada_b[l])[:, None, :]
        mod_c = silu_cc @ ada_w[l] + ada_b[l]
        sh1, sc1, g1, sh2, sc2, g2 = jnp.split(mod, 6, axis=-1)
        sh1c, sc1c, g1c, sh2c, sc2c, g2c = jnp.split(mod_c, 6, axis=-1)

        hx = rmsnorm(x, norm1_g[l]) * (1.0 + sc1) + sh1
        hc = rmsnorm(ctx, norm1_g[l]) * (1.0 + sc1c) + sh1c
        x_conv, x_pool, x_q, x_ssm, x_kv, x_kpe = jnp.split(hx @ w_mix_in[l], SPLIT_AT, axis=-1)
        if last:
            c_ssm, c_kv, c_kpe = jnp.split(hc @ w_mix_in[l][:, OFF_SSM:],
                                           (W_SSM, W_SSM + MLA_KV_RANK), axis=-1)
        else:
            c_conv, c_pool, c_q, c_ssm, c_kv, c_kpe = jnp.split(hc @ w_mix_in[l], SPLIT_AT, axis=-1)

        conv_p = (conv_dw[l], conv_dw_b[l], conv_ln_g[l], conv_ln_b[l], conv_pw[l])
        ya_x = conv_module(x_conv, *conv_p)
        yp_x = multiscale_pool(x_pool, pool_w[l], pool_scale[l])
        ys_x, ys_c = s5_mixer(x_ssm, c_ssm, ssm_a_re[l], ssm_a_im[l], ssm_log_dt[l],
                              ssm_b_re[l], ssm_b_im[l], ssm_c_re[l], ssm_c_im[l],
                              ssm_d[l], ssm_glu_w[l], ssm_glu_b[l], not last)
        qx = apply_axial_rope(mla_query(x_q, mla_q_a_g[l], mla_wq_b[l], mla_q_g[l]), rope)
        kx, vx = mla_key_value(x_kv, x_kpe, mla_kv_a_g[l], mla_wkv_b[l], mla_k_g[l])
        kx = apply_axial_rope(kx, rope)
        kc, vc = mla_key_value(c_kv, c_kpe, mla_kv_a_g[l], mla_wkv_b[l], mla_k_g[l])
        ym_x = blocked_attention(qx, jnp.concatenate([kx, kc], axis=1),
                                 jnp.concatenate([vx, vc], axis=1))
        x = x + g1 * (jnp.concatenate([ya_x, yp_x, ys_x, ym_x], axis=-1) @ w_mix_out[l])
        if not last:
            ya_c = conv_module(c_conv, *conv_p)
            yp_c = multiscale_pool(c_pool, pool_w[l], pool_scale[l])
            qc = mla_query(c_q, mla_q_a_g[l], mla_wq_b[l], mla_q_g[l])
            ym_c = attend(qc, kc, vc).reshape(bsz, n_ctx, W_MLA)
            ctx = ctx + g1c * (jnp.concatenate([ya_c, yp_c, ys_c, ym_c], axis=-1) @ w_mix_out[l])

        moe_p = (router_w[l], router_b[l], exp_w_in[l], exp_b_in[l], exp_w_out[l], exp_b_out[l])
        h2x = (rmsnorm(x, norm2_g[l]) * (1.0 + sc2) + sh2).reshape(bsz * n_lat, dim)
        if last:
            x = x + g2 * moe(h2x, *moe_p).reshape(bsz, n_lat, dim)
        else:
            h2c = (rmsnorm(ctx, norm2_g[l]) * (1.0 + sc2c) + sh2c).reshape(bsz * n_ctx, dim)
            m = moe(jnp.concatenate([h2x, h2c], axis=0), *moe_p)
            x = x + g2 * m[:bsz * n_lat].reshape(bsz, n_lat, dim)
            ctx = ctx + g2c * m[bsz * n_lat:].reshape(bsz, n_ctx, dim)
    return x
```

```python
import functools
import math

import jax
import jax.numpy as jnp
from jax import lax
from jax.experimental import pallas as pl
from jax.experimental.pallas import tpu as pltpu

D_MODEL = 1024
DEPTH = 2
GRID_W = 64
EPS = 1e-6

D_MIX = D_MODEL
W_CONV = D_MIX // 4
W_POOL = D_MIX // 4
W_SSM = D_MIX // 4
W_MLA = D_MIX - W_CONV - W_POOL - W_SSM
CONV_K = 31
POOL_WINDOWS = (2, 4, 8, 16)
POOL_CH = W_POOL // len(POOL_WINDOWS)
SSM_CH = 16
SSM_GROUPS = W_SSM // SSM_CH
SSM_STATE = 64
MLA_V = 64
MLA_HEADS = W_MLA // MLA_V
MLA_NOPE = 64
MLA_ROPE = 32
MLA_QK = MLA_NOPE + MLA_ROPE
MLA_Q_RANK = 192
MLA_KV_RANK = 128
ROPE_AXIS = MLA_ROPE // 2
ROPE_BASE = 10000.0
Q_BLOCK = 128
N_EXPERTS = 32
TOP_K = 4
D_EXPERT = D_MODEL
SWIGLU_LIMIT = 7.0
SWIGLU_ALPHA = 1.702
MOE_BLOCK = 512

OFF_POOL = 2 * W_CONV
OFF_Q = OFF_POOL + W_POOL
OFF_SSM = OFF_Q + MLA_Q_RANK
OFF_KV = OFF_SSM + W_SSM
OFF_KPE = OFF_KV + MLA_KV_RANK
N_IN = OFF_KPE + MLA_ROPE
SPLIT_AT = (OFF_POOL, OFF_Q, OFF_SSM, OFF_KV, OFF_KPE)

F32 = jnp.float32
BF16 = jnp.bfloat16


def rmsnorm(x, g):
    xf = x.astype(jnp.float32)
    y = xf * lax.rsqrt(jnp.mean(xf * xf, axis=-1, keepdims=True) + EPS)
    return y.astype(x.dtype) * g


def layernorm(x, g, b):
    xf = x.astype(jnp.float32)
    mu = jnp.mean(xf, axis=-1, keepdims=True)
    var = jnp.mean(jnp.square(xf - mu), axis=-1, keepdims=True)
    return ((xf - mu) * lax.rsqrt(var + EPS)).astype(x.dtype) * g + b


def conv_module(p, dw, dw_b, ln_g, ln_b, pw):
    val, gate = jnp.split(p, 2, axis=-1)
    u = val * jax.nn.sigmoid(gate)
    u = lax.conv_general_dilated(
        u, dw[:, None, :], window_strides=(1,),
        padding=[(CONV_K // 2, CONV_K // 2)],
        dimension_numbers=('NWC', 'WIO', 'NWC'),
        feature_group_count=u.shape[-1]) + dw_b
    u = jax.nn.silu(layernorm(u, ln_g, ln_b))
    return u @ pw


def multiscale_pool(u, w, scale):
    bsz, length, _ = u.shape
    n_g = len(POOL_WINDOWS)
    ug = u.reshape(bsz, length, n_g, POOL_CH)
    cs = jnp.concatenate([jnp.zeros((bsz, 1, n_g, POOL_CH), jnp.float32),
                          jnp.cumsum(ug.astype(jnp.float32), axis=1)], axis=1)
    t = jnp.arange(length)
    groups = []
    for g, win in enumerate(POOL_WINDOWS):
        lo = jnp.maximum(t - win // 2, 0)
        hi = jnp.minimum(t + win - 1 - win // 2, length - 1)
        csg = cs[:, :, g]
        cnt = (hi - lo + 1).astype(jnp.float32)[None, :, None]
        mean = (csg[:, hi + 1] - csg[:, lo]) / cnt
        groups.append(mean.astype(u.dtype) - ug[:, :, g])
    pooled = jnp.stack(groups, axis=2)
    y = jnp.einsum('blgc,gcd->blgd', pooled, w)
    return y.reshape(bsz, length, W_POOL) * scale


S5_NS = SSM_GROUPS * SSM_STATE
S5_TC = 128
S5_BG = 8
S5_LANES = 512


def _s5_scan_kernel(u_ref, perm_ref, a_ref, bd_ref, cd_ref, *rest, reverse, final):
    if final:
        yp_ref, permt_ref, d_ref, gw_ref, gb_ref, y_ref, bu_s, hh_s, h_s = rest
    else:
        y_ref, bu_s, hh_s, h_s = rest
    tc = u_ref.shape[1]
    rows = S5_BG * tc
    j = pl.program_id(1)

    @pl.when(j == 0)
    def _():
        h_s[...] = jnp.zeros_like(h_s)

    u = u_ref[...].reshape(rows, W_SSM)
    u_hi = u.astype(BF16)
    if final:
        u_lo = (u - u_hi.astype(F32)).astype(BF16)
        u_tm = jnp.dot(perm_ref[...], jnp.concatenate([u_hi, u_lo], axis=1), preferred_element_type=F32)
        u_hi_tm = u_tm[:, :W_SSM].astype(BF16)
        u_f32_tm = u_tm[:, :W_SSM] + u_tm[:, W_SSM:]
    else:
        u_hi_tm = jnp.dot(perm_ref[...], u_hi, preferred_element_type=F32).astype(BF16)
    bu_s[...] = jnp.dot(u_hi_tm, bd_ref[...], preferred_element_type=F32)

    for hf in range(S5_NS // S5_LANES):
        re_cols = pl.ds(hf * S5_LANES, S5_LANES)
        im_cols = pl.ds(S5_NS + hf * S5_LANES, S5_LANES)
        a_re = jnp.broadcast_to(a_ref[0:1, hf * S5_LANES:(hf + 1) * S5_LANES], (S5_BG, S5_LANES))
        a_im = jnp.broadcast_to(a_ref[1:2, hf * S5_LANES:(hf + 1) * S5_LANES], (S5_BG, S5_LANES))

        def step(i, carry):
            h_re, h_im = carry
            t = (tc - 1 - i) if reverse else i
            r8 = pl.ds(pl.multiple_of(t * S5_BG, S5_BG), S5_BG)
            n_re = a_re * h_re - a_im * h_im + bu_s[r8, re_cols]
            n_im = a_re * h_im + a_im * h_re + bu_s[r8, im_cols]
            hh_s[r8, re_cols] = n_re
            hh_s[r8, im_cols] = n_im
            return n_re, n_im

        h_re, h_im = lax.fori_loop(0, tc, step, (h_s[:, re_cols], h_s[:, im_cols]), unroll=8)
        h_s[:, re_cols] = h_re
        h_s[:, im_cols] = h_im

    y = jnp.dot(hh_s[...].astype(BF16), cd_ref[...], preferred_element_type=F32)
    if final:
        y = y + yp_ref[...] + d_ref[...] * u_f32_tm
        z = jax.nn.gelu(y)
        y = z * jax.nn.sigmoid(jnp.dot(z.astype(BF16), gw_ref[...], preferred_element_type=F32) + gb_ref[...])
        y = jnp.dot(permt_ref[...], y.astype(BF16), preferred_element_type=F32)
        y_ref[...] = y.astype(BF16).reshape(S5_BG, tc, W_SSM)
    else:
        y_ref[...] = y


def s5_direction(u_all, perm, a, bd, cd, n_ctx, reverse, final_args=None):
    bsz, n_all, _ = u_all.shape
    tc = S5_TC
    n_chunks = n_all // tc
    ctx_chunks = n_ctx // tc
    rows = S5_BG * tc
    assert n_all % tc == 0 and n_ctx % tc == 0 and bsz % S5_BG == 0

    if reverse:
        def chunk(j):
            return jnp.where(j < ctx_chunks, ctx_chunks - 1 - j, n_chunks - 1 + ctx_chunks - j)
    else:
        def chunk(j):
            return j
    seq_spec = pl.BlockSpec((S5_BG, tc, W_SSM), lambda g, j: (g, chunk(j), 0))
    part_spec = pl.BlockSpec((None, None, rows, W_SSM), lambda g, j: (g, chunk(j), 0, 0))

    def full(arr):
        return pl.BlockSpec(arr.shape, lambda g, j: (0,) * arr.ndim)
    in_specs = [seq_spec, full(perm), full(a), full(bd), full(cd)]
    args = [u_all, perm, a, bd, cd]
    if final_args is not None:
        y_prev, d, glu_w, glu_b = final_args
        perm_t = perm.T
        in_specs += [part_spec, full(perm_t), full(d), full(glu_w), full(glu_b)]
        args += [y_prev, perm_t, d, glu_w, glu_b]
        out_shape = jax.ShapeDtypeStruct(u_all.shape, BF16)
        out_spec = seq_spec
    else:
        out_shape = jax.ShapeDtypeStruct((bsz // S5_BG, n_chunks, rows, W_SSM), F32)
        out_spec = part_spec
    return pl.pallas_call(
        functools.partial(_s5_scan_kernel, reverse=reverse, final=final_args is not None),
        out_shape=out_shape,
        grid=(bsz // S5_BG, n_chunks),
        in_specs=in_specs,
        out_specs=out_spec,
        scratch_shapes=[pltpu.VMEM((rows, 2 * S5_NS), F32), pltpu.VMEM((rows, 2 * S5_NS), F32),
                        pltpu.VMEM((S5_BG, 2 * S5_NS), F32)],
        compiler_params=pltpu.CompilerParams(dimension_semantics=("arbitrary", "arbitrary")),
        name="s5_scan_bwd" if reverse else "s5_scan_fwd",
    )(*args)


def s5_time_major_perm(tc):
    r = jnp.arange(S5_BG * tc)
    src = (r % S5_BG) * tc + r // S5_BG
    return (src[:, None] == r[None, :]).astype(BF16)


def s5_params(a_re, a_im, log_dt, b_re, b_im, c_re, c_im):
    a_re = jnp.minimum(a_re.astype(F32), -1e-4)
    a_im = a_im.astype(F32)
    dt = jnp.exp(log_dt.astype(F32))[:, None]
    mag = jnp.exp(a_re * dt)
    ab_re = mag * jnp.cos(a_im * dt)
    ab_im = mag * jnp.sin(a_im * dt)
    den = a_re * a_re + a_im * a_im
    f_re = ((ab_re - 1.0) * a_re + ab_im * a_im) / den
    f_im = (ab_im * a_re - (ab_re - 1.0) * a_im) / den
    bb_re = f_re[..., None] * b_re - f_im[..., None] * b_im
    bb_im = f_re[..., None] * b_im + f_im[..., None] * b_re
    eye = jnp.eye(SSM_GROUPS, dtype=F32)

    def in_map(bb):
        return jnp.einsum('gpc,gh->gchp', bb, eye).reshape(W_SSM, S5_NS)

    def out_map(cc):
        return jnp.einsum('gcp,gh->gphc', cc, eye).reshape(S5_NS, W_SSM)
    a = jnp.stack([ab_re.reshape(-1), ab_im.reshape(-1)])
    bd = jnp.concatenate([in_map(bb_re), in_map(bb_im)], axis=1).astype(BF16)
    cd = jnp.concatenate([out_map(c_re.astype(F32)), -out_map(c_im.astype(F32))], axis=0).astype(BF16)
    return a, bd, cd


def s5_mixer_pallas(u_all, n_ctx, a_re, a_im, log_dt, b_re, b_im, c_re, c_im, d, glu_w, glu_b):
    pf = s5_params(a_re[0], a_im[0], log_dt[0], b_re[0], b_im[0], c_re[0], c_im[0])
    pb = s5_params(a_re[1], a_im[1], log_dt[1], b_re[1], b_im[1], c_re[1], c_im[1])
    perm = s5_time_major_perm(S5_TC)
    y_f = s5_direction(u_all, perm, *pf, n_ctx, False)
    return s5_direction(u_all, perm, *pb, n_ctx, True,
                        (y_f, d.reshape(1, W_SSM).astype(F32), glu_w.astype(BF16), glu_b.reshape(1, W_SSM)))


def axial_rope_tables(rows, dtype):
    row = jnp.repeat(jnp.arange(rows), GRID_W).astype(jnp.float32)
    col = jnp.tile(jnp.arange(GRID_W), rows).astype(jnp.float32)
    inv = ROPE_BASE ** (-jnp.arange(0, ROPE_AXIS, 2, dtype=jnp.float32) / ROPE_AXIS)
    ang_r = row[:, None] * inv
    ang_c = col[:, None] * inv
    return tuple(t.astype(dtype)[:, None, :] for t in
                 (jnp.cos(ang_r), jnp.sin(ang_r), jnp.cos(ang_c), jnp.sin(ang_c)))


def _rotate_half(x, cos, sin):
    x1, x2 = jnp.split(x, 2, axis=-1)
    return jnp.concatenate([x1 * cos - x2 * sin, x2 * cos + x1 * sin], axis=-1)


def apply_axial_rope(t, tables):
    cos_r, sin_r, cos_c, sin_c = tables
    return jnp.concatenate([
        t[..., :MLA_NOPE],
        _rotate_half(t[..., MLA_NOPE:MLA_NOPE + ROPE_AXIS], cos_r, sin_r),
        _rotate_half(t[..., MLA_NOPE + ROPE_AXIS:], cos_c, sin_c)], axis=-1)


def mla_query(cq, q_a_g, wq_b, q_g):
    q = rmsnorm(cq, q_a_g) @ wq_b
    return rmsnorm(q.reshape(*cq.shape[:-1], MLA_HEADS, MLA_QK), q_g)


def mla_key_value(ckv, kpe, kv_a_g, wkv_b, k_g):
    lead = ckv.shape[:-1]
    kv = (rmsnorm(ckv, kv_a_g) @ wkv_b).reshape(*lead, MLA_HEADS, MLA_NOPE + MLA_V)
    k_pe = jnp.broadcast_to(kpe[..., None, :], (*lead, MLA_HEADS, MLA_ROPE))
    k = rmsnorm(jnp.concatenate([kv[..., :MLA_NOPE], k_pe], axis=-1), k_g)
    return k, kv[..., MLA_NOPE:]


def attend(q, k, v):
    s = jnp.einsum('bqhd,bkhd->bhqk', q, k).astype(jnp.float32) * (MLA_QK ** -0.5)
    p = jax.nn.softmax(s, axis=-1).astype(v.dtype)
    return jnp.einsum('bhqk,bkhv->bqhv', p, v)


def blocked_attention(q, k, v):
    bsz, length = q.shape[:2]
    qb = q.reshape(bsz, length // Q_BLOCK, Q_BLOCK, MLA_HEADS, MLA_QK).swapaxes(0, 1)
    o = lax.map(lambda qi: attend(qi, k, v), qb)
    return o.swapaxes(0, 1).reshape(bsz, length, MLA_HEADS * MLA_V)


def _expert_block_kernel(be_ref, nu_ref, x_ref, win_ref, bin_ref, wout_ref, bout_ref, o_ref):
    i = pl.program_id(0)

    @pl.when(i < nu_ref[0])
    def _():
        gu = jnp.dot(x_ref[...], win_ref[0], preferred_element_type=F32) + bin_ref[0]
        gate = jnp.minimum(gu[:, :D_EXPERT], SWIGLU_LIMIT)
        up = jnp.clip(gu[:, D_EXPERT:], -SWIGLU_LIMIT, SWIGLU_LIMIT)
        act = (up + 1.0) * gate * jax.nn.sigmoid(SWIGLU_ALPHA * gate)
        o_ref[...] = jnp.dot(act.astype(BF16), wout_ref[0], preferred_element_type=F32) + bout_ref[0]

    @pl.when(i >= nu_ref[0])
    def _():
        o_ref[...] = jnp.zeros_like(o_ref)


def expert_blocks(xs, block_e, n_used, w_in, b_in, w_out, b_out):
    n_pad, dim = xs.shape
    n_blocks = n_pad // MOE_BLOCK

    def row_map(i, be, nu):
        return (jnp.minimum(i, nu[0] - 1), 0)

    def exp_map3(i, be, nu):
        return (be[i], 0, 0)

    return pl.pallas_call(
        _expert_block_kernel,
        out_shape=jax.ShapeDtypeStruct((n_pad, dim), F32),
        grid_spec=pltpu.PrefetchScalarGridSpec(
            num_scalar_prefetch=2,
            grid=(n_blocks,),
            in_specs=[
                pl.BlockSpec((MOE_BLOCK, dim), row_map),
                pl.BlockSpec((1, dim, 2 * D_EXPERT), exp_map3),
                pl.BlockSpec((1, 1, 2 * D_EXPERT), exp_map3),
                pl.BlockSpec((1, D_EXPERT, dim), exp_map3),
                pl.BlockSpec((1, 1, dim), exp_map3),
            ],
            out_specs=pl.BlockSpec((MOE_BLOCK, dim), lambda i, be, nu: (i, 0)),
        ),
        compiler_params=pltpu.CompilerParams(dimension_semantics=("arbitrary",)),
        name="expert_blocks",
    )(block_e, n_used, xs, w_in, b_in[:, None, :], w_out, b_out[:, None, :])


def moe(h, router_w, router_b, w_in, b_in, w_out, b_out):
    n_tok, dim = h.shape
    logits = (h @ router_w + router_b).astype(jnp.float32)
    top_v, top_i = lax.top_k(logits, TOP_K)
    top_w = jax.nn.softmax(top_v, axis=-1)
    flat_e = top_i.reshape(-1)
    n_asg = n_tok * TOP_K
    flat_t = jnp.arange(n_asg, dtype=jnp.int32) // TOP_K
    order = jnp.argsort(flat_e)
    sorted_e = flat_e[order]
    counts = jnp.bincount(flat_e, length=N_EXPERTS)
    padded = (counts + MOE_BLOCK - 1) // MOE_BLOCK * MOE_BLOCK
    pad_end = jnp.cumsum(padded)
    pad_start = pad_end - padded
    start = jnp.cumsum(counts) - counts
    dest = (pad_start[sorted_e] + (jnp.arange(n_asg) - start[sorted_e])).astype(jnp.int32)
    n_blocks = -(-n_asg // MOE_BLOCK) + N_EXPERTS
    n_pad = n_blocks * MOE_BLOCK
    tok_pad = jnp.zeros((n_pad,), jnp.int32).at[dest].set(flat_t[order])
    pos = jnp.zeros((n_asg,), jnp.int32).at[order].set(dest)
    block_e = jnp.minimum(jnp.searchsorted(pad_end, jnp.arange(n_blocks) * MOE_BLOCK, side='right'),
                          N_EXPERTS - 1).astype(jnp.int32)
    n_used = (pad_end[-1] // MOE_BLOCK).astype(jnp.int32).reshape(1)
    xs = h.astype(BF16)[tok_pad]
    ys = expert_blocks(xs, block_e, n_used, w_in.astype(BF16), b_in, w_out.astype(BF16), b_out)
    yk = ys[pos].reshape(n_tok, TOP_K, dim)
    return jnp.sum(yk * top_w[:, :, None], axis=1)


def kernel(x, c, ctx, c_ctx, ada_w, ada_b, norm1_g, norm2_g, w_mix_in, w_mix_out,
           conv_dw, conv_dw_b, conv_ln_g, conv_ln_b, conv_pw, pool_w, pool_scale,
           ssm_a_re, ssm_a_im, ssm_log_dt, ssm_b_re, ssm_b_im, ssm_c_re, ssm_c_im,
           ssm_d, ssm_glu_w, ssm_glu_b,
           mla_q_a_g, mla_wq_b, mla_kv_a_g, mla_wkv_b, mla_q_g, mla_k_g,
           router_w, router_b, exp_w_in, exp_b_in, exp_w_out, exp_b_out):
    bsz, n_lat, dim = x.shape
    n_ctx = ctx.shape[1]
    rows = n_lat // GRID_W
    rope = axial_rope_tables(rows, x.dtype)
    silu_c = jax.nn.silu(c)
    silu_cc = jax.nn.silu(c_ctx)
    for l in range(DEPTH):
        last = l == DEPTH - 1
        mod = (silu_c @ ada_w[l] + ada_b[l])[:, None, :]
        mod_c = silu_cc @ ada_w[l] + ada_b[l]
        sh1, sc1, g1, sh2, sc2, g2 = jnp.split(mod, 6, axis=-1)
        sh1c, sc1c, g1c, sh2c, sc2c, g2c = jnp.split(mod_c, 6, axis=-1)

        hx = rmsnorm(x, norm1_g[l]) * (1.0 + sc1) + sh1
        hc = rmsnorm(ctx, norm1_g[l]) * (1.0 + sc1c) + sh1c
        x_conv, x_pool, x_q, x_ssm, x_kv, x_kpe = jnp.split(hx @ w_mix_in[l], SPLIT_AT, axis=-1)
        if last:
            c_ssm, c_kv, c_kpe = jnp.split(hc @ w_mix_in[l][:, OFF_SSM:],
                                           (W_SSM, W_SSM + MLA_KV_RANK), axis=-1)
        else:
            c_conv, c_pool, c_q, c_ssm, c_kv, c_kpe = jnp.split(hc @ w_mix_in[l], SPLIT_AT, axis=-1)

        conv_p = (conv_dw[l], conv_dw_b[l], conv_ln_g[l], conv_ln_b[l], conv_pw[l])
        ya_x = conv_module(x_conv, *conv_p)
        yp_x = multiscale_pool(x_pool, pool_w[l], pool_scale[l])
        ys_all = s5_mixer_pallas(jnp.concatenate([c_ssm, x_ssm], axis=1), n_ctx,
                                 ssm_a_re[l], ssm_a_im[l], ssm_log_dt[l],
                                 ssm_b_re[l], ssm_b_im[l], ssm_c_re[l], ssm_c_im[l],
                                 ssm_d[l], ssm_glu_w[l], ssm_glu_b[l]).astype(F32)
        ys_x, ys_c = ys_all[:, n_ctx:], ys_all[:, :n_ctx]
        qx = apply_axial_rope(mla_query(x_q, mla_q_a_g[l], mla_wq_b[l], mla_q_g[l]), rope)
        kx, vx = mla_key_value(x_kv, x_kpe, mla_kv_a_g[l], mla_wkv_b[l], mla_k_g[l])
        kx = apply_axial_rope(kx, rope)
        kc, vc = mla_key_value(c_kv, c_kpe, mla_kv_a_g[l], mla_wkv_b[l], mla_k_g[l])
        ym_x = blocked_attention(qx, jnp.concatenate([kx, kc], axis=1),
                                 jnp.concatenate([vx, vc], axis=1))
        x = x + g1 * (jnp.concatenate([ya_x, yp_x, ys_x, ym_x], axis=-1) @ w_mix_out[l])
        if not last:
            ya_c = conv_module(c_conv, *conv_p)
            yp_c = multiscale_pool(c_pool, pool_w[l], pool_scale[l])
            qc = mla_query(c_q, mla_q_a_g[l], mla_wq_b[l], mla_q_g[l])
            ym_c = attend(qc, kc, vc).reshape(bsz, n_ctx, W_MLA)
            ctx = ctx + g1c * (jnp.concatenate([ya_c, yp_c, ys_c, ym_c], axis=-1) @ w_mix_out[l])

        moe_p = (router_w[l], router_b[l], exp_w_in[l], exp_b_in[l], exp_w_out[l], exp_b_out[l])
        h2x = (rmsnorm(x, norm2_g[l]) * (1.0 + sc2) + sh2).reshape(bsz * n_lat, dim)
        if last:
            x = x + g2 * moe(h2x, *moe_p).reshape(bsz, n_lat, dim)
        else:
            h2c = (rmsnorm(ctx, norm2_g[l]) * (1.0 + sc2c) + sh2c).reshape(bsz * n_ctx, dim)
            m = moe(jnp.concatenate([h2x, h2c], axis=0), *moe_p)
            x = x + g2 * m[:bsz * n_lat].reshape(bsz, n_lat, dim)
            ctx = ctx + g2c * m[bsz * n_lat:].reshape(bsz, n_ctx, dim)
    return x
```

```python
import functools

import jax
import jax.numpy as jnp
from jax import lax
from jax.experimental import pallas as pl
from jax.experimental.pallas import tpu as pltpu

D_MODEL = 1024
DEPTH = 2
GRID_W = 64
EPS = 1e-6

D_MIX = D_MODEL
W_CONV = D_MIX // 4
W_POOL = D_MIX // 4
W_SSM = D_MIX // 4
W_MLA = D_MIX - W_CONV - W_POOL - W_SSM
CONV_K = 31
POOL_WINDOWS = (2, 4, 8, 16)
POOL_CH = W_POOL // len(POOL_WINDOWS)
SSM_CH = 16
SSM_GROUPS = W_SSM // SSM_CH
SSM_STATE = 64
MLA_V = 64
MLA_HEADS = W_MLA // MLA_V
MLA_NOPE = 64
MLA_ROPE = 32
MLA_QK = MLA_NOPE + MLA_ROPE
MLA_Q_RANK = 192
MLA_KV_RANK = 128
ROPE_AXIS = MLA_ROPE // 2
ROPE_BASE = 10000.0
N_EXPERTS = 32
TOP_K = 4
D_EXPERT = D_MODEL
SWIGLU_LIMIT = 7.0
SWIGLU_ALPHA = 1.702
MOE_BLOCK = 512

OFF_POOL = 2 * W_CONV
OFF_Q = OFF_POOL + W_POOL
OFF_SSM = OFF_Q + MLA_Q_RANK
OFF_KV = OFF_SSM + W_SSM
OFF_KPE = OFF_KV + MLA_KV_RANK
N_IN = OFF_KPE + MLA_ROPE

F32 = jnp.float32
BF16 = jnp.bfloat16
LANE = 128
QKP_W = 256


MOD_ROWS = 40
MOD_COLS = 1024


def _mod_kernel(c_ref, w_ref, b_ref, o_ref):
    c = c_ref[...]
    s = (c * jax.nn.sigmoid(c)).astype(BF16)
    o_ref[...] = jnp.dot(s, w_ref[...].astype(BF16), preferred_element_type=F32) + b_ref[...]


def modulation(c_all, ada_w, ada_b):
    n_out = ada_w.shape[1]
    return pl.pallas_call(
        _mod_kernel,
        out_shape=jax.ShapeDtypeStruct((MOD_ROWS, n_out), F32),
        grid=(n_out // MOD_COLS,),
        in_specs=[pl.BlockSpec((MOD_ROWS, D_MODEL), lambda j: (0, 0)),
                  pl.BlockSpec((D_MODEL, MOD_COLS), lambda j: (0, j)),
                  pl.BlockSpec((1, MOD_COLS), lambda j: (0, j))],
        out_specs=pl.BlockSpec((MOD_ROWS, MOD_COLS), lambda j: (0, j)),
        compiler_params=pltpu.CompilerParams(dimension_semantics=("arbitrary",)),
        name="modulation",
    )(c_all, ada_w, ada_b.reshape(1, n_out))


TOK_ROWS = 256
IN_SPLITS = (2 * W_CONV, W_POOL, W_SSM, MLA_KV_RANK, QKP_W)


def _mix_in_kernel(x_ref, sh_ref, sc_ref, g_ref, w_ref, *outs):
    x = x_ref[...]
    y = x * lax.rsqrt(jnp.mean(x * x, axis=-1, keepdims=True) + EPS) * g_ref[...]
    h = y * (1.0 + sc_ref[...]) + sh_ref[...]
    p = jnp.dot(h.astype(BF16), w_ref[...], preferred_element_type=F32)
    off = 0
    for o_ref, w in zip(outs, IN_SPLITS):
        o_ref[...] = p[:, off:off + w]
        off += w


def mix_in_weights(w_mix_in):
    cols = [w_mix_in[:, :OFF_POOL], w_mix_in[:, OFF_POOL:OFF_Q], w_mix_in[:, OFF_SSM:OFF_KV],
            w_mix_in[:, OFF_KV:OFF_KPE], w_mix_in[:, OFF_Q:OFF_SSM], w_mix_in[:, OFF_KPE:],
            jnp.zeros((D_MODEL, QKP_W - MLA_Q_RANK - MLA_ROPE), w_mix_in.dtype)]
    return jnp.concatenate(cols, axis=1).astype(BF16)


def mix_in(xc, mod3, norm_g, w_in_p, n_ctx):
    bsz, n_all, dim = xc.shape
    assert n_ctx % TOK_ROWS == 0 and n_all % TOK_ROWS == 0 and bsz < MOD_ROWS
    nct = n_ctx // TOK_ROWS
    tile = lambda w: pl.BlockSpec((None, TOK_ROWS, w), lambda b, i: (b, i, 0))

    def mod_spec(chunk):
        return pl.BlockSpec((None, 1, dim), lambda b, i: (jnp.where(i < nct, bsz, b), 0, chunk))
    return pl.pallas_call(
        _mix_in_kernel,
        out_shape=tuple(jax.ShapeDtypeStruct((bsz, n_all, w), F32) for w in IN_SPLITS),
        grid=(bsz, n_all // TOK_ROWS),
        in_specs=[tile(dim), mod_spec(0), mod_spec(1),
                  pl.BlockSpec((1, dim), lambda b, i: (0, 0)),
                  pl.BlockSpec(w_in_p.shape, lambda b, i: (0, 0))],
        out_specs=tuple(tile(w) for w in IN_SPLITS),
        compiler_params=pltpu.CompilerParams(dimension_semantics=("arbitrary", "arbitrary")),
        name="mix_in",
    )(xc, mod3, mod3, norm_g.reshape(1, dim), w_in_p)


LM_PAD = 16
LM_ROWS = 64
LM_DBL_ROWS = 32


def _local_mixers_kernel(cin_ref, pin_ref, dw_ref, dwb_ref, lng_ref, lnb_ref, pw_ref, pwb_ref, psc_ref,
                         ya_ref, yp_ref, upad, ppad, s2, s4, s8, *, n_ctx):
    n_all = cin_ref.shape[0]
    n_lat = n_all - n_ctx
    n_pad = upad.shape[0]
    lat0 = n_ctx + 2 * LM_PAD
    zeros = jnp.zeros((LM_PAD, W_CONV), F32)
    for buf in (upad, ppad):
        buf[0:LM_PAD, :] = zeros
        buf[LM_PAD + n_ctx:lat0, :] = zeros
        buf[lat0 + n_lat:n_pad, :] = zeros

    def pad_row(i):
        return pl.multiple_of(i * LM_ROWS + jnp.where(i < n_ctx // LM_ROWS, LM_PAD, 2 * LM_PAD), 8)

    def fill(i, _):
        r0 = pl.multiple_of(i * LM_ROWS, LM_ROWS)
        c = cin_ref[pl.ds(r0, LM_ROWS), :]
        dst = pl.ds(pad_row(i), LM_ROWS)
        upad[dst, :] = c[:, :W_CONV] * jax.nn.sigmoid(c[:, W_CONV:])
        ppad[dst, :] = pin_ref[pl.ds(r0, LM_ROWS), :]
        return 0
    lax.fori_loop(0, n_all // LM_ROWS, fill, 0)

    def doubling(src, dst, back, fwd):
        dst[0:8, :] = zeros[:8]
        dst[n_pad - 8:n_pad, :] = zeros[:8]

        def body(i, _):
            r0 = pl.multiple_of(8 + i * LM_DBL_ROWS, 8)
            h = src[pl.ds(r0 - 8, LM_DBL_ROWS + 16), :]
            dst[pl.ds(r0, LM_DBL_ROWS), :] = (h[8 - back:8 - back + LM_DBL_ROWS, :]
                                              + h[8 + fwd:8 + fwd + LM_DBL_ROWS, :])
            return 0
        lax.fori_loop(0, (n_pad - 16) // LM_DBL_ROWS, body, 0)
    doubling(ppad, s2, 1, 0)
    doubling(s2, s4, 1, 1)
    doubling(s4, s8, 2, 2)

    lane = lax.broadcasted_iota(jnp.int32, (LM_ROWS, W_POOL), 1)
    grp = lane // POOL_CH
    back = jnp.where(grp == 0, 1, jnp.where(grp == 1, 2, jnp.where(grp == 2, 4, 8)))
    fwd = back - 1

    def tile(i, _):
        r0 = pl.multiple_of(i * LM_ROWS, LM_ROWS)
        p0 = pad_row(i)
        halo = upad[pl.ds(p0 - LM_PAD, LM_ROWS + 2 * LM_PAD), :]
        acc = jnp.zeros((LM_ROWS, W_CONV), F32)
        for k in range(CONV_K):
            o = LM_PAD + k - CONV_K // 2
            acc = acc + halo[o:o + LM_ROWS, :] * dw_ref[k:k + 1, :]
        acc = acc + dwb_ref[...]
        mu = jnp.mean(acc, axis=-1, keepdims=True)
        var = jnp.mean(jnp.square(acc - mu), axis=-1, keepdims=True)
        v = (acc - mu) * lax.rsqrt(var + EPS) * lng_ref[...] + lnb_ref[...]
        v = v * jax.nn.sigmoid(v)
        ya_ref[pl.ds(r0, LM_ROWS), :] = jnp.dot(v.astype(BF16), pw_ref[...],
                                                preferred_element_type=F32).astype(ya_ref.dtype)
        rows = pl.ds(p0, LM_ROWS)
        h8 = s8[pl.ds(p0 - 8, LM_ROWS + 16), :]
        s16 = h8[4:4 + LM_ROWS, :] + h8[12:12 + LM_ROWS, :]
        wsum = jnp.where(grp == 0, s2[rows, :], jnp.where(grp == 1, s4[rows, :],
                                                          jnp.where(grp == 2, s8[rows, :], s16)))
        seg_len = jnp.where(i < n_ctx // LM_ROWS, n_ctx, n_lat)
        t = (r0 - jnp.where(i < n_ctx // LM_ROWS, 0, n_ctx)) + lax.broadcasted_iota(jnp.int32, (LM_ROWS, W_POOL), 0)
        cnt = jnp.minimum(t + fwd, seg_len - 1) - jnp.maximum(t - back, 0) + 1
        pooled = wsum / cnt.astype(F32) - ppad[rows, :]
        y = jnp.dot(pooled.astype(BF16), pwb_ref[...], preferred_element_type=F32) * psc_ref[...]
        yp_ref[pl.ds(r0, LM_ROWS), :] = y.astype(yp_ref.dtype)
        return 0
    lax.fori_loop(0, n_all // LM_ROWS, tile, 0)


def local_mixers(conv_in, pool_in, n_ctx, conv_dw, conv_dw_b, ln_g, ln_b, conv_pw, pool_w, pool_scale):
    bsz, n_all, _ = conv_in.shape
    assert n_ctx % LM_ROWS == 0 and n_all % LM_ROWS == 0
    n_pad = n_all + 3 * LM_PAD
    assert (n_pad - 16) % LM_DBL_ROWS == 0
    dw = jnp.zeros((CONV_K + 1, W_CONV), F32).at[:CONV_K].set(conv_dw)
    eye = jnp.eye(len(POOL_WINDOWS), dtype=F32)
    pwb = jnp.einsum('gcd,gh->gchd', pool_w, eye).reshape(W_POOL, W_POOL).astype(BF16)
    row = lambda v: v.reshape(1, -1).astype(F32)
    params = [dw, row(conv_dw_b), row(ln_g), row(ln_b), conv_pw.astype(BF16), pwb, row(pool_scale)]

    def full(arr):
        return pl.BlockSpec(arr.shape, lambda b: (0,) * arr.ndim)
    seq = lambda w: pl.BlockSpec((None, n_all, w), lambda b: (b, 0, 0))
    out = jax.ShapeDtypeStruct((bsz, n_all, W_CONV), BF16)
    return pl.pallas_call(
        functools.partial(_local_mixers_kernel, n_ctx=n_ctx),
        out_shape=(out, out),
        grid=(bsz,),
        in_specs=[seq(2 * W_CONV), seq(W_POOL)] + [full(p) for p in params],
        out_specs=(seq(W_CONV), seq(W_POOL)),
        scratch_shapes=[pltpu.VMEM((n_pad, W_CONV), F32) for _ in range(5)],
        compiler_params=pltpu.CompilerParams(dimension_semantics=("arbitrary",)),
        name="local_mixers",
    )(conv_in, pool_in, *params)


S5_NS = SSM_GROUPS * SSM_STATE
S5_TC = 128
S5_BG = 8
S5_LANES = 512


def _s5_scan_kernel(u_ref, perm_ref, a_ref, bd_ref, cd_ref, *rest, reverse, final):
    if final:
        yp_ref, permt_ref, d_ref, gw_ref, gb_ref, y_ref, bu_s, hh_s, h_s = rest
    else:
        y_ref, bu_s, hh_s, h_s = rest
    tc = u_ref.shape[1]
    rows = S5_BG * tc
    j = pl.program_id(1)

    @pl.when(j == 0)
    def _():
        h_s[...] = jnp.zeros_like(h_s)

    u = u_ref[...].reshape(rows, W_SSM)
    u_hi = u.astype(BF16)
    if final:
        u_lo = (u - u_hi.astype(F32)).astype(BF16)
        u_tm = jnp.dot(perm_ref[...], jnp.concatenate([u_hi, u_lo], axis=1), preferred_element_type=F32)
        u_hi_tm = u_tm[:, :W_SSM].astype(BF16)
        u_f32_tm = u_tm[:, :W_SSM] + u_tm[:, W_SSM:]
    else:
        u_hi_tm = jnp.dot(perm_ref[...], u_hi, preferred_element_type=F32).astype(BF16)
    bu_s[...] = jnp.dot(u_hi_tm, bd_ref[...], preferred_element_type=F32)

    for hf in range(S5_NS // S5_LANES):
        re_cols = pl.ds(hf * S5_LANES, S5_LANES)
        im_cols = pl.ds(S5_NS + hf * S5_LANES, S5_LANES)
        a_re = jnp.broadcast_to(a_ref[0:1, hf * S5_LANES:(hf + 1) * S5_LANES], (S5_BG, S5_LANES))
        a_im = jnp.broadcast_to(a_ref[1:2, hf * S5_LANES:(hf + 1) * S5_LANES], (S5_BG, S5_LANES))

        def step(i, carry):
            h_re, h_im = carry
            t = (tc - 1 - i) if reverse else i
            r8 = pl.ds(pl.multiple_of(t * S5_BG, S5_BG), S5_BG)
            n_re = a_re * h_re - a_im * h_im + bu_s[r8, re_cols]
            n_im = a_re * h_im + a_im * h_re + bu_s[r8, im_cols]
            hh_s[r8, re_cols] = n_re
            hh_s[r8, im_cols] = n_im
            return n_re, n_im

        h_re, h_im = lax.fori_loop(0, tc, step, (h_s[:, re_cols], h_s[:, im_cols]), unroll=8)
        h_s[:, re_cols] = h_re
        h_s[:, im_cols] = h_im

    y = jnp.dot(hh_s[...].astype(BF16), cd_ref[...], preferred_element_type=F32)
    if final:
        y = y + yp_ref[...] + d_ref[...] * u_f32_tm
        z = jax.nn.gelu(y)
        y = z * jax.nn.sigmoid(jnp.dot(z.astype(BF16), gw_ref[...], preferred_element_type=F32) + gb_ref[...])
        y = jnp.dot(permt_ref[...], y.astype(BF16), preferred_element_type=F32)
        y_ref[...] = y.astype(BF16).reshape(S5_BG, tc, W_SSM)
    else:
        y_ref[...] = y


def s5_direction(u_all, perm, a, bd, cd, n_ctx, reverse, final_args=None):
    bsz, n_all, _ = u_all.shape
    tc = S5_TC
    n_chunks = n_all // tc
    ctx_chunks = n_ctx // tc
    rows = S5_BG * tc
    assert n_all % tc == 0 and n_ctx % tc == 0 and bsz % S5_BG == 0

    if reverse:
        def chunk(j):
            return jnp.where(j < ctx_chunks, ctx_chunks - 1 - j, n_chunks - 1 + ctx_chunks - j)
    else:
        def chunk(j):
            return j
    seq_spec = pl.BlockSpec((S5_BG, tc, W_SSM), lambda g, j: (g, chunk(j), 0))
    part_spec = pl.BlockSpec((None, None, rows, W_SSM), lambda g, j: (g, chunk(j), 0, 0))

    def full(arr):
        return pl.BlockSpec(arr.shape, lambda g, j: (0,) * arr.ndim)
    in_specs = [seq_spec, full(perm), full(a), full(bd), full(cd)]
    args = [u_all, perm, a, bd, cd]
    if final_args is not None:
        y_prev, d, glu_w, glu_b = final_args
        perm_t = perm.T
        in_specs += [part_spec, full(perm_t), full(d), full(glu_w), full(glu_b)]
        args += [y_prev, perm_t, d, glu_w, glu_b]
        out_shape = jax.ShapeDtypeStruct(u_all.shape, BF16)
        out_spec = seq_spec
    else:
        out_shape = jax.ShapeDtypeStruct((bsz // S5_BG, n_chunks, rows, W_SSM), F32)
        out_spec = part_spec
    return pl.pallas_call(
        functools.partial(_s5_scan_kernel, reverse=reverse, final=final_args is not None),
        out_shape=out_shape,
        grid=(bsz // S5_BG, n_chunks),
        in_specs=in_specs,
        out_specs=out_spec,
        scratch_shapes=[pltpu.VMEM((rows, 2 * S5_NS), F32), pltpu.VMEM((rows, 2 * S5_NS), F32),
                        pltpu.VMEM((S5_BG, 2 * S5_NS), F32)],
        compiler_params=pltpu.CompilerParams(dimension_semantics=("arbitrary", "arbitrary")),
        name="s5_scan_bwd" if reverse else "s5_scan_fwd",
    )(*args)


def s5_time_major_perm(tc):
    r = jnp.arange(S5_BG * tc)
    src = (r % S5_BG) * tc + r // S5_BG
    return (src[:, None] == r[None, :]).astype(BF16)


def s5_params(a_re, a_im, log_dt, b_re, b_im, c_re, c_im):
    a_re = jnp.minimum(a_re.astype(F32), -1e-4)
    a_im = a_im.astype(F32)
    dt = jnp.exp(log_dt.astype(F32))[:, None]
    mag = jnp.exp(a_re * dt)
    ab_re = mag * jnp.cos(a_im * dt)
    ab_im = mag * jnp.sin(a_im * dt)
    den = a_re * a_re + a_im * a_im
    f_re = ((ab_re - 1.0) * a_re + ab_im * a_im) / den
    f_im = (ab_im * a_re - (ab_re - 1.0) * a_im) / den
    bb_re = f_re[..., None] * b_re - f_im[..., None] * b_im
    bb_im = f_re[..., None] * b_im + f_im[..., None] * b_re
    eye = jnp.eye(SSM_GROUPS, dtype=F32)

    def in_map(bb):
        return jnp.einsum('gpc,gh->gchp', bb, eye).reshape(W_SSM, S5_NS)

    def out_map(cc):
        return jnp.einsum('gcp,gh->gphc', cc, eye).reshape(S5_NS, W_SSM)
    a = jnp.stack([ab_re.reshape(-1), ab_im.reshape(-1)])
    bd = jnp.concatenate([in_map(bb_re), in_map(bb_im)], axis=1).astype(BF16)
    cd = jnp.concatenate([out_map(c_re.astype(F32)), -out_map(c_im.astype(F32))], axis=0).astype(BF16)
    return a, bd, cd


def s5_mixer_pallas(u_all, n_ctx, a_re, a_im, log_dt, b_re, b_im, c_re, c_im, d, glu_w, glu_b):
    pf = s5_params(a_re[0], a_im[0], log_dt[0], b_re[0], b_im[0], c_re[0], c_im[0])
    pb = s5_params(a_re[1], a_im[1], log_dt[1], b_re[1], b_im[1], c_re[1], c_im[1])
    perm = s5_time_major_perm(S5_TC)
    y_f = s5_direction(u_all, perm, *pf, n_ctx, False)
    return s5_direction(u_all, perm, *pb, n_ctx, True,
                        (y_f, d.reshape(1, W_SSM).astype(F32), glu_w.astype(BF16), glu_b.reshape(1, W_SSM)))


HEAD_W = LANE
ROT_A = MLA_NOPE
ROT_B = MLA_NOPE + MLA_ROPE // 2
ATT_PREP_ROWS = 256
ATT_Q_ROWS = 256


def _head_lane_of_dim():
    half = ROPE_AXIS // 2
    lanes = list(range(MLA_NOPE))
    lanes += [ROT_A + i for i in range(half)] + [ROT_B + i for i in range(half)]
    lanes += [ROT_A + half + i for i in range(half)] + [ROT_B + half + i for i in range(half)]
    return jnp.array(lanes, jnp.int32)


def mla_layout_params(q_a_g, wq_b, kv_a_g, wkv_b, q_g, k_g):
    lane_of = _head_lane_of_dim()
    wq = wq_b.reshape(MLA_Q_RANK, MLA_HEADS, MLA_QK)
    wq_p = jnp.zeros((QKP_W, MLA_HEADS, HEAD_W), F32).at[:MLA_Q_RANK, :, lane_of].set(wq)
    wq_p = wq_p.reshape(QKP_W, MLA_HEADS * HEAD_W).astype(BF16)
    qa_g = jnp.zeros((1, QKP_W), F32).at[0, :MLA_Q_RANK].set(q_a_g)
    wkv = wkv_b.reshape(MLA_KV_RANK, MLA_HEADS, MLA_NOPE + MLA_V)
    wk_p = jnp.zeros((MLA_KV_RANK, MLA_HEADS, HEAD_W), F32).at[:, :, :MLA_NOPE].set(wkv[:, :, :MLA_NOPE])
    wkv_p = jnp.concatenate([wk_p.reshape(MLA_KV_RANK, -1), wkv[:, :, MLA_NOPE:].reshape(MLA_KV_RANK, -1)],
                            axis=1).astype(BF16)
    src = MLA_Q_RANK + jnp.arange(MLA_ROPE)
    place = jnp.zeros((QKP_W, MLA_HEADS, HEAD_W), F32).at[src, :, lane_of[MLA_NOPE:]].set(1.0)
    place = place.reshape(QKP_W, MLA_HEADS * HEAD_W).astype(BF16)

    def head_gain(g):
        return jnp.zeros((1, HEAD_W), F32).at[0, lane_of].set(g)
    return wq_p, qa_g, wkv_p, kv_a_g.reshape(1, MLA_KV_RANK), place, head_gain(q_g), head_gain(k_g)


def rope_lane_tables(n_ctx, n_lat):
    t = jnp.arange(n_lat)
    row = (t // GRID_W).astype(F32)
    col = (t % GRID_W).astype(F32)
    inv = ROPE_BASE ** (-jnp.arange(0, ROPE_AXIS, 2, dtype=F32) / ROPE_AXIS)
    ang = jnp.concatenate([row[:, None] * inv, col[:, None] * inv], axis=1)
    cos, sin = jnp.cos(ang), jnp.sin(ang)
    half = MLA_ROPE // 2
    cmul = jnp.ones((n_lat, HEAD_W), F32).at[:, ROT_A:ROT_A + half].set(cos).at[:, ROT_B:ROT_B + half].set(cos)
    s_up = jnp.zeros((n_lat, HEAD_W), F32).at[:, ROT_A:ROT_A + half].set(-sin)
    s_dn = jnp.zeros((n_lat, HEAD_W), F32).at[:, ROT_B:ROT_B + half].set(sin)
    ctx = [jnp.ones((n_ctx, HEAD_W), F32), jnp.zeros((n_ctx, HEAD_W), F32), jnp.zeros((n_ctx, HEAD_W), F32)]
    return jnp.stack([jnp.concatenate([c, x], axis=0) for c, x in zip(ctx, (cmul, s_up, s_dn))])


def _mla_kernel(ckv_ref, qkp_ref, rope_ref, wq_ref, qag_ref, wkv_ref, kvag_ref, place_ref, qg_ref, kg_ref,
                o_ref, q_s, k_s, v_s, *, n_ctx, ctx_queries):
    n_all = ckv_ref.shape[0]
    half = MLA_ROPE // 2

    def rope(x, r0):
        rows = pl.ds(r0, ATT_PREP_ROWS)
        up = pltpu.roll(x, HEAD_W - half, axis=1)
        dn = pltpu.roll(x, half, axis=1)
        return x * rope_ref[0, rows, :] + up * rope_ref[1, rows, :] + dn * rope_ref[2, rows, :]

    def head_norm(x, g):
        ss = jnp.sum(x * x, axis=-1, keepdims=True) * (1.0 / MLA_QK)
        return x * lax.rsqrt(ss + EPS) * g

    def prep(i, _):
        r0 = pl.multiple_of(i * ATT_PREP_ROWS, ATT_PREP_ROWS)
        rows = pl.ds(r0, ATT_PREP_ROWS)
        ckv = ckv_ref[rows, :]
        kvn = ckv * lax.rsqrt(jnp.mean(ckv * ckv, axis=-1, keepdims=True) + EPS) * kvag_ref[...]
        kv = jnp.dot(kvn.astype(BF16), wkv_ref[...], preferred_element_type=F32)
        qkp = qkp_ref[rows, :]
        p_hi = qkp.astype(BF16)
        p_lo = (qkp - p_hi.astype(F32)).astype(BF16)
        kpe = (jnp.dot(p_hi, place_ref[...], preferred_element_type=F32)
               + jnp.dot(p_lo, place_ref[...], preferred_element_type=F32))
        lane = lax.broadcasted_iota(jnp.int32, qkp.shape, 1)
        qsq = jnp.where(lane < MLA_Q_RANK, qkp * qkp, 0.0)
        qn = qkp * lax.rsqrt(jnp.sum(qsq, axis=-1, keepdims=True) * (1.0 / MLA_Q_RANK) + EPS) * qag_ref[...]
        q = jnp.dot(qn.astype(BF16), wq_ref[...], preferred_element_type=F32)
        for h in range(MLA_HEADS):
            blk = slice(h * HEAD_W, (h + 1) * HEAD_W)
            k_h = rope(head_norm(kv[:, blk] + kpe[:, blk], kg_ref[...]), r0)
            q_h = rope(head_norm(q[:, blk], qg_ref[...]), r0) * (MLA_QK ** -0.5)
            k_s[h, rows, :] = k_h.astype(BF16)
            q_s[h, rows, :] = q_h.astype(BF16)
        v_s[rows, :] = kv[:, MLA_HEADS * HEAD_W:].astype(BF16)
        return 0

    lax.fori_loop(0, n_all // ATT_PREP_ROWS, prep, 0)

    lane = lax.broadcasted_iota(jnp.int32, (ATT_Q_ROWS, LANE), 1)

    def attend_rows(r0, n_keys):
        outs = []
        for pair in range(MLA_HEADS // 2):
            v2 = v_s[pl.ds(0, n_keys), pair * LANE:(pair + 1) * LANE]
            o2 = []
            for h in (2 * pair, 2 * pair + 1):
                q = q_s[h, pl.ds(r0, ATT_Q_ROWS), :]
                k = k_s[h, pl.ds(0, n_keys), :]
                s = lax.dot_general(q, k, (((1,), (1,)), ((), ())), preferred_element_type=F32)
                p = jnp.exp(s - jnp.max(s, axis=-1, keepdims=True))
                den = jnp.sum(p, axis=-1, keepdims=True)
                o2.append(jnp.dot(p.astype(BF16), v2, preferred_element_type=F32) / den)
            outs.append(jnp.where(lane < MLA_V, o2[0], o2[1]))
        return jnp.concatenate(outs, axis=1)

    for i in range(n_ctx // ATT_Q_ROWS):
        r0 = i * ATT_Q_ROWS
        if ctx_queries:
            o_ref[pl.ds(r0, ATT_Q_ROWS), :] = attend_rows(r0, n_ctx).astype(o_ref.dtype)
        else:
            o_ref[pl.ds(r0, ATT_Q_ROWS), :] = jnp.zeros((ATT_Q_ROWS, W_MLA), o_ref.dtype)

    def q_tile(i, _):
        r0 = pl.multiple_of(n_ctx + i * ATT_Q_ROWS, ATT_Q_ROWS)
        o_ref[pl.ds(r0, ATT_Q_ROWS), :] = attend_rows(r0, n_all).astype(o_ref.dtype)
        return 0

    lax.fori_loop(0, (n_all - n_ctx) // ATT_Q_ROWS, q_tile, 0)


def mla_attention(ckv, qkp, rope_tab, params, n_ctx, ctx_queries):
    bsz, n_all, _ = ckv.shape
    assert n_all % ATT_PREP_ROWS == 0 and n_ctx % ATT_Q_ROWS == 0 and (n_all - n_ctx) % ATT_Q_ROWS == 0

    def full(arr):
        return pl.BlockSpec(arr.shape, lambda b: (0,) * arr.ndim)
    return pl.pallas_call(
        functools.partial(_mla_kernel, n_ctx=n_ctx, ctx_queries=ctx_queries),
        out_shape=jax.ShapeDtypeStruct((bsz, n_all, W_MLA), BF16),
        grid=(bsz,),
        in_specs=[pl.BlockSpec((None, n_all, MLA_KV_RANK), lambda b: (b, 0, 0)),
                  pl.BlockSpec((None, n_all, QKP_W), lambda b: (b, 0, 0)),
                  full(rope_tab)] + [full(p) for p in params],
        out_specs=pl.BlockSpec((None, n_all, W_MLA), lambda b: (b, 0, 0)),
        scratch_shapes=[pltpu.VMEM((MLA_HEADS, n_all, HEAD_W), BF16), pltpu.VMEM((MLA_HEADS, n_all, HEAD_W), BF16),
                        pltpu.VMEM((n_all, MLA_HEADS * MLA_V), BF16)],
        compiler_params=pltpu.CompilerParams(dimension_semantics=("arbitrary",)),
        name="mla_attention",
    )(ckv, qkp, rope_tab, *params)


def _mix_out_kernel(ya_ref, yp_ref, ys_ref, ym_ref, x_ref, g1_ref, sh_ref, sc_ref, ng_ref, wo_ref, rw_ref, rb_ref,
                    xo_ref, h_ref, rt_ref):
    acc = jnp.dot(ya_ref[...], wo_ref[0:W_CONV, :], preferred_element_type=F32)
    acc += jnp.dot(yp_ref[...], wo_ref[W_CONV:W_CONV + W_POOL, :], preferred_element_type=F32)
    acc += jnp.dot(ys_ref[...], wo_ref[W_CONV + W_POOL:D_MIX - W_MLA, :], preferred_element_type=F32)
    acc += jnp.dot(ym_ref[...], wo_ref[D_MIX - W_MLA:, :], preferred_element_type=F32)
    x = x_ref[...] + g1_ref[...] * acc
    xo_ref[...] = x
    y = x * lax.rsqrt(jnp.mean(x * x, axis=-1, keepdims=True) + EPS) * ng_ref[...]
    h = (y * (1.0 + sc_ref[...]) + sh_ref[...]).astype(BF16)
    h_ref[...] = h
    logits = jnp.dot(h, rw_ref[...], preferred_element_type=F32) + rb_ref[...]
    lane = lax.broadcasted_iota(jnp.int32, logits.shape, 1)
    route = jnp.zeros(logits.shape, F32)
    vals = []
    for k in range(TOP_K):
        m = jnp.max(logits, axis=-1, keepdims=True)
        idx = jnp.min(jnp.where(logits == m, lane, LANE), axis=-1, keepdims=True)
        logits = jnp.where(lane == idx, -jnp.inf, logits)
        route = jnp.where(lane == k, idx.astype(F32), route)
        vals.append(m)
    ex = [jnp.exp(v - vals[0]) for v in vals]
    den = ex[0] + ex[1] + ex[2] + ex[3]
    for k in range(TOP_K):
        route = jnp.where(lane == TOP_K + k, ex[k] / den, route)
    rt_ref[...] = route


def mix_out(ya, yp, ys, ym, xc, mod3, norm_g, w_out, router_w, router_b, n_ctx, skip_ctx):
    bsz, n_all, dim = xc.shape
    nct = n_ctx // TOK_ROWS
    off = nct if skip_ctx else 0
    n_out = n_all - off * TOK_ROWS
    tile_in = lambda w: pl.BlockSpec((None, TOK_ROWS, w), lambda b, i: (b, i + off, 0))
    tile_out = lambda w: pl.BlockSpec((None, TOK_ROWS, w), lambda b, i: (b, i, 0))

    def mod_spec(chunk):
        return pl.BlockSpec((None, 1, dim), lambda b, i: (jnp.where(i + off < nct, bsz, b), 0, chunk))
    rw = jnp.zeros((dim, LANE), F32).at[:, :N_EXPERTS].set(router_w).astype(BF16)
    rb = jnp.full((1, LANE), -1e30, F32).at[0, :N_EXPERTS].set(router_b)
    full = lambda a: pl.BlockSpec(a.shape, lambda b, i: (0,) * a.ndim)
    wo = w_out.astype(BF16)
    ng = norm_g.reshape(1, dim)
    return pl.pallas_call(
        _mix_out_kernel,
        out_shape=(jax.ShapeDtypeStruct((bsz, n_out, dim), F32), jax.ShapeDtypeStruct((bsz, n_out, dim), BF16),
                   jax.ShapeDtypeStruct((bsz, n_out, LANE), F32)),
        grid=(bsz, n_out // TOK_ROWS),
        in_specs=[tile_in(W_CONV), tile_in(W_POOL), tile_in(W_SSM), tile_in(W_MLA), tile_in(dim),
                  mod_spec(2), mod_spec(3), mod_spec(4), full(ng), full(wo), full(rw), full(rb)],
        out_specs=(tile_out(dim), tile_out(dim), tile_out(LANE)),
        compiler_params=pltpu.CompilerParams(dimension_semantics=("arbitrary", "arbitrary")),
        name="mix_out",
    )(ya, yp, ys, ym, xc, mod3, mod3, mod3, ng, wo, rw, rb)


def _expert_block_kernel(be_ref, nu_ref, x_ref, win_ref, bin_ref, wout_ref, bout_ref, o_ref):
    i = pl.program_id(0)

    @pl.when(i < nu_ref[0])
    def _():
        gu = jnp.dot(x_ref[...], win_ref[0], preferred_element_type=F32) + bin_ref[0]
        gate = jnp.minimum(gu[:, :D_EXPERT], SWIGLU_LIMIT)
        up = jnp.clip(gu[:, D_EXPERT:], -SWIGLU_LIMIT, SWIGLU_LIMIT)
        act = (up + 1.0) * gate * jax.nn.sigmoid(SWIGLU_ALPHA * gate)
        o_ref[...] = jnp.dot(act.astype(BF16), wout_ref[0], preferred_element_type=F32) + bout_ref[0]

    @pl.when(i >= nu_ref[0])
    def _():
        o_ref[...] = jnp.zeros_like(o_ref)


def expert_blocks(xs, block_e, n_used, w_in, b_in, w_out, b_out):
    n_pad, dim = xs.shape
    n_blocks = n_pad // MOE_BLOCK

    def row_map(i, be, nu):
        return (jnp.minimum(i, nu[0] - 1), 0)

    def exp_map3(i, be, nu):
        return (be[i], 0, 0)

    return pl.pallas_call(
        _expert_block_kernel,
        out_shape=jax.ShapeDtypeStruct((n_pad, dim), F32),
        grid_spec=pltpu.PrefetchScalarGridSpec(
            num_scalar_prefetch=2,
            grid=(n_blocks,),
            in_specs=[
                pl.BlockSpec((MOE_BLOCK, dim), row_map),
                pl.BlockSpec((1, dim, 2 * D_EXPERT), exp_map3),
                pl.BlockSpec((1, 1, 2 * D_EXPERT), exp_map3),
                pl.BlockSpec((1, D_EXPERT, dim), exp_map3),
                pl.BlockSpec((1, 1, dim), exp_map3),
            ],
            out_specs=pl.BlockSpec((MOE_BLOCK, dim), lambda i, be, nu: (i, 0)),
        ),
        compiler_params=pltpu.CompilerParams(dimension_semantics=("arbitrary",)),
        name="expert_blocks",
    )(block_e, n_used, xs, w_in, b_in[:, None, :], w_out, b_out[:, None, :])


def moe(h, top_i, top_w, w_in, b_in, w_out, b_out):
    n_tok, dim = h.shape
    flat_e = top_i.reshape(-1)
    n_asg = n_tok * TOP_K
    flat_t = jnp.arange(n_asg, dtype=jnp.int32) // TOP_K
    order = jnp.argsort(flat_e)
    sorted_e = flat_e[order]
    counts = jnp.bincount(flat_e, length=N_EXPERTS)
    padded = (counts + MOE_BLOCK - 1) // MOE_BLOCK * MOE_BLOCK
    pad_end = jnp.cumsum(padded)
    pad_start = pad_end - padded
    start = jnp.cumsum(counts) - counts
    dest = (pad_start[sorted_e] + (jnp.arange(n_asg) - start[sorted_e])).astype(jnp.int32)
    n_blocks = -(-n_asg // MOE_BLOCK) + N_EXPERTS
    n_pad = n_blocks * MOE_BLOCK
    tok_pad = jnp.zeros((n_pad,), jnp.int32).at[dest].set(flat_t[order])
    pos = jnp.zeros((n_asg,), jnp.int32).at[order].set(dest)
    block_e = jnp.minimum(jnp.searchsorted(pad_end, jnp.arange(n_blocks) * MOE_BLOCK, side='right'),
                          N_EXPERTS - 1).astype(jnp.int32)
    n_used = (pad_end[-1] // MOE_BLOCK).astype(jnp.int32).reshape(1)
    xs = h[tok_pad]
    ys = expert_blocks(xs, block_e, n_used, w_in.astype(BF16), b_in, w_out.astype(BF16), b_out)
    yk = ys[pos].reshape(n_tok, TOP_K, dim)
    return jnp.sum(yk * top_w[:, :, None], axis=1)


def kernel(x, c, ctx, c_ctx, ada_w, ada_b, norm1_g, norm2_g, w_mix_in, w_mix_out,
           conv_dw, conv_dw_b, conv_ln_g, conv_ln_b, conv_pw, pool_w, pool_scale,
           ssm_a_re, ssm_a_im, ssm_log_dt, ssm_b_re, ssm_b_im, ssm_c_re, ssm_c_im,
           ssm_d, ssm_glu_w, ssm_glu_b,
           mla_q_a_g, mla_wq_b, mla_kv_a_g, mla_wkv_b, mla_q_g, mla_k_g,
           router_w, router_b, exp_w_in, exp_b_in, exp_w_out, exp_b_out):
    bsz, n_lat, dim = x.shape
    n_ctx = ctx.shape[1]
    xc = jnp.concatenate([ctx, x], axis=1)
    c_all = jnp.zeros((MOD_ROWS, dim), F32).at[:bsz].set(c).at[bsz].set(c_ctx)
    rope_tab = rope_lane_tables(n_ctx, n_lat)
    for l in range(DEPTH):
        last = l == DEPTH - 1
        mod = modulation(c_all, ada_w[l], ada_b[l])
        mod3 = mod.reshape(MOD_ROWS, 1, 6 * dim)

        conv_in, pool_in, ssm_in, ckv, qkp = mix_in(xc, mod3, norm1_g[l], mix_in_weights(w_mix_in[l]), n_ctx)
        ya, yp = local_mixers(conv_in, pool_in, n_ctx, conv_dw[l], conv_dw_b[l], conv_ln_g[l], conv_ln_b[l],
                              conv_pw[l], pool_w[l], pool_scale[l])
        ys = s5_mixer_pallas(ssm_in, n_ctx, ssm_a_re[l], ssm_a_im[l], ssm_log_dt[l],
                             ssm_b_re[l], ssm_b_im[l], ssm_c_re[l], ssm_c_im[l],
                             ssm_d[l], ssm_glu_w[l], ssm_glu_b[l])
        ym = mla_attention(ckv, qkp, rope_tab,
                           mla_layout_params(mla_q_a_g[l], mla_wq_b[l], mla_kv_a_g[l], mla_wkv_b[l],
                                             mla_q_g[l], mla_k_g[l]), n_ctx, not last)
        xn, h2, route = mix_out(ya, yp, ys, ym, xc, mod3, norm2_g[l], w_mix_out[l], router_w[l], router_b[l],
                                n_ctx, last)
        n_rows = xn.shape[1]
        route = route.reshape(bsz * n_rows, LANE)
        m = moe(h2.reshape(bsz * n_rows, dim), route[:, :TOP_K].astype(jnp.int32), route[:, TOP_K:2 * TOP_K],
                exp_w_in[l], exp_b_in[l], exp_w_out[l], exp_b_out[l]).reshape(bsz, n_rows, dim)
        g2 = mod[:bsz, None, 5 * dim:]
        if last:
            return xn + g2 * m
        g2c = jnp.broadcast_to(mod[bsz, 5 * dim:], (bsz, n_ctx, dim))
        xc = xn + jnp.concatenate([g2c, jnp.broadcast_to(g2, (bsz, n_lat, dim))], axis=1) * m
```

```python
import functools

import jax
import jax.numpy as jnp
from jax import lax
from jax.experimental import pallas as pl
from jax.experimental.pallas import tpu as pltpu

D_MODEL = 1024
DEPTH = 2
GRID_W = 64
EPS = 1e-6

D_MIX = D_MODEL
W_CONV = D_MIX // 4
W_POOL = D_MIX // 4
W_SSM = D_MIX // 4
W_MLA = D_MIX - W_CONV - W_POOL - W_SSM
CONV_K = 31
POOL_WINDOWS = (2, 4, 8, 16)
POOL_CH = W_POOL // len(POOL_WINDOWS)
SSM_CH = 16
SSM_GROUPS = W_SSM // SSM_CH
SSM_STATE = 64
MLA_V = 64
MLA_HEADS = W_MLA // MLA_V
MLA_NOPE = 64
MLA_ROPE = 32
MLA_QK = MLA_NOPE + MLA_ROPE
MLA_Q_RANK = 192
MLA_KV_RANK = 128
ROPE_AXIS = MLA_ROPE // 2
ROPE_BASE = 10000.0
N_EXPERTS = 32
TOP_K = 4
D_EXPERT = D_MODEL
SWIGLU_LIMIT = 7.0
SWIGLU_ALPHA = 1.702
MOE_BLOCK = 512

OFF_POOL = 2 * W_CONV
OFF_Q = OFF_POOL + W_POOL
OFF_SSM = OFF_Q + MLA_Q_RANK
OFF_KV = OFF_SSM + W_SSM
OFF_KPE = OFF_KV + MLA_KV_RANK
N_IN = OFF_KPE + MLA_ROPE

F32 = jnp.float32
BF16 = jnp.bfloat16
LANE = 128
QKP_W = 256


MOD_ROWS = 40
MOD_COLS = 1024


def _mod_kernel(c_ref, w_ref, b_ref, o_ref):
    c = c_ref[...]
    s = (c * jax.nn.sigmoid(c)).astype(BF16)
    o_ref[...] = jnp.dot(s, w_ref[...].astype(BF16), preferred_element_type=F32) + b_ref[...]


def modulation(c_all, ada_w, ada_b):
    n_out = ada_w.shape[1]
    return pl.pallas_call(
        _mod_kernel,
        out_shape=jax.ShapeDtypeStruct((MOD_ROWS, n_out), F32),
        grid=(n_out // MOD_COLS,),
        in_specs=[pl.BlockSpec((MOD_ROWS, D_MODEL), lambda j: (0, 0)),
                  pl.BlockSpec((D_MODEL, MOD_COLS), lambda j: (0, j)),
                  pl.BlockSpec((1, MOD_COLS), lambda j: (0, j))],
        out_specs=pl.BlockSpec((MOD_ROWS, MOD_COLS), lambda j: (0, j)),
        compiler_params=pltpu.CompilerParams(dimension_semantics=("arbitrary",)),
        name="modulation",
    )(c_all, ada_w, ada_b.reshape(1, n_out))


TOK_ROWS = 256
IN_SPLITS = (2 * W_CONV, W_POOL, W_SSM, MLA_KV_RANK, QKP_W)


def _mix_in_kernel(x_ref, sh_ref, sc_ref, g_ref, w_ref, *outs):
    x = x_ref[...]
    y = x * lax.rsqrt(jnp.mean(x * x, axis=-1, keepdims=True) + EPS) * g_ref[...]
    h = y * (1.0 + sc_ref[...]) + sh_ref[...]
    p = jnp.dot(h.astype(BF16), w_ref[...], preferred_element_type=F32)
    off = 0
    for o_ref, w in zip(outs, IN_SPLITS):
        o_ref[...] = p[:, off:off + w]
        off += w


def mix_in_weights(w_mix_in):
    cols = [w_mix_in[:, :OFF_POOL], w_mix_in[:, OFF_POOL:OFF_Q], w_mix_in[:, OFF_SSM:OFF_KV],
            w_mix_in[:, OFF_KV:OFF_KPE], w_mix_in[:, OFF_Q:OFF_SSM], w_mix_in[:, OFF_KPE:],
            jnp.zeros((D_MODEL, QKP_W - MLA_Q_RANK - MLA_ROPE), w_mix_in.dtype)]
    return jnp.concatenate(cols, axis=1).astype(BF16)


def mix_in(xc, mod3, norm_g, w_in_p, n_ctx):
    bsz, n_all, dim = xc.shape
    assert n_ctx % TOK_ROWS == 0 and n_all % TOK_ROWS == 0 and bsz < MOD_ROWS
    nct = n_ctx // TOK_ROWS
    tile = lambda w: pl.BlockSpec((None, TOK_ROWS, w), lambda b, i: (b, i, 0))

    def mod_spec(chunk):
        return pl.BlockSpec((None, 1, dim), lambda b, i: (jnp.where(i < nct, bsz, b), 0, chunk))
    return pl.pallas_call(
        _mix_in_kernel,
        out_shape=tuple(jax.ShapeDtypeStruct((bsz, n_all, w), F32) for w in IN_SPLITS),
        grid=(bsz, n_all // TOK_ROWS),
        in_specs=[tile(dim), mod_spec(0), mod_spec(1),
                  pl.BlockSpec((1, dim), lambda b, i: (0, 0)),
                  pl.BlockSpec(w_in_p.shape, lambda b, i: (0, 0))],
        out_specs=tuple(tile(w) for w in IN_SPLITS),
        compiler_params=pltpu.CompilerParams(dimension_semantics=("arbitrary", "arbitrary")),
        name="mix_in",
    )(xc, mod3, mod3, norm_g.reshape(1, dim), w_in_p)


LM_PAD = 16
LM_ROWS = 64
LM_DBL_ROWS = 32


def _local_mixers_kernel(cin_ref, pin_ref, dw_ref, dwb_ref, lng_ref, lnb_ref, pw_ref, pwb_ref, psc_ref,
                         ya_ref, yp_ref, upad, ppad, s2, s4, s8, *, n_ctx):
    n_all = cin_ref.shape[0]
    n_lat = n_all - n_ctx
    n_pad = upad.shape[0]
    lat0 = n_ctx + 2 * LM_PAD
    zeros = jnp.zeros((LM_PAD, W_CONV), F32)
    for buf in (upad, ppad):
        buf[0:LM_PAD, :] = zeros
        buf[LM_PAD + n_ctx:lat0, :] = zeros
        buf[lat0 + n_lat:n_pad, :] = zeros

    def pad_row(i):
        return pl.multiple_of(i * LM_ROWS + jnp.where(i < n_ctx // LM_ROWS, LM_PAD, 2 * LM_PAD), 8)

    def fill(i, _):
        r0 = pl.multiple_of(i * LM_ROWS, LM_ROWS)
        c = cin_ref[pl.ds(r0, LM_ROWS), :]
        dst = pl.ds(pad_row(i), LM_ROWS)
        upad[dst, :] = c[:, :W_CONV] * jax.nn.sigmoid(c[:, W_CONV:])
        ppad[dst, :] = pin_ref[pl.ds(r0, LM_ROWS), :]
        return 0
    lax.fori_loop(0, n_all // LM_ROWS, fill, 0)

    def doubling(src, dst, back, fwd):
        dst[0:8, :] = zeros[:8]
        dst[n_pad - 8:n_pad, :] = zeros[:8]

        def body(i, _):
            r0 = pl.multiple_of(8 + i * LM_DBL_ROWS, 8)
            h = src[pl.ds(r0 - 8, LM_DBL_ROWS + 16), :]
            dst[pl.ds(r0, LM_DBL_ROWS), :] = (h[8 - back:8 - back + LM_DBL_ROWS, :]
                                              + h[8 + fwd:8 + fwd + LM_DBL_ROWS, :])
            return 0
        lax.fori_loop(0, (n_pad - 16) // LM_DBL_ROWS, body, 0)
    doubling(ppad, s2, 1, 0)
    doubling(s2, s4, 1, 1)
    doubling(s4, s8, 2, 2)

    lane = lax.broadcasted_iota(jnp.int32, (LM_ROWS, W_POOL), 1)
    grp = lane // POOL_CH
    back = jnp.where(grp == 0, 1, jnp.where(grp == 1, 2, jnp.where(grp == 2, 4, 8)))
    fwd = back - 1

    def tile(i, _):
        r0 = pl.multiple_of(i * LM_ROWS, LM_ROWS)
        p0 = pad_row(i)
        halo = upad[pl.ds(p0 - LM_PAD, LM_ROWS + 2 * LM_PAD), :]
        acc = jnp.zeros((LM_ROWS, W_CONV), F32)
        for k in range(CONV_K):
            o = LM_PAD + k - CONV_K // 2
            acc = acc + halo[o:o + LM_ROWS, :] * dw_ref[k:k + 1, :]
        acc = acc + dwb_ref[...]
        mu = jnp.mean(acc, axis=-1, keepdims=True)
        var = jnp.mean(jnp.square(acc - mu), axis=-1, keepdims=True)
        v = (acc - mu) * lax.rsqrt(var + EPS) * lng_ref[...] + lnb_ref[...]
        v = v * jax.nn.sigmoid(v)
        ya_ref[pl.ds(r0, LM_ROWS), :] = jnp.dot(v.astype(BF16), pw_ref[...],
                                                preferred_element_type=F32).astype(ya_ref.dtype)
        rows = pl.ds(p0, LM_ROWS)
        h8 = s8[pl.ds(p0 - 8, LM_ROWS + 16), :]
        s16 = h8[4:4 + LM_ROWS, :] + h8[12:12 + LM_ROWS, :]
        wsum = jnp.where(grp == 0, s2[rows, :], jnp.where(grp == 1, s4[rows, :],
                                                          jnp.where(grp == 2, s8[rows, :], s16)))
        seg_len = jnp.where(i < n_ctx // LM_ROWS, n_ctx, n_lat)
        t = (r0 - jnp.where(i < n_ctx // LM_ROWS, 0, n_ctx)) + lax.broadcasted_iota(jnp.int32, (LM_ROWS, W_POOL), 0)
        cnt = jnp.minimum(t + fwd, seg_len - 1) - jnp.maximum(t - back, 0) + 1
        pooled = wsum / cnt.astype(F32) - ppad[rows, :]
        y = jnp.dot(pooled.astype(BF16), pwb_ref[...], preferred_element_type=F32) * psc_ref[...]
        yp_ref[pl.ds(r0, LM_ROWS), :] = y.astype(yp_ref.dtype)
        return 0
    lax.fori_loop(0, n_all // LM_ROWS, tile, 0)


def local_mixers(conv_in, pool_in, n_ctx, conv_dw, conv_dw_b, ln_g, ln_b, conv_pw, pool_w, pool_scale):
    bsz, n_all, _ = conv_in.shape
    assert n_ctx % LM_ROWS == 0 and n_all % LM_ROWS == 0
    n_pad = n_all + 3 * LM_PAD
    assert (n_pad - 16) % LM_DBL_ROWS == 0
    dw = jnp.zeros((CONV_K + 1, W_CONV), F32).at[:CONV_K].set(conv_dw)
    eye = jnp.eye(len(POOL_WINDOWS), dtype=F32)
    pwb = jnp.einsum('gcd,gh->gchd', pool_w, eye).reshape(W_POOL, W_POOL).astype(BF16)
    row = lambda v: v.reshape(1, -1).astype(F32)
    params = [dw, row(conv_dw_b), row(ln_g), row(ln_b), conv_pw.astype(BF16), pwb, row(pool_scale)]

    def full(arr):
        return pl.BlockSpec(arr.shape, lambda b: (0,) * arr.ndim)
    seq = lambda w: pl.BlockSpec((None, n_all, w), lambda b: (b, 0, 0))
    out = jax.ShapeDtypeStruct((bsz, n_all, W_CONV), BF16)
    return pl.pallas_call(
        functools.partial(_local_mixers_kernel, n_ctx=n_ctx),
        out_shape=(out, out),
        grid=(bsz,),
        in_specs=[seq(2 * W_CONV), seq(W_POOL)] + [full(p) for p in params],
        out_specs=(seq(W_CONV), seq(W_POOL)),
        scratch_shapes=[pltpu.VMEM((n_pad, W_CONV), F32) for _ in range(5)],
        compiler_params=pltpu.CompilerParams(dimension_semantics=("arbitrary",)),
        name="local_mixers",
    )(conv_in, pool_in, *params)


S5_NS = SSM_GROUPS * SSM_STATE
S5_TC = 128
S5_BG = 8
S5_LANES = 512


def _s5_scan_kernel(u_ref, perm_ref, a_ref, bd_ref, cd_ref, *rest, reverse, final):
    if final:
        yp_ref, permt_ref, d_ref, gw_ref, gb_ref, y_ref, bu_s, hh_s, h_s = rest
    else:
        y_ref, bu_s, hh_s, h_s = rest
    tc = u_ref.shape[1]
    rows = S5_BG * tc
    j = pl.program_id(1)

    @pl.when(j == 0)
    def _():
        h_s[...] = jnp.zeros_like(h_s)

    u = u_ref[...].reshape(rows, W_SSM)
    u_hi = u.astype(BF16)
    if final:
        u_lo = (u - u_hi.astype(F32)).astype(BF16)
        u_tm = jnp.dot(perm_ref[...], jnp.concatenate([u_hi, u_lo], axis=1), preferred_element_type=F32)
        u_hi_tm = u_tm[:, :W_SSM].astype(BF16)
        u_f32_tm = u_tm[:, :W_SSM] + u_tm[:, W_SSM:]
    else:
        u_hi_tm = jnp.dot(perm_ref[...], u_hi, preferred_element_type=F32).astype(BF16)
    bu_s[...] = jnp.dot(u_hi_tm, bd_ref[...], preferred_element_type=F32)

    for hf in range(S5_NS // S5_LANES):
        re_cols = pl.ds(hf * S5_LANES, S5_LANES)
        im_cols = pl.ds(S5_NS + hf * S5_LANES, S5_LANES)
        a_re = jnp.broadcast_to(a_ref[0:1, hf * S5_LANES:(hf + 1) * S5_LANES], (S5_BG, S5_LANES))
        a_im = jnp.broadcast_to(a_ref[1:2, hf * S5_LANES:(hf + 1) * S5_LANES], (S5_BG, S5_LANES))

        def step(i, carry):
            h_re, h_im = carry
            t = (tc - 1 - i) if reverse else i
            r8 = pl.ds(pl.multiple_of(t * S5_BG, S5_BG), S5_BG)
            n_re = a_re * h_re - a_im * h_im + bu_s[r8, re_cols]
            n_im = a_re * h_im + a_im * h_re + bu_s[r8, im_cols]
            hh_s[r8, re_cols] = n_re
            hh_s[r8, im_cols] = n_im
            return n_re, n_im

        h_re, h_im = lax.fori_loop(0, tc, step, (h_s[:, re_cols], h_s[:, im_cols]), unroll=8)
        h_s[:, re_cols] = h_re
        h_s[:, im_cols] = h_im

    y = jnp.dot(hh_s[...].astype(BF16), cd_ref[...], preferred_element_type=F32)
    if final:
        y = y + yp_ref[...] + d_ref[...] * u_f32_tm
        z = jax.nn.gelu(y)
        y = z * jax.nn.sigmoid(jnp.dot(z.astype(BF16), gw_ref[...], preferred_element_type=F32) + gb_ref[...])
        y = jnp.dot(permt_ref[...], y.astype(BF16), preferred_element_type=F32)
        y_ref[...] = y.astype(BF16).reshape(S5_BG, tc, W_SSM)
    else:
        y_ref[...] = y


def s5_direction(u_all, perm, a, bd, cd, n_ctx, reverse, final_args=None):
    bsz, n_all, _ = u_all.shape
    tc = S5_TC
    n_chunks = n_all // tc
    ctx_chunks = n_ctx // tc
    rows = S5_BG * tc
    assert n_all % tc == 0 and n_ctx % tc == 0 and bsz % S5_BG == 0

    if reverse:
        def chunk(j):
            return jnp.where(j < ctx_chunks, ctx_chunks - 1 - j, n_chunks - 1 + ctx_chunks - j)
    else:
        def chunk(j):
            return j
    seq_spec = pl.BlockSpec((S5_BG, tc, W_SSM), lambda g, j: (g, chunk(j), 0))
    part_spec = pl.BlockSpec((None, None, rows, W_SSM), lambda g, j: (g, chunk(j), 0, 0))

    def full(arr):
        return pl.BlockSpec(arr.shape, lambda g, j: (0,) * arr.ndim)
    in_specs = [seq_spec, full(perm), full(a), full(bd), full(cd)]
    args = [u_all, perm, a, bd, cd]
    if final_args is not None:
        y_prev, d, glu_w, glu_b = final_args
        perm_t = perm.T
        in_specs += [part_spec, full(perm_t), full(d), full(glu_w), full(glu_b)]
        args += [y_prev, perm_t, d, glu_w, glu_b]
        out_shape = jax.ShapeDtypeStruct(u_all.shape, BF16)
        out_spec = seq_spec
    else:
        out_shape = jax.ShapeDtypeStruct((bsz // S5_BG, n_chunks, rows, W_SSM), F32)
        out_spec = part_spec
    return pl.pallas_call(
        functools.partial(_s5_scan_kernel, reverse=reverse, final=final_args is not None),
        out_shape=out_shape,
        grid=(bsz // S5_BG, n_chunks),
        in_specs=in_specs,
        out_specs=out_spec,
        scratch_shapes=[pltpu.VMEM((rows, 2 * S5_NS), F32), pltpu.VMEM((rows, 2 * S5_NS), F32),
                        pltpu.VMEM((S5_BG, 2 * S5_NS), F32)],
        compiler_params=pltpu.CompilerParams(dimension_semantics=("arbitrary", "arbitrary")),
        name="s5_scan_bwd" if reverse else "s5_scan_fwd",
    )(*args)


def s5_time_major_perm(tc):
    r = jnp.arange(S5_BG * tc)
    src = (r % S5_BG) * tc + r // S5_BG
    return (src[:, None] == r[None, :]).astype(BF16)


def s5_params(a_re, a_im, log_dt, b_re, b_im, c_re, c_im):
    a_re = jnp.minimum(a_re.astype(F32), -1e-4)
    a_im = a_im.astype(F32)
    dt = jnp.exp(log_dt.astype(F32))[:, None]
    mag = jnp.exp(a_re * dt)
    ab_re = mag * jnp.cos(a_im * dt)
    ab_im = mag * jnp.sin(a_im * dt)
    den = a_re * a_re + a_im * a_im
    f_re = ((ab_re - 1.0) * a_re + ab_im * a_im) / den
    f_im = (ab_im * a_re - (ab_re - 1.0) * a_im) / den
    bb_re = f_re[..., None] * b_re - f_im[..., None] * b_im
    bb_im = f_re[..., None] * b_im + f_im[..., None] * b_re
    eye = jnp.eye(SSM_GROUPS, dtype=F32)

    def in_map(bb):
        return jnp.einsum('gpc,gh->gchp', bb, eye).reshape(W_SSM, S5_NS)

    def out_map(cc):
        return jnp.einsum('gcp,gh->gphc', cc, eye).reshape(S5_NS, W_SSM)
    a = jnp.stack([ab_re.reshape(-1), ab_im.reshape(-1)])
    bd = jnp.concatenate([in_map(bb_re), in_map(bb_im)], axis=1).astype(BF16)
    cd = jnp.concatenate([out_map(c_re.astype(F32)), -out_map(c_im.astype(F32))], axis=0).astype(BF16)
    return a, bd, cd


def s5_mixer_pallas(u_all, n_ctx, a_re, a_im, log_dt, b_re, b_im, c_re, c_im, d, glu_w, glu_b):
    pf = s5_params(a_re[0], a_im[0], log_dt[0], b_re[0], b_im[0], c_re[0], c_im[0])
    pb = s5_params(a_re[1], a_im[1], log_dt[1], b_re[1], b_im[1], c_re[1], c_im[1])
    perm = s5_time_major_perm(S5_TC)
    y_f = s5_direction(u_all, perm, *pf, n_ctx, False)
    return s5_direction(u_all, perm, *pb, n_ctx, True,
                        (y_f, d.reshape(1, W_SSM).astype(F32), glu_w.astype(BF16), glu_b.reshape(1, W_SSM)))


HEAD_W = LANE
ROT_A = MLA_NOPE
ROT_B = MLA_NOPE + MLA_ROPE // 2
ATT_PREP_ROWS = 256
ATT_Q_ROWS = 256


def _head_lane_of_dim():
    half = ROPE_AXIS // 2
    lanes = list(range(MLA_NOPE))
    lanes += [ROT_A + i for i in range(half)] + [ROT_B + i for i in range(half)]
    lanes += [ROT_A + half + i for i in range(half)] + [ROT_B + half + i for i in range(half)]
    return jnp.array(lanes, jnp.int32)


def mla_layout_params(q_a_g, wq_b, kv_a_g, wkv_b, q_g, k_g):
    lane_of = _head_lane_of_dim()
    wq = wq_b.reshape(MLA_Q_RANK, MLA_HEADS, MLA_QK)
    wq_p = jnp.zeros((QKP_W, MLA_HEADS, HEAD_W), F32).at[:MLA_Q_RANK, :, lane_of].set(wq)
    wq_p = wq_p.reshape(QKP_W, MLA_HEADS * HEAD_W).astype(BF16)
    qa_g = jnp.zeros((1, QKP_W), F32).at[0, :MLA_Q_RANK].set(q_a_g)
    wkv = wkv_b.reshape(MLA_KV_RANK, MLA_HEADS, MLA_NOPE + MLA_V)
    wk_p = jnp.zeros((MLA_KV_RANK, MLA_HEADS, HEAD_W), F32).at[:, :, :MLA_NOPE].set(wkv[:, :, :MLA_NOPE])
    wkv_p = jnp.concatenate([wk_p.reshape(MLA_KV_RANK, -1), wkv[:, :, MLA_NOPE:].reshape(MLA_KV_RANK, -1)],
                            axis=1).astype(BF16)
    src = MLA_Q_RANK + jnp.arange(MLA_ROPE)
    place = jnp.zeros((QKP_W, MLA_HEADS, HEAD_W), F32).at[src, :, lane_of[MLA_NOPE:]].set(1.0)
    place = place.reshape(QKP_W, MLA_HEADS * HEAD_W).astype(BF16)

    def head_gain(g):
        return jnp.zeros((1, HEAD_W), F32).at[0, lane_of].set(g)
    return wq_p, qa_g, wkv_p, kv_a_g.reshape(1, MLA_KV_RANK), place, head_gain(q_g), head_gain(k_g)


def rope_lane_tables(n_ctx, n_lat):
    t = jnp.arange(n_lat)
    row = (t // GRID_W).astype(F32)
    col = (t % GRID_W).astype(F32)
    inv = ROPE_BASE ** (-jnp.arange(0, ROPE_AXIS, 2, dtype=F32) / ROPE_AXIS)
    ang = jnp.concatenate([row[:, None] * inv, col[:, None] * inv], axis=1)
    cos, sin = jnp.cos(ang), jnp.sin(ang)
    half = MLA_ROPE // 2
    cmul = jnp.ones((n_lat, HEAD_W), F32).at[:, ROT_A:ROT_A + half].set(cos).at[:, ROT_B:ROT_B + half].set(cos)
    s_up = jnp.zeros((n_lat, HEAD_W), F32).at[:, ROT_A:ROT_A + half].set(-sin)
    s_dn = jnp.zeros((n_lat, HEAD_W), F32).at[:, ROT_B:ROT_B + half].set(sin)
    ctx = [jnp.ones((n_ctx, HEAD_W), F32), jnp.zeros((n_ctx, HEAD_W), F32), jnp.zeros((n_ctx, HEAD_W), F32)]
    return jnp.stack([jnp.concatenate([c, x], axis=0) for c, x in zip(ctx, (cmul, s_up, s_dn))])


def _mla_kernel(ckv_ref, qkp_ref, rope_ref, wq_ref, qag_ref, wkv_ref, kvag_ref, place_ref, qg_ref, kg_ref,
                o_ref, q_s, k_s, v_s, *, n_ctx, ctx_queries):
    n_all = ckv_ref.shape[0]
    half = MLA_ROPE // 2

    def rope(x, r0):
        rows = pl.ds(r0, ATT_PREP_ROWS)
        up = pltpu.roll(x, HEAD_W - half, axis=1)
        dn = pltpu.roll(x, half, axis=1)
        return x * rope_ref[0, rows, :] + up * rope_ref[1, rows, :] + dn * rope_ref[2, rows, :]

    def head_norm(x, g):
        ss = jnp.sum(x * x, axis=-1, keepdims=True) * (1.0 / MLA_QK)
        return x * lax.rsqrt(ss + EPS) * g

    def prep(i, _):
        r0 = pl.multiple_of(i * ATT_PREP_ROWS, ATT_PREP_ROWS)
        rows = pl.ds(r0, ATT_PREP_ROWS)
        ckv = ckv_ref[rows, :]
        kvn = ckv * lax.rsqrt(jnp.mean(ckv * ckv, axis=-1, keepdims=True) + EPS) * kvag_ref[...]
        kv = jnp.dot(kvn.astype(BF16), wkv_ref[...], preferred_element_type=F32)
        qkp = qkp_ref[rows, :]
        p_hi = qkp.astype(BF16)
        p_lo = (qkp - p_hi.astype(F32)).astype(BF16)
        kpe = (jnp.dot(p_hi, place_ref[...], preferred_element_type=F32)
               + jnp.dot(p_lo, place_ref[...], preferred_element_type=F32))
        lane = lax.broadcasted_iota(jnp.int32, qkp.shape, 1)
        qsq = jnp.where(lane < MLA_Q_RANK, qkp * qkp, 0.0)
        qn = qkp * lax.rsqrt(jnp.sum(qsq, axis=-1, keepdims=True) * (1.0 / MLA_Q_RANK) + EPS) * qag_ref[...]
        q = jnp.dot(qn.astype(BF16), wq_ref[...], preferred_element_type=F32)
        for h in range(MLA_HEADS):
            blk = slice(h * HEAD_W, (h + 1) * HEAD_W)
            k_h = rope(head_norm(kv[:, blk] + kpe[:, blk], kg_ref[...]), r0)
            q_h = rope(head_norm(q[:, blk], qg_ref[...]), r0) * (MLA_QK ** -0.5)
            k_s[h, rows, :] = k_h.astype(BF16)
            q_s[h, rows, :] = q_h.astype(BF16)
        v_s[rows, :] = kv[:, MLA_HEADS * HEAD_W:].astype(BF16)
        return 0

    lax.fori_loop(0, n_all // ATT_PREP_ROWS, prep, 0)

    lane = lax.broadcasted_iota(jnp.int32, (ATT_Q_ROWS, LANE), 1)

    def attend_rows(r0, n_keys):
        outs = []
        for pair in range(MLA_HEADS // 2):
            v2 = v_s[pl.ds(0, n_keys), pair * LANE:(pair + 1) * LANE]
            o2 = []
            for h in (2 * pair, 2 * pair + 1):
                q = q_s[h, pl.ds(r0, ATT_Q_ROWS), :]
                k = k_s[h, pl.ds(0, n_keys), :]
                s = lax.dot_general(q, k, (((1,), (1,)), ((), ())), preferred_element_type=F32)
                p = jnp.exp(s - jnp.max(s, axis=-1, keepdims=True))
                den = jnp.sum(p, axis=-1, keepdims=True)
                o2.append(jnp.dot(p.astype(BF16), v2, preferred_element_type=F32) / den)
            outs.append(jnp.where(lane < MLA_V, o2[0], o2[1]))
        return jnp.concatenate(outs, axis=1)

    for i in range(n_ctx // ATT_Q_ROWS):
        r0 = i * ATT_Q_ROWS
        if ctx_queries:
            o_ref[pl.ds(r0, ATT_Q_ROWS), :] = attend_rows(r0, n_ctx).astype(o_ref.dtype)
        else:
            o_ref[pl.ds(r0, ATT_Q_ROWS), :] = jnp.zeros((ATT_Q_ROWS, W_MLA), o_ref.dtype)

    def q_tile(i, _):
        r0 = pl.multiple_of(n_ctx + i * ATT_Q_ROWS, ATT_Q_ROWS)
        o_ref[pl.ds(r0, ATT_Q_ROWS), :] = attend_rows(r0, n_all).astype(o_ref.dtype)
        return 0

    lax.fori_loop(0, (n_all - n_ctx) // ATT_Q_ROWS, q_tile, 0)


def mla_attention(ckv, qkp, rope_tab, params, n_ctx, ctx_queries):
    bsz, n_all, _ = ckv.shape
    assert n_all % ATT_PREP_ROWS == 0 and n_ctx % ATT_Q_ROWS == 0 and (n_all - n_ctx) % ATT_Q_ROWS == 0

    def full(arr):
        return pl.BlockSpec(arr.shape, lambda b: (0,) * arr.ndim)
    return pl.pallas_call(
        functools.partial(_mla_kernel, n_ctx=n_ctx, ctx_queries=ctx_queries),
        out_shape=jax.ShapeDtypeStruct((bsz, n_all, W_MLA), BF16),
        grid=(bsz,),
        in_specs=[pl.BlockSpec((None, n_all, MLA_KV_RANK), lambda b: (b, 0, 0)),
                  pl.BlockSpec((None, n_all, QKP_W), lambda b: (b, 0, 0)),
                  full(rope_tab)] + [full(p) for p in params],
        out_specs=pl.BlockSpec((None, n_all, W_MLA), lambda b: (b, 0, 0)),
        scratch_shapes=[pltpu.VMEM((MLA_HEADS, n_all, HEAD_W), BF16), pltpu.VMEM((MLA_HEADS, n_all, HEAD_W), BF16),
                        pltpu.VMEM((n_all, MLA_HEADS * MLA_V), BF16)],
        compiler_params=pltpu.CompilerParams(dimension_semantics=("arbitrary",)),
        name="mla_attention",
    )(ckv, qkp, rope_tab, *params)


SUBLANE = 8
RUN_ROWS = 1280
RUN_BITS = (256, 128, 64, 32, 16, 8)
PACK_W = D_MODEL // 2
RT_EXPERT, RT_WEIGHT, RT_SLOT = 0, TOP_K, 2 * TOP_K
META_SRC, META_CNT, META_BASE = 0, 1, 2
I32 = jnp.int32
U32 = jnp.uint32


def _mix_out_kernel(ya_ref, yp_ref, ys_ref, ym_ref, x_ref, g1_ref, sh_ref, sc_ref, ng_ref, wo_ref, rw_ref, rb_ref,
                    tri_ref, upper_ref, xo_ref, h_ref, rt_ref, meta_ref, tot_ref, carry_s):
    first = jnp.logical_and(pl.program_id(0) == 0, pl.program_id(1) == 0)

    @pl.when(first)
    def _():
        carry_s[...] = jnp.zeros_like(carry_s)

    acc = jnp.dot(ya_ref[...], wo_ref[0:W_CONV, :], preferred_element_type=F32)
    acc += jnp.dot(yp_ref[...], wo_ref[W_CONV:W_CONV + W_POOL, :], preferred_element_type=F32)
    acc += jnp.dot(ys_ref[...], wo_ref[W_CONV + W_POOL:D_MIX - W_MLA, :], preferred_element_type=F32)
    acc += jnp.dot(ym_ref[...], wo_ref[D_MIX - W_MLA:, :], preferred_element_type=F32)
    x = x_ref[...] + g1_ref[...] * acc
    xo_ref[...] = x
    y = x * lax.rsqrt(jnp.mean(x * x, axis=-1, keepdims=True) + EPS) * ng_ref[...]
    h = (y * (1.0 + sc_ref[...]) + sh_ref[...]).astype(BF16)
    h_ref[...] = h
    logits = jnp.dot(h, rw_ref[...], preferred_element_type=F32) + rb_ref[...]
    lane = lax.broadcasted_iota(I32, logits.shape, 1)
    route = jnp.zeros(logits.shape, F32)
    onehot = jnp.zeros(logits.shape, F32)
    vals, ids = [], []
    for k in range(TOP_K):
        m = jnp.max(logits, axis=-1, keepdims=True)
        idx = jnp.min(jnp.where(logits == m, lane, LANE), axis=-1, keepdims=True)
        hit = lane == idx
        logits = jnp.where(hit, -jnp.inf, logits)
        onehot = jnp.where(hit, 1.0, onehot)
        route = jnp.where(lane == RT_EXPERT + k, idx.astype(F32), route)
        vals.append(m)
        ids.append(idx)
    ex = [jnp.exp(v - vals[0]) for v in vals]
    den = ex[0] + ex[1] + ex[2] + ex[3]
    for k in range(TOP_K):
        route = jnp.where(lane == RT_WEIGHT + k, ex[k] / den, route)
    before = jnp.dot(tri_ref[...], onehot.astype(BF16), preferred_element_type=F32)
    cnt = jnp.sum(onehot, axis=0, keepdims=True)
    cnt8 = jnp.floor((cnt + (SUBLANE - 1)) * (1.0 / SUBLANE)) * SUBLANE
    cnt8_rows = jnp.broadcast_to(cnt8, (SUBLANE, LANE))
    src = jnp.dot(cnt8_rows.astype(BF16), upper_ref[...], preferred_element_type=F32)
    slot_of = before + src[0:1, :]
    for k in range(TOP_K):
        slot = jnp.sum(jnp.where(lane == ids[k], slot_of, 0.0), axis=-1, keepdims=True)
        route = jnp.where(lane == RT_SLOT + k, slot, route)
    rt_ref[...] = route
    base = carry_s[...]
    row = lax.broadcasted_iota(I32, (SUBLANE, LANE), 0)
    meta = jnp.where(row == META_SRC, src, jnp.where(row == META_CNT, cnt8_rows,
                                                     jnp.where(row == META_BASE, base, 0.0)))
    meta_ref[...] = meta.astype(I32)
    carry_s[...] = base + cnt8_rows
    tot_ref[...] = (base + cnt8_rows).astype(I32)


def mix_out(ya, yp, ys, ym, xc, mod3, norm_g, w_out, router_w, router_b, n_ctx, skip_ctx):
    bsz, n_all, dim = xc.shape
    nct = n_ctx // TOK_ROWS
    off = nct if skip_ctx else 0
    n_out = n_all - off * TOK_ROWS
    tiles = n_out // TOK_ROWS
    tile_in = lambda w: pl.BlockSpec((None, TOK_ROWS, w), lambda b, i: (b, i + off, 0))
    tile_out = lambda w: pl.BlockSpec((None, TOK_ROWS, w), lambda b, i: (b, i, 0))

    def mod_spec(chunk):
        return pl.BlockSpec((None, 1, dim), lambda b, i: (jnp.where(i + off < nct, bsz, b), 0, chunk))
    rw = jnp.zeros((dim, LANE), F32).at[:, :N_EXPERTS].set(router_w).astype(BF16)
    rb = jnp.full((1, LANE), -1e30, F32).at[0, :N_EXPERTS].set(router_b)
    r = jnp.arange(TOK_ROWS)
    tri = (r[None, :] < r[:, None]).astype(BF16)
    q = jnp.arange(LANE)
    upper = (q[:, None] < q[None, :]).astype(BF16)
    full = lambda a: pl.BlockSpec(a.shape, lambda b, i: (0,) * a.ndim)
    wo = w_out.astype(BF16)
    ng = norm_g.reshape(1, dim)
    return pl.pallas_call(
        _mix_out_kernel,
        out_shape=(jax.ShapeDtypeStruct((bsz, n_out, dim), F32), jax.ShapeDtypeStruct((bsz, n_out, dim), BF16),
                   jax.ShapeDtypeStruct((bsz, n_out, LANE), F32),
                   jax.ShapeDtypeStruct((bsz * tiles, SUBLANE, LANE), I32),
                   jax.ShapeDtypeStruct((SUBLANE, LANE), I32)),
        grid=(bsz, tiles),
        in_specs=[tile_in(W_CONV), tile_in(W_POOL), tile_in(W_SSM), tile_in(W_MLA), tile_in(dim),
                  mod_spec(2), mod_spec(3), mod_spec(4), full(ng), full(wo), full(rw), full(rb), full(tri),
                  full(upper)],
        out_specs=(tile_out(dim), tile_out(dim), tile_out(LANE),
                   pl.BlockSpec((None, SUBLANE, LANE), lambda b, i: (b * tiles + i, 0, 0)),
                   pl.BlockSpec((SUBLANE, LANE), lambda b, i: (0, 0))),
        scratch_shapes=[pltpu.VMEM((SUBLANE, LANE), F32)],
        compiler_params=pltpu.CompilerParams(dimension_semantics=("arbitrary", "arbitrary")),
        name="mix_out",
    )(ya, yp, ys, ym, xc, mod3, mod3, mod3, ng, wo, rw, rb, tri, upper)


def expert_layout(meta, tot, n_tok):
    n_tiles = meta.shape[0]
    tot8 = tot[0, :N_EXPERTS]
    region = (tot8 + MOE_BLOCK - 1) // MOE_BLOCK * MOE_BLOCK
    g_end = jnp.cumsum(region)
    g_start = g_end - region
    n_blocks = -(-(n_tok * TOP_K + n_tiles * N_EXPERTS * (SUBLANE - 1)) // MOE_BLOCK) + N_EXPERTS
    block_e = jnp.minimum(jnp.searchsorted(g_end, jnp.arange(n_blocks) * MOE_BLOCK, side='right'),
                          N_EXPERTS - 1).astype(I32)
    n_used = (g_end[-1] // MOE_BLOCK).astype(I32).reshape(1)
    g_start_l = jnp.zeros((LANE,), I32).at[:N_EXPERTS].set(g_start.astype(I32))
    runs = jnp.stack([meta[:, META_SRC], meta[:, META_CNT], meta[:, META_BASE] + g_start_l], axis=1)
    tails = jnp.zeros((2, LANE), I32).at[0, :N_EXPERTS].set((g_start + tot8).astype(I32))
    tails = tails.at[1, :N_EXPERTS].set((region - tot8).astype(I32))
    return runs, tails, block_e, n_used, n_blocks


def _run_copies(runs_ref, local, remote, sem, to_remote):
    def per_expert(e, fn):
        s = runs_ref[0, 0, e]
        n = runs_ref[0, 1, e]
        d = runs_ref[0, 2, e]
        for bit in RUN_BITS:
            take = (n & bit) != 0
            loc = local.at[pl.ds(pl.multiple_of(s, SUBLANE), bit), :]
            rem = remote.at[pl.ds(pl.multiple_of(d, SUBLANE), bit), :]
            cp = pltpu.make_async_copy(loc, rem, sem) if to_remote else pltpu.make_async_copy(rem, loc, sem)

            @pl.when(take)
            def _():
                fn(cp)
            s = s + jnp.where(take, bit, 0)
            d = d + jnp.where(take, bit, 0)

    def start(e, c):
        per_expert(e, lambda cp: cp.start())
        return c

    def wait(e, c):
        per_expert(e, lambda cp: cp.wait())
        return c
    lax.fori_loop(0, N_EXPERTS, start, 0)
    lax.fori_loop(0, N_EXPERTS, wait, 0)


def _dispatch_kernel(runs_ref, tails_ref, h_ref, slot_ref, xs_ref, buf, zbuf, sem):
    @pl.when(pl.program_id(0) == 0)
    def _():
        zbuf[...] = jnp.zeros_like(zbuf)

        def per_expert(e, fn):
            d = tails_ref[0, e]
            n = tails_ref[1, e]
            for bit in RUN_BITS:
                take = (n & bit) != 0
                cp = pltpu.make_async_copy(zbuf.at[pl.ds(0, bit), :],
                                           xs_ref.at[pl.ds(pl.multiple_of(d, SUBLANE), bit), :], sem)

                @pl.when(take)
                def _():
                    fn(cp)
                d = d + jnp.where(take, bit, 0)

        def start(e, c):
            per_expert(e, lambda cp: cp.start())
            return c

        def wait(e, c):
            per_expert(e, lambda cp: cp.wait())
            return c
        lax.fori_loop(0, N_EXPERTS, start, 0)
        lax.fori_loop(0, N_EXPERTS, wait, 0)

    j = lax.broadcasted_iota(I32, (RUN_ROWS, TOK_ROWS), 0)
    hit = j == slot_ref[0:1, :]
    for k in range(1, TOP_K):
        hit = jnp.logical_or(hit, j == slot_ref[k:k + 1, :])
    p = jnp.where(hit, 1.0, 0.0).astype(BF16)
    lo = jnp.dot(p, h_ref[:, :PACK_W], preferred_element_type=F32)
    hi = jnp.dot(p, h_ref[:, PACK_W:], preferred_element_type=F32)
    lo_bits = lax.shift_right_logical(pltpu.bitcast(lo, U32), jnp.uint32(16))
    hi_bits = pltpu.bitcast(hi, U32) & jnp.uint32(0xFFFF0000)
    buf[...] = hi_bits | lo_bits
    _run_copies(runs_ref, buf, xs_ref, sem, True)


def dispatch(h, slots_t, runs, tails, n_blocks):
    n_tok, dim = h.shape
    n_tiles = n_tok // TOK_ROWS
    return pl.pallas_call(
        _dispatch_kernel,
        out_shape=jax.ShapeDtypeStruct((n_blocks * MOE_BLOCK, PACK_W), U32),
        grid=(n_tiles,),
        in_specs=[pl.BlockSpec((1, 3, LANE), lambda i: (i, 0, 0), memory_space=pltpu.SMEM),
                  pl.BlockSpec((2, LANE), lambda i: (0, 0), memory_space=pltpu.SMEM),
                  pl.BlockSpec((TOK_ROWS, dim), lambda i: (i, 0)),
                  pl.BlockSpec((None, SUBLANE, TOK_ROWS), lambda i: (i, 0, 0))],
        out_specs=pl.BlockSpec(memory_space=pl.ANY),
        scratch_shapes=[pltpu.VMEM((RUN_ROWS, PACK_W), U32), pltpu.VMEM((RUN_BITS[0], PACK_W), U32),
                        pltpu.SemaphoreType.DMA],
        compiler_params=pltpu.CompilerParams(dimension_semantics=("arbitrary",)),
        name="moe_dispatch",
    )(runs, tails, h, slots_t)


def _expert_block_kernel(be_ref, nu_ref, x_ref, win_ref, bin_ref, wout_ref, bout_ref, o_ref):
    i = pl.program_id(0)

    @pl.when(i < nu_ref[0])
    def _():
        w = x_ref[...]
        x_lo = pltpu.bitcast(lax.shift_left(w, jnp.uint32(16)), F32).astype(BF16)
        x_hi = pltpu.bitcast(w & jnp.uint32(0xFFFF0000), F32).astype(BF16)
        gu = (jnp.dot(x_lo, win_ref[0, :PACK_W, :], preferred_element_type=F32)
              + jnp.dot(x_hi, win_ref[0, PACK_W:, :], preferred_element_type=F32) + bin_ref[0])
        gate = jnp.minimum(gu[:, :D_EXPERT], SWIGLU_LIMIT)
        up = jnp.clip(gu[:, D_EXPERT:], -SWIGLU_LIMIT, SWIGLU_LIMIT)
        act = (up + 1.0) * gate * jax.nn.sigmoid(SWIGLU_ALPHA * gate)
        o_ref[...] = jnp.dot(act.astype(BF16), wout_ref[0], preferred_element_type=F32) + bout_ref[0]

    @pl.when(i >= nu_ref[0])
    def _():
        o_ref[...] = jnp.zeros_like(o_ref)


def expert_blocks(xs, block_e, n_used, w_in, b_in, w_out, b_out):
    n_pad = xs.shape[0]
    dim = w_out.shape[-1]
    n_blocks = n_pad // MOE_BLOCK

    def row_map(i, be, nu):
        return (jnp.minimum(i, nu[0] - 1), 0)

    def exp_map3(i, be, nu):
        return (be[i], 0, 0)

    return pl.pallas_call(
        _expert_block_kernel,
        out_shape=jax.ShapeDtypeStruct((n_pad, dim), F32),
        grid_spec=pltpu.PrefetchScalarGridSpec(
            num_scalar_prefetch=2,
            grid=(n_blocks,),
            in_specs=[
                pl.BlockSpec((MOE_BLOCK, PACK_W), row_map),
                pl.BlockSpec((1, dim, 2 * D_EXPERT), exp_map3),
                pl.BlockSpec((1, 1, 2 * D_EXPERT), exp_map3),
                pl.BlockSpec((1, D_EXPERT, dim), exp_map3),
                pl.BlockSpec((1, 1, dim), exp_map3),
            ],
            out_specs=pl.BlockSpec((MOE_BLOCK, dim), lambda i, be, nu: (i, 0)),
        ),
        compiler_params=pltpu.CompilerParams(dimension_semantics=("arbitrary",)),
        name="expert_blocks",
    )(block_e, n_used, xs, w_in, b_in[:, None, :], w_out, b_out[:, None, :])


def _combine_kernel(runs_ref, ys_ref, rt_ref, x_ref, g_ref, o_ref, ybuf, sem):
    @pl.when(pl.program_id(0) == 0)
    def _():
        ybuf[...] = jnp.zeros_like(ybuf)

    _run_copies(runs_ref, ybuf, ys_ref, sem, False)
    rt = rt_ref[...]
    j = lax.broadcasted_iota(I32, (TOK_ROWS, RUN_ROWS), 1).astype(F32)
    wmat = jnp.zeros((TOK_ROWS, RUN_ROWS), F32)
    for k in range(TOP_K):
        wmat = jnp.where(j == rt[:, RT_SLOT + k:RT_SLOT + k + 1], rt[:, RT_WEIGHT + k:RT_WEIGHT + k + 1], wmat)
    w_hi = wmat.astype(BF16)
    w_lo = (wmat - w_hi.astype(F32)).astype(BF16)
    y = ybuf[...]
    y_hi = y.astype(BF16)
    y_lo = (y - y_hi.astype(F32)).astype(BF16)
    m = (jnp.dot(w_hi, y_hi, preferred_element_type=F32) + jnp.dot(w_hi, y_lo, preferred_element_type=F32)
         + jnp.dot(w_lo, y_hi, preferred_element_type=F32))
    o_ref[...] = x_ref[...] + g_ref[...] * m


def combine(ys, route, runs, xn, mod3, n_ctx_tiles):
    bsz, n_rows, dim = xn.shape
    tiles = n_rows // TOK_ROWS
    tile = lambda w: pl.BlockSpec((None, TOK_ROWS, w), lambda i: (i // tiles, i % tiles, 0))
    g_spec = pl.BlockSpec((None, 1, dim), lambda i: (jnp.where(i % tiles < n_ctx_tiles, bsz, i // tiles), 0, 5))
    return pl.pallas_call(
        _combine_kernel,
        out_shape=jax.ShapeDtypeStruct(xn.shape, F32),
        grid=(bsz * tiles,),
        in_specs=[pl.BlockSpec((1, 3, LANE), lambda i: (i, 0, 0), memory_space=pltpu.SMEM),
                  pl.BlockSpec(memory_space=pl.ANY), tile(LANE), tile(dim), g_spec],
        out_specs=tile(dim),
        scratch_shapes=[pltpu.VMEM((RUN_ROWS, dim), F32), pltpu.SemaphoreType.DMA],
        compiler_params=pltpu.CompilerParams(dimension_semantics=("arbitrary",)),
        name="moe_combine",
    )(runs, ys, route, xn, mod3)


def routed_experts(xn, h2, route, meta, tot, mod3, n_ctx_tiles, w_in, b_in, w_out, b_out):
    bsz, n_rows, dim = xn.shape
    n_tok = bsz * n_rows
    runs, tails, block_e, n_used, n_blocks = expert_layout(meta, tot, n_tok)
    slots = route.reshape(n_tok // TOK_ROWS, TOK_ROWS, LANE)[:, :, RT_SLOT:RT_SLOT + SUBLANE]
    slots_t = jnp.swapaxes(slots, 1, 2).astype(I32)
    xs = dispatch(h2.reshape(n_tok, dim), slots_t, runs, tails, n_blocks)
    ys = expert_blocks(xs, block_e, n_used, w_in.astype(BF16), b_in, w_out.astype(BF16), b_out)
    return combine(ys, route, runs, xn, mod3, n_ctx_tiles)


def kernel(x, c, ctx, c_ctx, ada_w, ada_b, norm1_g, norm2_g, w_mix_in, w_mix_out,
           conv_dw, conv_dw_b, conv_ln_g, conv_ln_b, conv_pw, pool_w, pool_scale,
           ssm_a_re, ssm_a_im, ssm_log_dt, ssm_b_re, ssm_b_im, ssm_c_re, ssm_c_im,
           ssm_d, ssm_glu_w, ssm_glu_b,
           mla_q_a_g, mla_wq_b, mla_kv_a_g, mla_wkv_b, mla_q_g, mla_k_g,
           router_w, router_b, exp_w_in, exp_b_in, exp_w_out, exp_b_out):
    bsz, n_lat, dim = x.shape
    n_ctx = ctx.shape[1]
    xc = jnp.concatenate([ctx, x], axis=1)
    c_all = jnp.zeros((MOD_ROWS, dim), F32).at[:bsz].set(c).at[bsz].set(c_ctx)
    rope_tab = rope_lane_tables(n_ctx, n_lat)
    for l in range(DEPTH):
        last = l == DEPTH - 1
        mod = modulation(c_all, ada_w[l], ada_b[l])
        mod3 = mod.reshape(MOD_ROWS, 1, 6 * dim)

        conv_in, pool_in, ssm_in, ckv, qkp = mix_in(xc, mod3, norm1_g[l], mix_in_weights(w_mix_in[l]), n_ctx)
        ya, yp = local_mixers(conv_in, pool_in, n_ctx, conv_dw[l], conv_dw_b[l], conv_ln_g[l], conv_ln_b[l],
                              conv_pw[l], pool_w[l], pool_scale[l])
        ys = s5_mixer_pallas(ssm_in, n_ctx, ssm_a_re[l], ssm_a_im[l], ssm_log_dt[l],
                             ssm_b_re[l], ssm_b_im[l], ssm_c_re[l], ssm_c_im[l],
                             ssm_d[l], ssm_glu_w[l], ssm_glu_b[l])
        ym = mla_attention(ckv, qkp, rope_tab,
                           mla_layout_params(mla_q_a_g[l], mla_wq_b[l], mla_kv_a_g[l], mla_wkv_b[l],
                                             mla_q_g[l], mla_k_g[l]), n_ctx, not last)
        xn, h2, route, meta, tot = mix_out(ya, yp, ys, ym, xc, mod3, norm2_g[l], w_mix_out[l],
                                           router_w[l], router_b[l], n_ctx, last)
        xc = routed_experts(xn, h2, route, meta, tot, mod3, 0 if last else n_ctx // TOK_ROWS,
                            exp_w_in[l], exp_b_in[l], exp_w_out[l], exp_b_out[l])
    return xc
```

```python
import functools

import jax
import jax.numpy as jnp
from jax import lax
from jax.experimental import pallas as pl
from jax.experimental.pallas import tpu as pltpu

D_MODEL = 1024
DEPTH = 2
GRID_W = 64
EPS = 1e-6

D_MIX = D_MODEL
W_CONV = D_MIX // 4
W_POOL = D_MIX // 4
W_SSM = D_MIX // 4
W_MLA = D_MIX - W_CONV - W_POOL - W_SSM
CONV_K = 31
POOL_WINDOWS = (2, 4, 8, 16)
POOL_CH = W_POOL // len(POOL_WINDOWS)
SSM_CH = 16
SSM_GROUPS = W_SSM // SSM_CH
SSM_STATE = 64
MLA_V = 64
MLA_HEADS = W_MLA // MLA_V
MLA_NOPE = 64
MLA_ROPE = 32
MLA_QK = MLA_NOPE + MLA_ROPE
MLA_Q_RANK = 192
MLA_KV_RANK = 128
ROPE_AXIS = MLA_ROPE // 2
ROPE_BASE = 10000.0
N_EXPERTS = 32
TOP_K = 4
D_EXPERT = D_MODEL
SWIGLU_LIMIT = 7.0
SWIGLU_ALPHA = 1.702
MOE_BLOCK = 512

OFF_POOL = 2 * W_CONV
OFF_Q = OFF_POOL + W_POOL
OFF_SSM = OFF_Q + MLA_Q_RANK
OFF_KV = OFF_SSM + W_SSM
OFF_KPE = OFF_KV + MLA_KV_RANK
N_IN = OFF_KPE + MLA_ROPE

F32 = jnp.float32
BF16 = jnp.bfloat16
LANE = 128
QKP_W = 256


MOD_ROWS = 40
MOD_COLS = 1024


def _mod_kernel(c_ref, w_ref, b_ref, o_ref):
    c = c_ref[...]
    s = (c * jax.nn.sigmoid(c)).astype(BF16)
    o_ref[...] = jnp.dot(s, w_ref[...].astype(BF16), preferred_element_type=F32) + b_ref[...]


def modulation(c_all, ada_w, ada_b):
    n_out = ada_w.shape[1]
    return pl.pallas_call(
        _mod_kernel,
        out_shape=jax.ShapeDtypeStruct((MOD_ROWS, n_out), F32),
        grid=(n_out // MOD_COLS,),
        in_specs=[pl.BlockSpec((MOD_ROWS, D_MODEL), lambda j: (0, 0)),
                  pl.BlockSpec((D_MODEL, MOD_COLS), lambda j: (0, j)),
                  pl.BlockSpec((1, MOD_COLS), lambda j: (0, j))],
        out_specs=pl.BlockSpec((MOD_ROWS, MOD_COLS), lambda j: (0, j)),
        compiler_params=pltpu.CompilerParams(dimension_semantics=("arbitrary",)),
        name="modulation",
    )(c_all, ada_w, ada_b.reshape(1, n_out))


TOK_ROWS = 256
IN_SPLITS = (2 * W_CONV, W_POOL, W_SSM, MLA_KV_RANK, QKP_W)


def _mix_in_kernel(x_ref, sh_ref, sc_ref, g_ref, w_ref, *outs):
    x = x_ref[...]
    y = x * lax.rsqrt(jnp.mean(x * x, axis=-1, keepdims=True) + EPS) * g_ref[...]
    h = y * (1.0 + sc_ref[...]) + sh_ref[...]
    p = jnp.dot(h.astype(BF16), w_ref[...], preferred_element_type=F32)
    off = 0
    for o_ref, w in zip(outs, IN_SPLITS):
        o_ref[...] = p[:, off:off + w]
        off += w


def mix_in_weights(w_mix_in):
    cols = [w_mix_in[:, :OFF_POOL], w_mix_in[:, OFF_POOL:OFF_Q], w_mix_in[:, OFF_SSM:OFF_KV],
            w_mix_in[:, OFF_KV:OFF_KPE], w_mix_in[:, OFF_Q:OFF_SSM], w_mix_in[:, OFF_KPE:],
            jnp.zeros((D_MODEL, QKP_W - MLA_Q_RANK - MLA_ROPE), w_mix_in.dtype)]
    return jnp.concatenate(cols, axis=1).astype(BF16)


def mix_in(xc, mod3, norm_g, w_in_p, n_ctx):
    bsz, n_all, dim = xc.shape
    assert n_ctx % TOK_ROWS == 0 and n_all % TOK_ROWS == 0 and bsz < MOD_ROWS
    nct = n_ctx // TOK_ROWS
    tile = lambda w: pl.BlockSpec((None, TOK_ROWS, w), lambda b, i: (b, i, 0))

    def mod_spec(chunk):
        return pl.BlockSpec((None, 1, dim), lambda b, i: (jnp.where(i < nct, bsz, b), 0, chunk))
    return pl.pallas_call(
        _mix_in_kernel,
        out_shape=tuple(jax.ShapeDtypeStruct((bsz, n_all, w), F32) for w in IN_SPLITS),
        grid=(bsz, n_all // TOK_ROWS),
        in_specs=[tile(dim), mod_spec(0), mod_spec(1),
                  pl.BlockSpec((1, dim), lambda b, i: (0, 0)),
                  pl.BlockSpec(w_in_p.shape, lambda b, i: (0, 0))],
        out_specs=tuple(tile(w) for w in IN_SPLITS),
        compiler_params=pltpu.CompilerParams(dimension_semantics=("arbitrary", "arbitrary")),
        name="mix_in",
    )(xc, mod3, mod3, norm_g.reshape(1, dim), w_in_p)


LM_PAD = 16
LM_ROWS = 64
LM_DBL_ROWS = 32


def _local_mixers_kernel(cin_ref, pin_ref, dw_ref, dwb_ref, lng_ref, lnb_ref, pw_ref, pwb_ref, psc_ref,
                         ya_ref, yp_ref, upad, ppad, s2, s4, s8, *, n_ctx):
    n_all = cin_ref.shape[0]
    n_lat = n_all - n_ctx
    n_pad = upad.shape[0]
    lat0 = n_ctx + 2 * LM_PAD
    zeros = jnp.zeros((LM_PAD, W_CONV), F32)
    for buf in (upad, ppad):
        buf[0:LM_PAD, :] = zeros
        buf[LM_PAD + n_ctx:lat0, :] = zeros
        buf[lat0 + n_lat:n_pad, :] = zeros

    def pad_row(i):
        return pl.multiple_of(i * LM_ROWS + jnp.where(i < n_ctx // LM_ROWS, LM_PAD, 2 * LM_PAD), 8)

    def fill(i, _):
        r0 = pl.multiple_of(i * LM_ROWS, LM_ROWS)
        c = cin_ref[pl.ds(r0, LM_ROWS), :]
        dst = pl.ds(pad_row(i), LM_ROWS)
        upad[dst, :] = c[:, :W_CONV] * jax.nn.sigmoid(c[:, W_CONV:])
        ppad[dst, :] = pin_ref[pl.ds(r0, LM_ROWS), :]
        return 0
    lax.fori_loop(0, n_all // LM_ROWS, fill, 0)

    def doubling(src, dst, back, fwd):
        dst[0:8, :] = zeros[:8]
        dst[n_pad - 8:n_pad, :] = zeros[:8]

        def body(i, _):
            r0 = pl.multiple_of(8 + i * LM_DBL_ROWS, 8)
            h = src[pl.ds(r0 - 8, LM_DBL_ROWS + 16), :]
            dst[pl.ds(r0, LM_DBL_ROWS), :] = (h[8 - back:8 - back + LM_DBL_ROWS, :]
                                              + h[8 + fwd:8 + fwd + LM_DBL_ROWS, :])
            return 0
        lax.fori_loop(0, (n_pad - 16) // LM_DBL_ROWS, body, 0)
    doubling(ppad, s2, 1, 0)
    doubling(s2, s4, 1, 1)
    doubling(s4, s8, 2, 2)

    lane = lax.broadcasted_iota(jnp.int32, (LM_ROWS, W_POOL), 1)
    grp = lane // POOL_CH
    back = jnp.where(grp == 0, 1, jnp.where(grp == 1, 2, jnp.where(grp == 2, 4, 8)))
    fwd = back - 1

    def tile(i, _):
        r0 = pl.multiple_of(i * LM_ROWS, LM_ROWS)
        p0 = pad_row(i)
        halo = upad[pl.ds(p0 - LM_PAD, LM_ROWS + 2 * LM_PAD), :]
        acc = jnp.zeros((LM_ROWS, W_CONV), F32)
        for k in range(CONV_K):
            o = LM_PAD + k - CONV_K // 2
            acc = acc + halo[o:o + LM_ROWS, :] * dw_ref[k:k + 1, :]
        acc = acc + dwb_ref[...]
        mu = jnp.mean(acc, axis=-1, keepdims=True)
        var = jnp.mean(jnp.square(acc - mu), axis=-1, keepdims=True)
        v = (acc - mu) * lax.rsqrt(var + EPS) * lng_ref[...] + lnb_ref[...]
        v = v * jax.nn.sigmoid(v)
        ya_ref[pl.ds(r0, LM_ROWS), :] = jnp.dot(v.astype(BF16), pw_ref[...],
                                                preferred_element_type=F32).astype(ya_ref.dtype)
        rows = pl.ds(p0, LM_ROWS)
        h8 = s8[pl.ds(p0 - 8, LM_ROWS + 16), :]
        s16 = h8[4:4 + LM_ROWS, :] + h8[12:12 + LM_ROWS, :]
        wsum = jnp.where(grp == 0, s2[rows, :], jnp.where(grp == 1, s4[rows, :],
                                                          jnp.where(grp == 2, s8[rows, :], s16)))
        seg_len = jnp.where(i < n_ctx // LM_ROWS, n_ctx, n_lat)
        t = (r0 - jnp.where(i < n_ctx // LM_ROWS, 0, n_ctx)) + lax.broadcasted_iota(jnp.int32, (LM_ROWS, W_POOL), 0)
        cnt = jnp.minimum(t + fwd, seg_len - 1) - jnp.maximum(t - back, 0) + 1
        pooled = wsum / cnt.astype(F32) - ppad[rows, :]
        y = jnp.dot(pooled.astype(BF16), pwb_ref[...], preferred_element_type=F32) * psc_ref[...]
        yp_ref[pl.ds(r0, LM_ROWS), :] = y.astype(yp_ref.dtype)
        return 0
    lax.fori_loop(0, n_all // LM_ROWS, tile, 0)


def local_mixers(conv_in, pool_in, n_ctx, conv_dw, conv_dw_b, ln_g, ln_b, conv_pw, pool_w, pool_scale):
    bsz, n_all, _ = conv_in.shape
    assert n_ctx % LM_ROWS == 0 and n_all % LM_ROWS == 0
    n_pad = n_all + 3 * LM_PAD
    assert (n_pad - 16) % LM_DBL_ROWS == 0
    dw = jnp.zeros((CONV_K + 1, W_CONV), F32).at[:CONV_K].set(conv_dw)
    eye = jnp.eye(len(POOL_WINDOWS), dtype=F32)
    pwb = jnp.einsum('gcd,gh->gchd', pool_w, eye).reshape(W_POOL, W_POOL).astype(BF16)
    row = lambda v: v.reshape(1, -1).astype(F32)
    params = [dw, row(conv_dw_b), row(ln_g), row(ln_b), conv_pw.astype(BF16), pwb, row(pool_scale)]

    def full(arr):
        return pl.BlockSpec(arr.shape, lambda b: (0,) * arr.ndim)
    seq = lambda w: pl.BlockSpec((None, n_all, w), lambda b: (b, 0, 0))
    out = jax.ShapeDtypeStruct((bsz, n_all, W_CONV), BF16)
    return pl.pallas_call(
        functools.partial(_local_mixers_kernel, n_ctx=n_ctx),
        out_shape=(out, out),
        grid=(bsz,),
        in_specs=[seq(2 * W_CONV), seq(W_POOL)] + [full(p) for p in params],
        out_specs=(seq(W_CONV), seq(W_POOL)),
        scratch_shapes=[pltpu.VMEM((n_pad, W_CONV), F32) for _ in range(5)],
        compiler_params=pltpu.CompilerParams(dimension_semantics=("arbitrary",)),
        name="local_mixers",
    )(conv_in, pool_in, *params)


S5_NS = SSM_GROUPS * SSM_STATE
S5_TC = 128
S5_BG = 8
S5_LANES = 512


def _s5_scan_kernel(u_ref, perm_ref, a_ref, bd_ref, cd_ref, *rest, reverse, final):
    if final:
        yp_ref, permt_ref, d_ref, gw_ref, gb_ref, y_ref, bu_s, hh_s, h_s = rest
    else:
        y_ref, bu_s, hh_s, h_s = rest
    tc = u_ref.shape[1]
    rows = S5_BG * tc
    j = pl.program_id(1)

    @pl.when(j == 0)
    def _():
        h_s[...] = jnp.zeros_like(h_s)

    u = u_ref[...].reshape(rows, W_SSM)
    u_hi = u.astype(BF16)
    if final:
        u_lo = (u - u_hi.astype(F32)).astype(BF16)
        u_tm = jnp.dot(perm_ref[...], jnp.concatenate([u_hi, u_lo], axis=1), preferred_element_type=F32)
        u_hi_tm = u_tm[:, :W_SSM].astype(BF16)
        u_f32_tm = u_tm[:, :W_SSM] + u_tm[:, W_SSM:]
    else:
        u_hi_tm = jnp.dot(perm_ref[...], u_hi, preferred_element_type=F32).astype(BF16)
    bu_s[...] = jnp.dot(u_hi_tm, bd_ref[...], preferred_element_type=F32)

    for hf in range(S5_NS // S5_LANES):
        re_cols = pl.ds(hf * S5_LANES, S5_LANES)
        im_cols = pl.ds(S5_NS + hf * S5_LANES, S5_LANES)
        a_re = jnp.broadcast_to(a_ref[0:1, hf * S5_LANES:(hf + 1) * S5_LANES], (S5_BG, S5_LANES))
        a_im = jnp.broadcast_to(a_ref[1:2, hf * S5_LANES:(hf + 1) * S5_LANES], (S5_BG, S5_LANES))

        def step(i, carry):
            h_re, h_im = carry
            t = (tc - 1 - i) if reverse else i
            r8 = pl.ds(pl.multiple_of(t * S5_BG, S5_BG), S5_BG)
            n_re = a_re * h_re - a_im * h_im + bu_s[r8, re_cols]
            n_im = a_re * h_im + a_im * h_re + bu_s[r8, im_cols]
            hh_s[r8, re_cols] = n_re
            hh_s[r8, im_cols] = n_im
            return n_re, n_im

        h_re, h_im = lax.fori_loop(0, tc, step, (h_s[:, re_cols], h_s[:, im_cols]), unroll=8)
        h_s[:, re_cols] = h_re
        h_s[:, im_cols] = h_im

    y = jnp.dot(hh_s[...].astype(BF16), cd_ref[...], preferred_element_type=F32)
    if final:
        y = y + yp_ref[...] + d_ref[...] * u_f32_tm
        z = jax.nn.gelu(y)
        y = z * jax.nn.sigmoid(jnp.dot(z.astype(BF16), gw_ref[...], preferred_element_type=F32) + gb_ref[...])
        y = jnp.dot(permt_ref[...], y.astype(BF16), preferred_element_type=F32)
        y_ref[...] = y.astype(BF16).reshape(S5_BG, tc, W_SSM)
    else:
        y_ref[...] = y


def s5_direction(u_all, perm, a, bd, cd, n_ctx, reverse, final_args=None):
    bsz, n_all, _ = u_all.shape
    tc = S5_TC
    n_chunks = n_all // tc
    ctx_chunks = n_ctx // tc
    rows = S5_BG * tc
    assert n_all % tc == 0 and n_ctx % tc == 0 and bsz % S5_BG == 0

    if reverse:
        def chunk(j):
            return jnp.where(j < ctx_chunks, ctx_chunks - 1 - j, n_chunks - 1 + ctx_chunks - j)
    else:
        def chunk(j):
            return j
    seq_spec = pl.BlockSpec((S5_BG, tc, W_SSM), lambda g, j: (g, chunk(j), 0))
    part_spec = pl.BlockSpec((None, None, rows, W_SSM), lambda g, j: (g, chunk(j), 0, 0))

    def full(arr):
        return pl.BlockSpec(arr.shape, lambda g, j: (0,) * arr.ndim)
    in_specs = [seq_spec, full(perm), full(a), full(bd), full(cd)]
    args = [u_all, perm, a, bd, cd]
    if final_args is not None:
        y_prev, d, glu_w, glu_b = final_args
        perm_t = perm.T
        in_specs += [part_spec, full(perm_t), full(d), full(glu_w), full(glu_b)]
        args += [y_prev, perm_t, d, glu_w, glu_b]
        out_shape = jax.ShapeDtypeStruct(u_all.shape, BF16)
        out_spec = seq_spec
    else:
        out_shape = jax.ShapeDtypeStruct((bsz // S5_BG, n_chunks, rows, W_SSM), F32)
        out_spec = part_spec
    return pl.pallas_call(
        functools.partial(_s5_scan_kernel, reverse=reverse, final=final_args is not None),
        out_shape=out_shape,
        grid=(bsz // S5_BG, n_chunks),
        in_specs=in_specs,
        out_specs=out_spec,
        scratch_shapes=[pltpu.VMEM((rows, 2 * S5_NS), F32), pltpu.VMEM((rows, 2 * S5_NS), F32),
                        pltpu.VMEM((S5_BG, 2 * S5_NS), F32)],
        compiler_params=pltpu.CompilerParams(dimension_semantics=("arbitrary", "arbitrary")),
        name="s5_scan_bwd" if reverse else "s5_scan_fwd",
    )(*args)


def s5_time_major_perm(tc):
    r = jnp.arange(S5_BG * tc)
    src = (r % S5_BG) * tc + r // S5_BG
    return (src[:, None] == r[None, :]).astype(BF16)


def s5_params(a_re, a_im, log_dt, b_re, b_im, c_re, c_im):
    a_re = jnp.minimum(a_re.astype(F32), -1e-4)
    a_im = a_im.astype(F32)
    dt = jnp.exp(log_dt.astype(F32))[:, None]
    mag = jnp.exp(a_re * dt)
    ab_re = mag * jnp.cos(a_im * dt)
    ab_im = mag * jnp.sin(a_im * dt)
    den = a_re * a_re + a_im * a_im
    f_re = ((ab_re - 1.0) * a_re + ab_im * a_im) / den
    f_im = (ab_im * a_re - (ab_re - 1.0) * a_im) / den
    bb_re = f_re[..., None] * b_re - f_im[..., None] * b_im
    bb_im = f_re[..., None] * b_im + f_im[..., None] * b_re
    eye = jnp.eye(SSM_GROUPS, dtype=F32)

    def in_map(bb):
        return jnp.einsum('gpc,gh->gchp', bb, eye).reshape(W_SSM, S5_NS)

    def out_map(cc):
        return jnp.einsum('gcp,gh->gphc', cc, eye).reshape(S5_NS, W_SSM)
    a = jnp.stack([ab_re.reshape(-1), ab_im.reshape(-1)])
    bd = jnp.concatenate([in_map(bb_re), in_map(bb_im)], axis=1).astype(BF16)
    cd = jnp.concatenate([out_map(c_re.astype(F32)), -out_map(c_im.astype(F32))], axis=0).astype(BF16)
    return a, bd, cd


def s5_mixer_pallas(u_all, n_ctx, a_re, a_im, log_dt, b_re, b_im, c_re, c_im, d, glu_w, glu_b):
    pf = s5_params(a_re[0], a_im[0], log_dt[0], b_re[0], b_im[0], c_re[0], c_im[0])
    pb = s5_params(a_re[1], a_im[1], log_dt[1], b_re[1], b_im[1], c_re[1], c_im[1])
    perm = s5_time_major_perm(S5_TC)
    y_f = s5_direction(u_all, perm, *pf, n_ctx, False)
    return s5_direction(u_all, perm, *pb, n_ctx, True,
                        (y_f, d.reshape(1, W_SSM).astype(F32), glu_w.astype(BF16), glu_b.reshape(1, W_SSM)))


HEAD_W = LANE
ROT_A = MLA_NOPE
ROT_B = MLA_NOPE + MLA_ROPE // 2
ATT_PREP_ROWS = 256
ATT_Q_ROWS = 256


def _head_lane_of_dim():
    half = ROPE_AXIS // 2
    lanes = list(range(MLA_NOPE))
    lanes += [ROT_A + i for i in range(half)] + [ROT_B + i for i in range(half)]
    lanes += [ROT_A + half + i for i in range(half)] + [ROT_B + half + i for i in range(half)]
    return jnp.array(lanes, jnp.int32)


def mla_layout_params(q_a_g, wq_b, kv_a_g, wkv_b, q_g, k_g):
    lane_of = _head_lane_of_dim()
    wq = wq_b.reshape(MLA_Q_RANK, MLA_HEADS, MLA_QK)
    wq_p = jnp.zeros((QKP_W, MLA_HEADS, HEAD_W), F32).at[:MLA_Q_RANK, :, lane_of].set(wq)
    wq_p = wq_p.reshape(QKP_W, MLA_HEADS * HEAD_W).astype(BF16)
    qa_g = jnp.zeros((1, QKP_W), F32).at[0, :MLA_Q_RANK].set(q_a_g)
    wkv = wkv_b.reshape(MLA_KV_RANK, MLA_HEADS, MLA_NOPE + MLA_V)
    wk_p = jnp.zeros((MLA_KV_RANK, MLA_HEADS, HEAD_W), F32).at[:, :, :MLA_NOPE].set(wkv[:, :, :MLA_NOPE])
    wkv_p = jnp.concatenate([wk_p.reshape(MLA_KV_RANK, -1), wkv[:, :, MLA_NOPE:].reshape(MLA_KV_RANK, -1)],
                            axis=1).astype(BF16)
    src = MLA_Q_RANK + jnp.arange(MLA_ROPE)
    place = jnp.zeros((QKP_W, MLA_HEADS, HEAD_W), F32).at[src, :, lane_of[MLA_NOPE:]].set(1.0)
    place = place.reshape(QKP_W, MLA_HEADS * HEAD_W).astype(BF16)

    def head_gain(g):
        return jnp.zeros((1, HEAD_W), F32).at[0, lane_of].set(g)
    return wq_p, qa_g, wkv_p, kv_a_g.reshape(1, MLA_KV_RANK), place, head_gain(q_g), head_gain(k_g)


def rope_lane_tables(n_ctx, n_lat):
    t = jnp.arange(n_lat)
    row = (t // GRID_W).astype(F32)
    col = (t % GRID_W).astype(F32)
    inv = ROPE_BASE ** (-jnp.arange(0, ROPE_AXIS, 2, dtype=F32) / ROPE_AXIS)
    ang = jnp.concatenate([row[:, None] * inv, col[:, None] * inv], axis=1)
    cos, sin = jnp.cos(ang), jnp.sin(ang)
    half = MLA_ROPE // 2
    cmul = jnp.ones((n_lat, HEAD_W), F32).at[:, ROT_A:ROT_A + half].set(cos).at[:, ROT_B:ROT_B + half].set(cos)
    s_up = jnp.zeros((n_lat, HEAD_W), F32).at[:, ROT_A:ROT_A + half].set(-sin)
    s_dn = jnp.zeros((n_lat, HEAD_W), F32).at[:, ROT_B:ROT_B + half].set(sin)
    ctx = [jnp.ones((n_ctx, HEAD_W), F32), jnp.zeros((n_ctx, HEAD_W), F32), jnp.zeros((n_ctx, HEAD_W), F32)]
    return jnp.stack([jnp.concatenate([c, x], axis=0) for c, x in zip(ctx, (cmul, s_up, s_dn))])


def _mla_kernel(ckv_ref, qkp_ref, rope_ref, wq_ref, qag_ref, wkv_ref, kvag_ref, place_ref, qg_ref, kg_ref,
                o_ref, q_s, k_s, v_s, *, n_ctx, ctx_queries):
    n_all = ckv_ref.shape[0]
    half = MLA_ROPE // 2

    def rope(x, r0):
        rows = pl.ds(r0, ATT_PREP_ROWS)
        up = pltpu.roll(x, HEAD_W - half, axis=1)
        dn = pltpu.roll(x, half, axis=1)
        return x * rope_ref[0, rows, :] + up * rope_ref[1, rows, :] + dn * rope_ref[2, rows, :]

    def head_norm(x, g):
        ss = jnp.sum(x * x, axis=-1, keepdims=True) * (1.0 / MLA_QK)
        return x * lax.rsqrt(ss + EPS) * g

    def prep(i, _):
        r0 = pl.multiple_of(i * ATT_PREP_ROWS, ATT_PREP_ROWS)
        rows = pl.ds(r0, ATT_PREP_ROWS)
        ckv = ckv_ref[rows, :]
        kvn = ckv * lax.rsqrt(jnp.mean(ckv * ckv, axis=-1, keepdims=True) + EPS) * kvag_ref[...]
        kv = jnp.dot(kvn.astype(BF16), wkv_ref[...], preferred_element_type=F32)
        qkp = qkp_ref[rows, :]
        p_hi = qkp.astype(BF16)
        p_lo = (qkp - p_hi.astype(F32)).astype(BF16)
        kpe = (jnp.dot(p_hi, place_ref[...], preferred_element_type=F32)
               + jnp.dot(p_lo, place_ref[...], preferred_element_type=F32))
        lane = lax.broadcasted_iota(jnp.int32, qkp.shape, 1)
        qsq = jnp.where(lane < MLA_Q_RANK, qkp * qkp, 0.0)
        qn = qkp * lax.rsqrt(jnp.sum(qsq, axis=-1, keepdims=True) * (1.0 / MLA_Q_RANK) + EPS) * qag_ref[...]
        q = jnp.dot(qn.astype(BF16), wq_ref[...], preferred_element_type=F32)
        for h in range(MLA_HEADS):
            blk = slice(h * HEAD_W, (h + 1) * HEAD_W)
            k_h = rope(head_norm(kv[:, blk] + kpe[:, blk], kg_ref[...]), r0)
            q_h = rope(head_norm(q[:, blk], qg_ref[...]), r0) * (MLA_QK ** -0.5)
            k_s[h, rows, :] = k_h.astype(BF16)
            q_s[h, rows, :] = q_h.astype(BF16)
        v_s[rows, :] = kv[:, MLA_HEADS * HEAD_W:].astype(BF16)
        return 0

    lax.fori_loop(0, n_all // ATT_PREP_ROWS, prep, 0)

    lane = lax.broadcasted_iota(jnp.int32, (ATT_Q_ROWS, LANE), 1)

    def attend_rows(r0, n_keys):
        outs = []
        for pair in range(MLA_HEADS // 2):
            v2 = v_s[pl.ds(0, n_keys), pair * LANE:(pair + 1) * LANE]
            o2 = []
            for h in (2 * pair, 2 * pair + 1):
                q = q_s[h, pl.ds(r0, ATT_Q_ROWS), :]
                k = k_s[h, pl.ds(0, n_keys), :]
                s = lax.dot_general(q, k, (((1,), (1,)), ((), ())), preferred_element_type=F32)
                p = jnp.exp(s - jnp.max(s, axis=-1, keepdims=True))
                den = jnp.sum(p, axis=-1, keepdims=True)
                o2.append(jnp.dot(p.astype(BF16), v2, preferred_element_type=F32) / den)
            outs.append(jnp.where(lane < MLA_V, o2[0], o2[1]))
        return jnp.concatenate(outs, axis=1)

    for i in range(n_ctx // ATT_Q_ROWS):
        r0 = i * ATT_Q_ROWS
        if ctx_queries:
            o_ref[pl.ds(r0, ATT_Q_ROWS), :] = attend_rows(r0, n_ctx).astype(o_ref.dtype)
        else:
            o_ref[pl.ds(r0, ATT_Q_ROWS), :] = jnp.zeros((ATT_Q_ROWS, W_MLA), o_ref.dtype)

    def q_tile(i, _):
        r0 = pl.multiple_of(n_ctx + i * ATT_Q_ROWS, ATT_Q_ROWS)
        o_ref[pl.ds(r0, ATT_Q_ROWS), :] = attend_rows(r0, n_all).astype(o_ref.dtype)
        return 0

    lax.fori_loop(0, (n_all - n_ctx) // ATT_Q_ROWS, q_tile, 0)


def mla_attention(ckv, qkp, rope_tab, params, n_ctx, ctx_queries):
    bsz, n_all, _ = ckv.shape
    assert n_all % ATT_PREP_ROWS == 0 and n_ctx % ATT_Q_ROWS == 0 and (n_all - n_ctx) % ATT_Q_ROWS == 0

    def full(arr):
        return pl.BlockSpec(arr.shape, lambda b: (0,) * arr.ndim)
    return pl.pallas_call(
        functools.partial(_mla_kernel, n_ctx=n_ctx, ctx_queries=ctx_queries),
        out_shape=jax.ShapeDtypeStruct((bsz, n_all, W_MLA), BF16),
        grid=(bsz,),
        in_specs=[pl.BlockSpec((None, n_all, MLA_KV_RANK), lambda b: (b, 0, 0)),
                  pl.BlockSpec((None, n_all, QKP_W), lambda b: (b, 0, 0)),
                  full(rope_tab)] + [full(p) for p in params],
        out_specs=pl.BlockSpec((None, n_all, W_MLA), lambda b: (b, 0, 0)),
        scratch_shapes=[pltpu.VMEM((MLA_HEADS, n_all, HEAD_W), BF16), pltpu.VMEM((MLA_HEADS, n_all, HEAD_W), BF16),
                        pltpu.VMEM((n_all, MLA_HEADS * MLA_V), BF16)],
        compiler_params=pltpu.CompilerParams(dimension_semantics=("arbitrary",)),
        name="mla_attention",
    )(ckv, qkp, rope_tab, *params)


SUBLANE = 8
RUN_ROWS = 1280
RUN_BITS = (256, 128, 64, 32, 16, 8)
PACK_W = D_MODEL // 2
RT_EXPERT, RT_WEIGHT, RT_SLOT = 0, TOP_K, 2 * TOP_K
META_SRC, META_CNT, META_BASE = 0, 1, 2
I32 = jnp.int32
U32 = jnp.uint32


def _mix_out_kernel(ya_ref, yp_ref, ys_ref, ym_ref, x_ref, g1_ref, sh_ref, sc_ref, ng_ref, wo_ref, rw_ref, rb_ref,
                    tri_ref, upper_ref, xo_ref, h_ref, rt_ref, meta_ref, tot_ref, carry_s):
    first = jnp.logical_and(pl.program_id(0) == 0, pl.program_id(1) == 0)

    @pl.when(first)
    def _():
        carry_s[...] = jnp.zeros_like(carry_s)

    acc = jnp.dot(ya_ref[...], wo_ref[0:W_CONV, :], preferred_element_type=F32)
    acc += jnp.dot(yp_ref[...], wo_ref[W_CONV:W_CONV + W_POOL, :], preferred_element_type=F32)
    acc += jnp.dot(ys_ref[...], wo_ref[W_CONV + W_POOL:D_MIX - W_MLA, :], preferred_element_type=F32)
    acc += jnp.dot(ym_ref[...], wo_ref[D_MIX - W_MLA:, :], preferred_element_type=F32)
    x = x_ref[...] + g1_ref[...] * acc
    xo_ref[...] = x
    y = x * lax.rsqrt(jnp.mean(x * x, axis=-1, keepdims=True) + EPS) * ng_ref[...]
    h = (y * (1.0 + sc_ref[...]) + sh_ref[...]).astype(BF16)
    h_ref[...] = h
    logits = jnp.dot(h, rw_ref[...], preferred_element_type=F32) + rb_ref[...]
    lane = lax.broadcasted_iota(I32, logits.shape, 1)
    route = jnp.zeros(logits.shape, F32)
    onehot = jnp.zeros(logits.shape, F32)
    vals, ids = [], []
    for k in range(TOP_K):
        m = jnp.max(logits, axis=-1, keepdims=True)
        idx = jnp.min(jnp.where(logits == m, lane, LANE), axis=-1, keepdims=True)
        hit = lane == idx
        logits = jnp.where(hit, -jnp.inf, logits)
        onehot = jnp.where(hit, 1.0, onehot)
        route = jnp.where(lane == RT_EXPERT + k, idx.astype(F32), route)
        vals.append(m)
        ids.append(idx)
    ex = [jnp.exp(v - vals[0]) for v in vals]
    den = ex[0] + ex[1] + ex[2] + ex[3]
    for k in range(TOP_K):
        route = jnp.where(lane == RT_WEIGHT + k, ex[k] / den, route)
    before = jnp.dot(tri_ref[...], onehot.astype(BF16), preferred_element_type=F32)
    cnt = jnp.sum(onehot, axis=0, keepdims=True)
    cnt8 = jnp.floor((cnt + (SUBLANE - 1)) * (1.0 / SUBLANE)) * SUBLANE
    cnt8_rows = jnp.broadcast_to(cnt8, (SUBLANE, LANE))
    src = jnp.dot(cnt8_rows.astype(BF16), upper_ref[...], preferred_element_type=F32)
    slot_of = before + src[0:1, :]
    for k in range(TOP_K):
        slot = jnp.sum(jnp.where(lane == ids[k], slot_of, 0.0), axis=-1, keepdims=True)
        route = jnp.where(lane == RT_SLOT + k, slot, route)
    rt_ref[...] = route
    base = carry_s[...]
    row = lax.broadcasted_iota(I32, (SUBLANE, LANE), 0)
    meta = jnp.where(row == META_SRC, src, jnp.where(row == META_CNT, cnt8_rows,
                                                     jnp.where(row == META_BASE, base, 0.0)))
    meta_ref[...] = meta.astype(I32)
    carry_s[...] = base + cnt8_rows
    tot_ref[...] = (base + cnt8_rows).astype(I32)


def mix_out(ya, yp, ys, ym, xc, mod3, norm_g, w_out, router_w, router_b, n_ctx, skip_ctx):
    bsz, n_all, dim = xc.shape
    nct = n_ctx // TOK_ROWS
    off = nct if skip_ctx else 0
    n_out = n_all - off * TOK_ROWS
    tiles = n_out // TOK_ROWS
    tile_in = lambda w: pl.BlockSpec((None, TOK_ROWS, w), lambda b, i: (b, i + off, 0))
    tile_out = lambda w: pl.BlockSpec((None, TOK_ROWS, w), lambda b, i: (b, i, 0))

    def mod_spec(chunk):
        return pl.BlockSpec((None, 1, dim), lambda b, i: (jnp.where(i + off < nct, bsz, b), 0, chunk))
    rw = jnp.zeros((dim, LANE), F32).at[:, :N_EXPERTS].set(router_w).astype(BF16)
    rb = jnp.full((1, LANE), -1e30, F32).at[0, :N_EXPERTS].set(router_b)
    r = jnp.arange(TOK_ROWS)
    tri = (r[None, :] < r[:, None]).astype(BF16)
    q = jnp.arange(LANE)
    upper = (q[:, None] < q[None, :]).astype(BF16)
    full = lambda a: pl.BlockSpec(a.shape, lambda b, i: (0,) * a.ndim)
    wo = w_out.astype(BF16)
    ng = norm_g.reshape(1, dim)
    return pl.pallas_call(
        _mix_out_kernel,
        out_shape=(jax.ShapeDtypeStruct((bsz, n_out, dim), F32), jax.ShapeDtypeStruct((bsz, n_out, dim), BF16),
                   jax.ShapeDtypeStruct((bsz, n_out, LANE), F32),
                   jax.ShapeDtypeStruct((bsz * tiles, SUBLANE, LANE), I32),
                   jax.ShapeDtypeStruct((SUBLANE, LANE), I32)),
        grid=(bsz, tiles),
        in_specs=[tile_in(W_CONV), tile_in(W_POOL), tile_in(W_SSM), tile_in(W_MLA), tile_in(dim),
                  mod_spec(2), mod_spec(3), mod_spec(4), full(ng), full(wo), full(rw), full(rb), full(tri),
                  full(upper)],
        out_specs=(tile_out(dim), tile_out(dim), tile_out(LANE),
                   pl.BlockSpec((None, SUBLANE, LANE), lambda b, i: (b * tiles + i, 0, 0)),
                   pl.BlockSpec((SUBLANE, LANE), lambda b, i: (0, 0))),
        scratch_shapes=[pltpu.VMEM((SUBLANE, LANE), F32)],
        compiler_params=pltpu.CompilerParams(dimension_semantics=("arbitrary", "arbitrary")),
        name="mix_out",
    )(ya, yp, ys, ym, xc, mod3, mod3, mod3, ng, wo, rw, rb, tri, upper)


def expert_layout(meta, tot, n_tok):
    n_tiles = meta.shape[0]
    tot8 = tot[0, :N_EXPERTS]
    region = (tot8 + MOE_BLOCK - 1) // MOE_BLOCK * MOE_BLOCK
    g_end = jnp.cumsum(region)
    g_start = g_end - region
    n_blocks = -(-(n_tok * TOP_K + n_tiles * N_EXPERTS * (SUBLANE - 1)) // MOE_BLOCK) + N_EXPERTS
    block_start = jnp.arange(n_blocks) * MOE_BLOCK
    block_e = jnp.minimum(jnp.sum(g_end[None, :] <= block_start[:, None], axis=1), N_EXPERTS - 1).astype(I32)
    n_used = (g_end[-1] // MOE_BLOCK).astype(I32).reshape(1)
    g_start_l = jnp.zeros((LANE,), I32).at[:N_EXPERTS].set(g_start.astype(I32))
    runs = jnp.stack([meta[:, META_SRC], meta[:, META_CNT], meta[:, META_BASE] + g_start_l], axis=1)
    tails = jnp.zeros((2, LANE), I32).at[0, :N_EXPERTS].set((g_start + tot8).astype(I32))
    tails = tails.at[1, :N_EXPERTS].set((region - tot8).astype(I32))
    return runs, tails, block_e, n_used, n_blocks


def _run_copies(runs_ref, local, remote, sem, to_remote, wait):
    def per_expert(e, c):
        s = runs_ref[0, 0, e]
        n = runs_ref[0, 1, e]
        d = runs_ref[0, 2, e]
        for bit in RUN_BITS:
            take = (n & bit) != 0
            loc = local.at[pl.ds(pl.multiple_of(s, SUBLANE), bit), :]
            rem = remote.at[pl.ds(pl.multiple_of(d, SUBLANE), bit), :]
            cp = pltpu.make_async_copy(loc, rem, sem) if to_remote else pltpu.make_async_copy(rem, loc, sem)

            @pl.when(take)
            def _():
                if wait:
                    cp.wait()
                else:
                    cp.start()
            s = s + jnp.where(take, bit, 0)
            d = d + jnp.where(take, bit, 0)
        return c
    lax.fori_loop(0, N_EXPERTS, per_expert, 0)


def _dispatch_kernel(runs_ref, runs_prev_ref, tails_ref, h_ref, slot_ref, xs_ref, buf, zbuf, sems):
    i = pl.program_id(0)
    cur = i % 2

    @pl.when(i == 0)
    def _():
        zbuf[...] = jnp.zeros_like(zbuf)

        def tail_copies(wait):
            def per_expert(e, c):
                d = tails_ref[0, e]
                n = tails_ref[1, e]
                for bit in RUN_BITS:
                    take = (n & bit) != 0
                    cp = pltpu.make_async_copy(zbuf.at[pl.ds(0, bit), :],
                                               xs_ref.at[pl.ds(pl.multiple_of(d, SUBLANE), bit), :], sems.at[2])

                    @pl.when(take)
                    def _():
                        if wait:
                            cp.wait()
                        else:
                            cp.start()
                    d = d + jnp.where(take, bit, 0)
                return c
            lax.fori_loop(0, N_EXPERTS, per_expert, 0)
        tail_copies(False)
        tail_copies(True)

    j = lax.broadcasted_iota(I32, (RUN_ROWS, TOK_ROWS), 0)
    hit = j == slot_ref[0:1, :]
    for k in range(1, TOP_K):
        hit = jnp.logical_or(hit, j == slot_ref[k:k + 1, :])
    p = jnp.where(hit, 1.0, 0.0).astype(BF16)
    lo = jnp.dot(p, h_ref[:, :PACK_W], preferred_element_type=F32)
    hi = jnp.dot(p, h_ref[:, PACK_W:], preferred_element_type=F32)
    lo_bits = lax.shift_right_logical(pltpu.bitcast(lo, U32), jnp.uint32(16))
    hi_bits = pltpu.bitcast(hi, U32) & jnp.uint32(0xFFFF0000)
    buf[cur] = hi_bits | lo_bits
    _run_copies(runs_ref, buf.at[cur], xs_ref, sems.at[cur], True, False)

    @pl.when(i > 0)
    def _():
        _run_copies(runs_prev_ref, buf.at[1 - cur], xs_ref, sems.at[1 - cur], True, True)

    @pl.when(i == pl.num_programs(0) - 1)
    def _():
        _run_copies(runs_ref, buf.at[cur], xs_ref, sems.at[cur], True, True)


def dispatch(h, slots_t, runs, tails, n_blocks):
    n_tok, dim = h.shape
    n_tiles = n_tok // TOK_ROWS
    return pl.pallas_call(
        _dispatch_kernel,
        out_shape=jax.ShapeDtypeStruct((n_blocks * MOE_BLOCK, PACK_W), U32),
        grid=(n_tiles,),
        in_specs=[pl.BlockSpec((1, 3, LANE), lambda i: (i, 0, 0), memory_space=pltpu.SMEM),
                  pl.BlockSpec((1, 3, LANE), lambda i: (jnp.maximum(i - 1, 0), 0, 0), memory_space=pltpu.SMEM),
                  pl.BlockSpec((2, LANE), lambda i: (0, 0), memory_space=pltpu.SMEM),
                  pl.BlockSpec((TOK_ROWS, dim), lambda i: (i, 0)),
                  pl.BlockSpec((None, SUBLANE, TOK_ROWS), lambda i: (i, 0, 0))],
        out_specs=pl.BlockSpec(memory_space=pl.ANY),
        scratch_shapes=[pltpu.VMEM((2, RUN_ROWS, PACK_W), U32), pltpu.VMEM((RUN_BITS[0], PACK_W), U32),
                        pltpu.SemaphoreType.DMA((3,))],
        compiler_params=pltpu.CompilerParams(dimension_semantics=("arbitrary",)),
        name="moe_dispatch",
    )(runs, runs, tails, h, slots_t)


def _expert_block_kernel(be_ref, nu_ref, x_ref, win_ref, bin_ref, wout_ref, bout_ref, o_ref):
    i = pl.program_id(0)

    @pl.when(i < nu_ref[0])
    def _():
        w = x_ref[...]
        x_lo = pltpu.bitcast(lax.shift_left(w, jnp.uint32(16)), F32).astype(BF16)
        x_hi = pltpu.bitcast(w & jnp.uint32(0xFFFF0000), F32).astype(BF16)
        gu = (jnp.dot(x_lo, win_ref[0, :PACK_W, :], preferred_element_type=F32)
              + jnp.dot(x_hi, win_ref[0, PACK_W:, :], preferred_element_type=F32) + bin_ref[0])
        gate = jnp.minimum(gu[:, :D_EXPERT], SWIGLU_LIMIT)
        up = jnp.clip(gu[:, D_EXPERT:], -SWIGLU_LIMIT, SWIGLU_LIMIT)
        act = (up + 1.0) * gate * jax.nn.sigmoid(SWIGLU_ALPHA * gate)
        o_ref[...] = jnp.dot(act.astype(BF16), wout_ref[0], preferred_element_type=F32) + bout_ref[0]

    @pl.when(i >= nu_ref[0])
    def _():
        o_ref[...] = jnp.zeros_like(o_ref)


def expert_blocks(xs, block_e, n_used, w_in, b_in, w_out, b_out):
    n_pad = xs.shape[0]
    dim = w_out.shape[-1]
    n_blocks = n_pad // MOE_BLOCK

    def row_map(i, be, nu):
        return (jnp.minimum(i, nu[0] - 1), 0)

    def exp_map3(i, be, nu):
        return (be[i], 0, 0)

    return pl.pallas_call(
        _expert_block_kernel,
        out_shape=jax.ShapeDtypeStruct((n_pad, dim), F32),
        grid_spec=pltpu.PrefetchScalarGridSpec(
            num_scalar_prefetch=2,
            grid=(n_blocks,),
            in_specs=[
                pl.BlockSpec((MOE_BLOCK, PACK_W), row_map),
                pl.BlockSpec((1, dim, 2 * D_EXPERT), exp_map3),
                pl.BlockSpec((1, 1, 2 * D_EXPERT), exp_map3),
                pl.BlockSpec((1, D_EXPERT, dim), exp_map3),
                pl.BlockSpec((1, 1, dim), exp_map3),
            ],
            out_specs=pl.BlockSpec((MOE_BLOCK, dim), lambda i, be, nu: (i, 0)),
        ),
        compiler_params=pltpu.CompilerParams(dimension_semantics=("arbitrary",)),
        name="expert_blocks",
    )(block_e, n_used, xs, w_in, b_in[:, None, :], w_out, b_out[:, None, :])


def _combine_kernel(runs_ref, runs_next_ref, ys_ref, rt_ref, x_ref, g_ref, o_ref, ybuf, sems):
    i = pl.program_id(0)
    cur = i % 2

    @pl.when(i == 0)
    def _():
        ybuf[...] = jnp.zeros_like(ybuf)
        _run_copies(runs_ref, ybuf.at[0], ys_ref, sems.at[0], False, False)

    @pl.when(i + 1 < pl.num_programs(0))
    def _():
        _run_copies(runs_next_ref, ybuf.at[1 - cur], ys_ref, sems.at[1 - cur], False, False)

    _run_copies(runs_ref, ybuf.at[cur], ys_ref, sems.at[cur], False, True)
    rt = rt_ref[...]
    j = lax.broadcasted_iota(I32, (TOK_ROWS, RUN_ROWS), 1).astype(F32)
    wmat = jnp.zeros((TOK_ROWS, RUN_ROWS), F32)
    for k in range(TOP_K):
        wmat = jnp.where(j == rt[:, RT_SLOT + k:RT_SLOT + k + 1], rt[:, RT_WEIGHT + k:RT_WEIGHT + k + 1], wmat)
    w_hi = wmat.astype(BF16)
    w_lo = (wmat - w_hi.astype(F32)).astype(BF16)
    y = ybuf[cur].astype(BF16)
    m = jnp.dot(w_hi, y, preferred_element_type=F32) + jnp.dot(w_lo, y, preferred_element_type=F32)
    o_ref[...] = x_ref[...] + g_ref[...] * m


def combine(ys, route, runs, xn, mod3, n_ctx_tiles):
    bsz, n_rows, dim = xn.shape
    tiles = n_rows // TOK_ROWS
    tile = lambda w: pl.BlockSpec((None, TOK_ROWS, w), lambda i: (i // tiles, i % tiles, 0))
    g_spec = pl.BlockSpec((None, 1, dim), lambda i: (jnp.where(i % tiles < n_ctx_tiles, bsz, i // tiles), 0, 5))
    return pl.pallas_call(
        _combine_kernel,
        out_shape=jax.ShapeDtypeStruct(xn.shape, F32),
        grid=(bsz * tiles,),
        in_specs=[pl.BlockSpec((1, 3, LANE), lambda i: (i, 0, 0), memory_space=pltpu.SMEM),
                  pl.BlockSpec((1, 3, LANE), lambda i: (jnp.minimum(i + 1, bsz * tiles - 1), 0, 0),
                               memory_space=pltpu.SMEM),
                  pl.BlockSpec(memory_space=pl.ANY), tile(LANE), tile(dim), g_spec],
        out_specs=tile(dim),
        scratch_shapes=[pltpu.VMEM((2, RUN_ROWS, dim), F32), pltpu.SemaphoreType.DMA((2,))],
        compiler_params=pltpu.CompilerParams(dimension_semantics=("arbitrary",)),
        name="moe_combine",
    )(runs, runs, ys, route, xn, mod3)


def routed_experts(xn, h2, route, meta, tot, mod3, n_ctx_tiles, w_in, b_in, w_out, b_out):
    bsz, n_rows, dim = xn.shape
    n_tok = bsz * n_rows
    runs, tails, block_e, n_used, n_blocks = expert_layout(meta, tot, n_tok)
    slots = route.reshape(n_tok // TOK_ROWS, TOK_ROWS, LANE)[:, :, RT_SLOT:RT_SLOT + SUBLANE]
    slots_t = jnp.swapaxes(slots, 1, 2).astype(I32)
    xs = dispatch(h2.reshape(n_tok, dim), slots_t, runs, tails, n_blocks)
    ys = expert_blocks(xs, block_e, n_used, w_in.astype(BF16), b_in, w_out.astype(BF16), b_out)
    return combine(ys, route, runs, xn, mod3, n_ctx_tiles)


def kernel(x, c, ctx, c_ctx, ada_w, ada_b, norm1_g, norm2_g, w_mix_in, w_mix_out,
           conv_dw, conv_dw_b, conv_ln_g, conv_ln_b, conv_pw, pool_w, pool_scale,
           ssm_a_re, ssm_a_im, ssm_log_dt, ssm_b_re, ssm_b_im, ssm_c_re, ssm_c_im,
           ssm_d, ssm_glu_w, ssm_glu_b,
           mla_q_a_g, mla_wq_b, mla_kv_a_g, mla_wkv_b, mla_q_g, mla_k_g,
           router_w, router_b, exp_w_in, exp_b_in, exp_w_out, exp_b_out):
    bsz, n_lat, dim = x.shape
    n_ctx = ctx.shape[1]
    xc = jnp.concatenate([ctx, x], axis=1)
    c_all = jnp.zeros((MOD_ROWS, dim), F32).at[:bsz].set(c).at[bsz].set(c_ctx)
    rope_tab = rope_lane_tables(n_ctx, n_lat)
    for l in range(DEPTH):
        last = l == DEPTH - 1
        mod = modulation(c_all, ada_w[l], ada_b[l])
        mod3 = mod.reshape(MOD_ROWS, 1, 6 * dim)

        conv_in, pool_in, ssm_in, ckv, qkp = mix_in(xc, mod3, norm1_g[l], mix_in_weights(w_mix_in[l]), n_ctx)
        ya, yp = local_mixers(conv_in, pool_in, n_ctx, conv_dw[l], conv_dw_b[l], conv_ln_g[l], conv_ln_b[l],
                              conv_pw[l], pool_w[l], pool_scale[l])
        ys = s5_mixer_pallas(ssm_in, n_ctx, ssm_a_re[l], ssm_a_im[l], ssm_log_dt[l],
                             ssm_b_re[l], ssm_b_im[l], ssm_c_re[l], ssm_c_im[l],
                             ssm_d[l], ssm_glu_w[l], ssm_glu_b[l])
        ym = mla_attention(ckv, qkp, rope_tab,
                           mla_layout_params(mla_q_a_g[l], mla_wq_b[l], mla_kv_a_g[l], mla_wkv_b[l],
                                             mla_q_g[l], mla_k_g[l]), n_ctx, not last)
        xn, h2, route, meta, tot = mix_out(ya, yp, ys, ym, xc, mod3, norm2_g[l], w_mix_out[l],
                                           router_w[l], router_b[l], n_ctx, last)
        xc = routed_experts(xn, h2, route, meta, tot, mod3, 0 if last else n_ctx // TOK_ROWS,
                            exp_w_in[l], exp_b_in[l], exp_w_out[l], exp_b_out[l])
    return xc
```

```python
import functools

import jax
import jax.numpy as jnp
import numpy as np
from jax import lax
from jax.experimental import pallas as pl
from jax.experimental.pallas import tpu as pltpu

D_MODEL = 1024
DEPTH = 2
GRID_W = 64
EPS = 1e-6

D_MIX = D_MODEL
W_CONV = D_MIX // 4
W_POOL = D_MIX // 4
W_SSM = D_MIX // 4
W_MLA = D_MIX - W_CONV - W_POOL - W_SSM
CONV_K = 31
POOL_WINDOWS = (2, 4, 8, 16)
POOL_CH = W_POOL // len(POOL_WINDOWS)
SSM_CH = 16
SSM_GROUPS = W_SSM // SSM_CH
SSM_STATE = 64
MLA_V = 64
MLA_HEADS = W_MLA // MLA_V
MLA_NOPE = 64
MLA_ROPE = 32
MLA_QK = MLA_NOPE + MLA_ROPE
MLA_Q_RANK = 192
MLA_KV_RANK = 128
ROPE_AXIS = MLA_ROPE // 2
ROPE_BASE = 10000.0
N_EXPERTS = 32
TOP_K = 4
D_EXPERT = D_MODEL
SWIGLU_LIMIT = 7.0
SWIGLU_ALPHA = 1.702
MOE_BLOCK = 512

OFF_POOL = 2 * W_CONV
OFF_Q = OFF_POOL + W_POOL
OFF_SSM = OFF_Q + MLA_Q_RANK
OFF_KV = OFF_SSM + W_SSM
OFF_KPE = OFF_KV + MLA_KV_RANK
N_IN = OFF_KPE + MLA_ROPE

F32 = jnp.float32
BF16 = jnp.bfloat16
LANE = 128
QKP_W = 256


MOD_ROWS = 40
MOD_COLS = 1024


def _mod_kernel(c_ref, w_ref, b_ref, o_ref):
    c = c_ref[...]
    s = (c * jax.nn.sigmoid(c)).astype(BF16)
    o_ref[...] = jnp.dot(s, w_ref[...].astype(BF16), preferred_element_type=F32) + b_ref[...]


def modulation(c_all, ada_w, ada_b):
    n_out = ada_w.shape[1]
    return pl.pallas_call(
        _mod_kernel,
        out_shape=jax.ShapeDtypeStruct((MOD_ROWS, n_out), F32),
        grid=(n_out // MOD_COLS,),
        in_specs=[pl.BlockSpec((MOD_ROWS, D_MODEL), lambda j: (0, 0)),
                  pl.BlockSpec((D_MODEL, MOD_COLS), lambda j: (0, j)),
                  pl.BlockSpec((1, MOD_COLS), lambda j: (0, j))],
        out_specs=pl.BlockSpec((MOD_ROWS, MOD_COLS), lambda j: (0, j)),
        compiler_params=pltpu.CompilerParams(dimension_semantics=("arbitrary",)),
        name="modulation",
    )(c_all, ada_w, ada_b.reshape(1, n_out))


TOK_ROWS = 256
IN_SPLITS = (2 * W_CONV, W_POOL, W_SSM, MLA_KV_RANK, QKP_W)


def _mix_in_kernel(x_ref, sh_ref, sc_ref, g_ref, w_ref, *outs):
    x = x_ref[...]
    y = x * lax.rsqrt(jnp.mean(x * x, axis=-1, keepdims=True) + EPS) * g_ref[...]
    h = y * (1.0 + sc_ref[...]) + sh_ref[...]
    p = jnp.dot(h.astype(BF16), w_ref[...], preferred_element_type=F32)
    off = 0
    for o_ref, w in zip(outs, IN_SPLITS):
        o_ref[...] = p[:, off:off + w]
        off += w


def mix_in_weights(w_mix_in):
    cols = [w_mix_in[:, :OFF_POOL], w_mix_in[:, OFF_POOL:OFF_Q], w_mix_in[:, OFF_SSM:OFF_KV],
            w_mix_in[:, OFF_KV:OFF_KPE], w_mix_in[:, OFF_Q:OFF_SSM], w_mix_in[:, OFF_KPE:],
            jnp.zeros((D_MODEL, QKP_W - MLA_Q_RANK - MLA_ROPE), w_mix_in.dtype)]
    return jnp.concatenate(cols, axis=1).astype(BF16)


def mix_in(xc, mod3, norm_g, w_in_p, n_ctx):
    bsz, n_all, dim = xc.shape
    assert n_ctx % TOK_ROWS == 0 and n_all % TOK_ROWS == 0 and bsz < MOD_ROWS
    nct = n_ctx // TOK_ROWS
    tile = lambda w: pl.BlockSpec((None, TOK_ROWS, w), lambda b, i: (b, i, 0))

    def mod_spec(chunk):
        return pl.BlockSpec((None, 1, dim), lambda b, i: (jnp.where(i < nct, bsz, b), 0, chunk))
    return pl.pallas_call(
        _mix_in_kernel,
        out_shape=tuple(jax.ShapeDtypeStruct((bsz, n_all, w), F32) for w in IN_SPLITS),
        grid=(bsz, n_all // TOK_ROWS),
        in_specs=[tile(dim), mod_spec(0), mod_spec(1),
                  pl.BlockSpec((1, dim), lambda b, i: (0, 0)),
                  pl.BlockSpec(w_in_p.shape, lambda b, i: (0, 0))],
        out_specs=tuple(tile(w) for w in IN_SPLITS),
        compiler_params=pltpu.CompilerParams(dimension_semantics=("arbitrary", "arbitrary")),
        name="mix_in",
    )(xc, mod3, mod3, norm_g.reshape(1, dim), w_in_p)


LM_PAD = 16
LM_ROWS = 64
LM_DBL_ROWS = 32


def _local_mixers_kernel(cin_ref, pin_ref, dw_ref, dwb_ref, lng_ref, lnb_ref, pw_ref, pwb_ref, psc_ref,
                         ya_ref, yp_ref, upad, ppad, s2, s4, s8, *, n_ctx):
    n_all = cin_ref.shape[0]
    n_lat = n_all - n_ctx
    n_pad = upad.shape[0]
    lat0 = n_ctx + 2 * LM_PAD
    zeros = jnp.zeros((LM_PAD, W_CONV), F32)
    for buf in (upad, ppad):
        buf[0:LM_PAD, :] = zeros
        buf[LM_PAD + n_ctx:lat0, :] = zeros
        buf[lat0 + n_lat:n_pad, :] = zeros

    def pad_row(i):
        return pl.multiple_of(i * LM_ROWS + jnp.where(i < n_ctx // LM_ROWS, LM_PAD, 2 * LM_PAD), 8)

    def fill(i, _):
        r0 = pl.multiple_of(i * LM_ROWS, LM_ROWS)
        c = cin_ref[pl.ds(r0, LM_ROWS), :]
        dst = pl.ds(pad_row(i), LM_ROWS)
        upad[dst, :] = c[:, :W_CONV] * jax.nn.sigmoid(c[:, W_CONV:])
        ppad[dst, :] = pin_ref[pl.ds(r0, LM_ROWS), :]
        return 0
    lax.fori_loop(0, n_all // LM_ROWS, fill, 0)

    def doubling(src, dst, back, fwd):
        dst[0:8, :] = zeros[:8]
        dst[n_pad - 8:n_pad, :] = zeros[:8]

        def body(i, _):
            r0 = pl.multiple_of(8 + i * LM_DBL_ROWS, 8)
            h = src[pl.ds(r0 - 8, LM_DBL_ROWS + 16), :]
            dst[pl.ds(r0, LM_DBL_ROWS), :] = (h[8 - back:8 - back + LM_DBL_ROWS, :]
                                              + h[8 + fwd:8 + fwd + LM_DBL_ROWS, :])
            return 0
        lax.fori_loop(0, (n_pad - 16) // LM_DBL_ROWS, body, 0)
    doubling(ppad, s2, 1, 0)
    doubling(s2, s4, 1, 1)
    doubling(s4, s8, 2, 2)

    lane = lax.broadcasted_iota(jnp.int32, (LM_ROWS, W_POOL), 1)
    grp = lane // POOL_CH
    back = jnp.where(grp == 0, 1, jnp.where(grp == 1, 2, jnp.where(grp == 2, 4, 8)))
    fwd = back - 1

    def tile(i, _):
        r0 = pl.multiple_of(i * LM_ROWS, LM_ROWS)
        p0 = pad_row(i)
        halo = upad[pl.ds(p0 - LM_PAD, LM_ROWS + 2 * LM_PAD), :]
        acc = jnp.zeros((LM_ROWS, W_CONV), F32)
        for k in range(CONV_K):
            o = LM_PAD + k - CONV_K // 2
            acc = acc + halo[o:o + LM_ROWS, :] * dw_ref[k:k + 1, :]
        acc = acc + dwb_ref[...]
        mu = jnp.mean(acc, axis=-1, keepdims=True)
        var = jnp.mean(jnp.square(acc - mu), axis=-1, keepdims=True)
        v = (acc - mu) * lax.rsqrt(var + EPS) * lng_ref[...] + lnb_ref[...]
        v = v * jax.nn.sigmoid(v)
        ya_ref[pl.ds(r0, LM_ROWS), :] = jnp.dot(v.astype(BF16), pw_ref[...],
                                                preferred_element_type=F32).astype(ya_ref.dtype)
        rows = pl.ds(p0, LM_ROWS)
        h8 = s8[pl.ds(p0 - 8, LM_ROWS + 16), :]
        s16 = h8[4:4 + LM_ROWS, :] + h8[12:12 + LM_ROWS, :]
        wsum = jnp.where(grp == 0, s2[rows, :], jnp.where(grp == 1, s4[rows, :],
                                                          jnp.where(grp == 2, s8[rows, :], s16)))
        seg_len = jnp.where(i < n_ctx // LM_ROWS, n_ctx, n_lat)
        t = (r0 - jnp.where(i < n_ctx // LM_ROWS, 0, n_ctx)) + lax.broadcasted_iota(jnp.int32, (LM_ROWS, W_POOL), 0)
        cnt = jnp.minimum(t + fwd, seg_len - 1) - jnp.maximum(t - back, 0) + 1
        pooled = wsum / cnt.astype(F32) - ppad[rows, :]
        y = jnp.dot(pooled.astype(BF16), pwb_ref[...], preferred_element_type=F32) * psc_ref[...]
        yp_ref[pl.ds(r0, LM_ROWS), :] = y.astype(yp_ref.dtype)
        return 0
    lax.fori_loop(0, n_all // LM_ROWS, tile, 0)


def local_mixers(conv_in, pool_in, n_ctx, conv_dw, conv_dw_b, ln_g, ln_b, conv_pw, pool_w, pool_scale):
    bsz, n_all, _ = conv_in.shape
    assert n_ctx % LM_ROWS == 0 and n_all % LM_ROWS == 0
    n_pad = n_all + 3 * LM_PAD
    assert (n_pad - 16) % LM_DBL_ROWS == 0
    dw = jnp.zeros((CONV_K + 1, W_CONV), F32).at[:CONV_K].set(conv_dw)
    eye = jnp.eye(len(POOL_WINDOWS), dtype=F32)
    pwb = jnp.einsum('gcd,gh->gchd', pool_w, eye).reshape(W_POOL, W_POOL).astype(BF16)
    row = lambda v: v.reshape(1, -1).astype(F32)
    params = [dw, row(conv_dw_b), row(ln_g), row(ln_b), conv_pw.astype(BF16), pwb, row(pool_scale)]

    def full(arr):
        return pl.BlockSpec(arr.shape, lambda b: (0,) * arr.ndim)
    seq = lambda w: pl.BlockSpec((None, n_all, w), lambda b: (b, 0, 0))
    out = jax.ShapeDtypeStruct((bsz, n_all, W_CONV), BF16)
    return pl.pallas_call(
        functools.partial(_local_mixers_kernel, n_ctx=n_ctx),
        out_shape=(out, out),
        grid=(bsz,),
        in_specs=[seq(2 * W_CONV), seq(W_POOL)] + [full(p) for p in params],
        out_specs=(seq(W_CONV), seq(W_POOL)),
        scratch_shapes=[pltpu.VMEM((n_pad, W_CONV), F32) for _ in range(5)],
        compiler_params=pltpu.CompilerParams(dimension_semantics=("arbitrary",)),
        name="local_mixers",
    )(conv_in, pool_in, *params)


S5_NS = SSM_GROUPS * SSM_STATE
S5_TC = 128
S5_BG = 8
S5_LANES = 512


def _s5_scan_kernel(u_ref, perm_ref, a_ref, bd_ref, cd_ref, *rest, reverse, final):
    if final:
        yp_ref, permt_ref, d_ref, gw_ref, gb_ref, y_ref, bu_s, hh_s, h_s = rest
    else:
        y_ref, bu_s, hh_s, h_s = rest
    tc = u_ref.shape[1]
    rows = S5_BG * tc
    j = pl.program_id(1)

    @pl.when(j == 0)
    def _():
        h_s[...] = jnp.zeros_like(h_s)

    u = u_ref[...].reshape(rows, W_SSM)
    u_hi = u.astype(BF16)
    if final:
        u_lo = (u - u_hi.astype(F32)).astype(BF16)
        u_tm = jnp.dot(perm_ref[...], jnp.concatenate([u_hi, u_lo], axis=1), preferred_element_type=F32)
        u_hi_tm = u_tm[:, :W_SSM].astype(BF16)
        u_f32_tm = u_tm[:, :W_SSM] + u_tm[:, W_SSM:]
    else:
        u_hi_tm = jnp.dot(perm_ref[...], u_hi, preferred_element_type=F32).astype(BF16)
    bu_s[...] = jnp.dot(u_hi_tm, bd_ref[...], preferred_element_type=F32)

    for hf in range(S5_NS // S5_LANES):
        re_cols = pl.ds(hf * S5_LANES, S5_LANES)
        im_cols = pl.ds(S5_NS + hf * S5_LANES, S5_LANES)
        a_re = jnp.broadcast_to(a_ref[0:1, hf * S5_LANES:(hf + 1) * S5_LANES], (S5_BG, S5_LANES))
        a_im = jnp.broadcast_to(a_ref[1:2, hf * S5_LANES:(hf + 1) * S5_LANES], (S5_BG, S5_LANES))

        def step(i, carry):
            h_re, h_im = carry
            t = (tc - 1 - i) if reverse else i
            r8 = pl.ds(pl.multiple_of(t * S5_BG, S5_BG), S5_BG)
            n_re = a_re * h_re - a_im * h_im + bu_s[r8, re_cols]
            n_im = a_re * h_im + a_im * h_re + bu_s[r8, im_cols]
            hh_s[r8, re_cols] = n_re
            hh_s[r8, im_cols] = n_im
            return n_re, n_im

        h_re, h_im = lax.fori_loop(0, tc, step, (h_s[:, re_cols], h_s[:, im_cols]), unroll=8)
        h_s[:, re_cols] = h_re
        h_s[:, im_cols] = h_im

    y = jnp.dot(hh_s[...].astype(BF16), cd_ref[...], preferred_element_type=F32)
    if final:
        y = y + yp_ref[...] + d_ref[...] * u_f32_tm
        z = jax.nn.gelu(y)
        y = z * jax.nn.sigmoid(jnp.dot(z.astype(BF16), gw_ref[...], preferred_element_type=F32) + gb_ref[...])
        y = jnp.dot(permt_ref[...], y.astype(BF16), preferred_element_type=F32)
        y_ref[...] = y.astype(BF16).reshape(S5_BG, tc, W_SSM)
    else:
        y_ref[...] = y


def s5_direction(u_all, perm, a, bd, cd, n_ctx, reverse, final_args=None):
    bsz, n_all, _ = u_all.shape
    tc = S5_TC
    n_chunks = n_all // tc
    ctx_chunks = n_ctx // tc
    rows = S5_BG * tc
    assert n_all % tc == 0 and n_ctx % tc == 0 and bsz % S5_BG == 0

    if reverse:
        def chunk(j):
            return jnp.where(j < ctx_chunks, ctx_chunks - 1 - j, n_chunks - 1 + ctx_chunks - j)
    else:
        def chunk(j):
            return j
    seq_spec = pl.BlockSpec((S5_BG, tc, W_SSM), lambda g, j: (g, chunk(j), 0))
    part_spec = pl.BlockSpec((None, None, rows, W_SSM), lambda g, j: (g, chunk(j), 0, 0))

    def full(arr):
        return pl.BlockSpec(arr.shape, lambda g, j: (0,) * arr.ndim)
    in_specs = [seq_spec, full(perm), full(a), full(bd), full(cd)]
    args = [u_all, perm, a, bd, cd]
    if final_args is not None:
        y_prev, d, glu_w, glu_b = final_args
        perm_t = perm.T
        in_specs += [part_spec, full(perm_t), full(d), full(glu_w), full(glu_b)]
        args += [y_prev, perm_t, d, glu_w, glu_b]
        out_shape = jax.ShapeDtypeStruct(u_all.shape, BF16)
        out_spec = seq_spec
    else:
        out_shape = jax.ShapeDtypeStruct((bsz // S5_BG, n_chunks, rows, W_SSM), F32)
        out_spec = part_spec
    return pl.pallas_call(
        functools.partial(_s5_scan_kernel, reverse=reverse, final=final_args is not None),
        out_shape=out_shape,
        grid=(bsz // S5_BG, n_chunks),
        in_specs=in_specs,
        out_specs=out_spec,
        scratch_shapes=[pltpu.VMEM((rows, 2 * S5_NS), F32), pltpu.VMEM((rows, 2 * S5_NS), F32),
                        pltpu.VMEM((S5_BG, 2 * S5_NS), F32)],
        compiler_params=pltpu.CompilerParams(dimension_semantics=("arbitrary", "arbitrary")),
        name="s5_scan_bwd" if reverse else "s5_scan_fwd",
    )(*args)


def s5_time_major_perm(tc):
    r = jnp.arange(S5_BG * tc)
    src = (r % S5_BG) * tc + r // S5_BG
    return (src[:, None] == r[None, :]).astype(BF16)


def s5_params(a_re, a_im, log_dt, b_re, b_im, c_re, c_im):
    a_re = jnp.minimum(a_re.astype(F32), -1e-4)
    a_im = a_im.astype(F32)
    dt = jnp.exp(log_dt.astype(F32))[:, None]
    mag = jnp.exp(a_re * dt)
    ab_re = mag * jnp.cos(a_im * dt)
    ab_im = mag * jnp.sin(a_im * dt)
    den = a_re * a_re + a_im * a_im
    f_re = ((ab_re - 1.0) * a_re + ab_im * a_im) / den
    f_im = (ab_im * a_re - (ab_re - 1.0) * a_im) / den
    bb_re = f_re[..., None] * b_re - f_im[..., None] * b_im
    bb_im = f_re[..., None] * b_im + f_im[..., None] * b_re
    eye = jnp.eye(SSM_GROUPS, dtype=F32)

    def in_map(bb):
        return jnp.einsum('gpc,gh->gchp', bb, eye).reshape(W_SSM, S5_NS)

    def out_map(cc):
        return jnp.einsum('gcp,gh->gphc', cc, eye).reshape(S5_NS, W_SSM)
    a = jnp.stack([ab_re.reshape(-1), ab_im.reshape(-1)])
    bd = jnp.concatenate([in_map(bb_re), in_map(bb_im)], axis=1).astype(BF16)
    cd = jnp.concatenate([out_map(c_re.astype(F32)), -out_map(c_im.astype(F32))], axis=0).astype(BF16)
    return a, bd, cd


def s5_mixer_pallas(u_all, n_ctx, a_re, a_im, log_dt, b_re, b_im, c_re, c_im, d, glu_w, glu_b):
    pf = s5_params(a_re[0], a_im[0], log_dt[0], b_re[0], b_im[0], c_re[0], c_im[0])
    pb = s5_params(a_re[1], a_im[1], log_dt[1], b_re[1], b_im[1], c_re[1], c_im[1])
    perm = s5_time_major_perm(S5_TC)
    y_f = s5_direction(u_all, perm, *pf, n_ctx, False)
    return s5_direction(u_all, perm, *pb, n_ctx, True,
                        (y_f, d.reshape(1, W_SSM).astype(F32), glu_w.astype(BF16), glu_b.reshape(1, W_SSM)))


HEAD_W = LANE
ATT_PREP_ROWS = 256
ATT_Q_ROWS = 256


def _head_dim_order():
    half = ROPE_AXIS // 2
    r = MLA_NOPE
    first = list(range(r, r + half)) + list(range(r + 2 * half, r + 3 * half))
    second = list(range(r + half, r + 2 * half)) + list(range(r + 3 * half, r + 4 * half))
    split = LANE // 2 - MLA_ROPE // 2
    return np.array(first + list(range(split)) + second + list(range(split, MLA_NOPE)))


def _to_head_lanes(w):
    order = _head_dim_order()
    pad = [(0, 0)] * (w.ndim - 1) + [(0, HEAD_W - len(order))]
    return jnp.pad(jnp.take(w, order, axis=-1), pad)


def mla_layout_params(q_a_g, wq_b, kv_a_g, wkv_b, q_g, k_g):
    wq = _to_head_lanes(wq_b.reshape(MLA_Q_RANK, MLA_HEADS, MLA_QK))
    wq_p = jnp.pad(wq, ((0, QKP_W - MLA_Q_RANK), (0, 0), (0, 0))).reshape(QKP_W, MLA_HEADS * HEAD_W).astype(BF16)
    qa_g = jnp.pad(q_a_g, (0, QKP_W - MLA_Q_RANK)).reshape(1, QKP_W)
    wkv = wkv_b.reshape(MLA_KV_RANK, MLA_HEADS, MLA_NOPE + MLA_V)
    wk = jnp.pad(wkv[:, :, :MLA_NOPE], ((0, 0), (0, 0), (0, MLA_ROPE)))
    wkv_p = jnp.concatenate([_to_head_lanes(wk).reshape(MLA_KV_RANK, -1),
                             wkv[:, :, MLA_NOPE:].reshape(MLA_KV_RANK, -1)], axis=1).astype(BF16)
    order = _head_dim_order()
    place = np.zeros((QKP_W, MLA_HEADS, HEAD_W), np.float32)
    for lane_i, dim in enumerate(order):
        if dim >= MLA_NOPE:
            place[MLA_Q_RANK + dim - MLA_NOPE, :, lane_i] = 1.0
    place = jnp.asarray(place.reshape(QKP_W, MLA_HEADS * HEAD_W), BF16)
    return (wq_p, qa_g, wkv_p, kv_a_g.reshape(1, MLA_KV_RANK), place,
            _to_head_lanes(q_g.reshape(1, MLA_QK)), _to_head_lanes(k_g.reshape(1, MLA_QK)))


def rope_lane_tables(n_ctx, n_lat):
    t = jnp.arange(n_lat)
    row = (t // GRID_W).astype(F32)
    col = (t % GRID_W).astype(F32)
    inv = ROPE_BASE ** (-jnp.arange(0, ROPE_AXIS, 2, dtype=F32) / ROPE_AXIS)
    ang = jnp.concatenate([row[:, None] * inv, col[:, None] * inv], axis=1)
    cos, sin = jnp.cos(ang), jnp.sin(ang)
    half = MLA_ROPE // 2
    one = jnp.ones((n_lat, LANE // 2 - half), F32)
    zero = jnp.zeros((n_lat, LANE // 2 - half), F32)
    cmul = jnp.concatenate([cos, one, cos, one], axis=1)
    smul = jnp.concatenate([-sin, zero, sin, zero], axis=1)
    ctx = [jnp.ones((n_ctx, HEAD_W), F32), jnp.zeros((n_ctx, HEAD_W), F32)]
    return jnp.stack([jnp.concatenate([c, x], axis=0) for c, x in zip(ctx, (cmul, smul))])


def _mla_kernel(ckv_ref, qkp_ref, rope_ref, wq_ref, qag_ref, wkv_ref, kvag_ref, place_ref, qg_ref, kg_ref,
                o_ref, q_s, k_s, v_s, *, n_ctx, ctx_queries):
    n_all = ckv_ref.shape[0]

    def rope(x, r0):
        rows = pl.ds(r0, ATT_PREP_ROWS)
        swapped = pltpu.roll(x, HEAD_W // 2, axis=1)
        return x * rope_ref[0, rows, :] + swapped * rope_ref[1, rows, :]

    def head_norm(x, g):
        ss = jnp.sum(x * x, axis=-1, keepdims=True) * (1.0 / MLA_QK)
        return x * lax.rsqrt(ss + EPS) * g

    def prep(i, _):
        r0 = pl.multiple_of(i * ATT_PREP_ROWS, ATT_PREP_ROWS)
        rows = pl.ds(r0, ATT_PREP_ROWS)
        ckv = ckv_ref[rows, :]
        kvn = ckv * lax.rsqrt(jnp.mean(ckv * ckv, axis=-1, keepdims=True) + EPS) * kvag_ref[...]
        kv = jnp.dot(kvn.astype(BF16), wkv_ref[...], preferred_element_type=F32)
        qkp = qkp_ref[rows, :]
        p_hi = qkp.astype(BF16)
        p_lo = (qkp - p_hi.astype(F32)).astype(BF16)
        kpe = (jnp.dot(p_hi, place_ref[...], preferred_element_type=F32)
               + jnp.dot(p_lo, place_ref[...], preferred_element_type=F32))
        lane = lax.broadcasted_iota(jnp.int32, qkp.shape, 1)
        qsq = jnp.where(lane < MLA_Q_RANK, qkp * qkp, 0.0)
        qn = qkp * lax.rsqrt(jnp.sum(qsq, axis=-1, keepdims=True) * (1.0 / MLA_Q_RANK) + EPS) * qag_ref[...]
        q = jnp.dot(qn.astype(BF16), wq_ref[...], preferred_element_type=F32)
        for h in range(MLA_HEADS):
            blk = slice(h * HEAD_W, (h + 1) * HEAD_W)
            k_h = rope(head_norm(kv[:, blk] + kpe[:, blk], kg_ref[...]), r0)
            q_h = rope(head_norm(q[:, blk], qg_ref[...]), r0) * (MLA_QK ** -0.5)
            k_s[h, rows, :] = k_h.astype(BF16)
            q_s[h, rows, :] = q_h.astype(BF16)
        v_s[rows, :] = kv[:, MLA_HEADS * HEAD_W:].astype(BF16)
        return 0

    lax.fori_loop(0, n_all // ATT_PREP_ROWS, prep, 0)

    lane = lax.broadcasted_iota(jnp.int32, (ATT_Q_ROWS, LANE), 1)

    def attend_rows(r0, n_keys):
        outs = []
        for pair in range(MLA_HEADS // 2):
            v2 = v_s[pl.ds(0, n_keys), pair * LANE:(pair + 1) * LANE]
            o2 = []
            for h in (2 * pair, 2 * pair + 1):
                q = q_s[h, pl.ds(r0, ATT_Q_ROWS), :]
                k = k_s[h, pl.ds(0, n_keys), :]
                s = lax.dot_general(q, k, (((1,), (1,)), ((), ())), preferred_element_type=F32)
                p = jnp.exp(s - jnp.max(s, axis=-1, keepdims=True))
                den = jnp.sum(p, axis=-1, keepdims=True)
                o2.append(jnp.dot(p.astype(BF16), v2, preferred_element_type=F32) / den)
            outs.append(jnp.where(lane < MLA_V, o2[0], o2[1]))
        return jnp.concatenate(outs, axis=1)

    for i in range(n_ctx // ATT_Q_ROWS):
        r0 = i * ATT_Q_ROWS
        if ctx_queries:
            o_ref[pl.ds(r0, ATT_Q_ROWS), :] = attend_rows(r0, n_ctx).astype(o_ref.dtype)
        else:
            o_ref[pl.ds(r0, ATT_Q_ROWS), :] = jnp.zeros((ATT_Q_ROWS, W_MLA), o_ref.dtype)

    def q_tile(i, _):
        r0 = pl.multiple_of(n_ctx + i * ATT_Q_ROWS, ATT_Q_ROWS)
        o_ref[pl.ds(r0, ATT_Q_ROWS), :] = attend_rows(r0, n_all).astype(o_ref.dtype)
        return 0

    lax.fori_loop(0, (n_all - n_ctx) // ATT_Q_ROWS, q_tile, 0)


def mla_attention(ckv, qkp, rope_tab, params, n_ctx, ctx_queries):
    bsz, n_all, _ = ckv.shape
    assert n_all % ATT_PREP_ROWS == 0 and n_ctx % ATT_Q_ROWS == 0 and (n_all - n_ctx) % ATT_Q_ROWS == 0

    def full(arr):
        return pl.BlockSpec(arr.shape, lambda b: (0,) * arr.ndim)
    return pl.pallas_call(
        functools.partial(_mla_kernel, n_ctx=n_ctx, ctx_queries=ctx_queries),
        out_shape=jax.ShapeDtypeStruct((bsz, n_all, W_MLA), BF16),
        grid=(bsz,),
        in_specs=[pl.BlockSpec((None, n_all, MLA_KV_RANK), lambda b: (b, 0, 0)),
                  pl.BlockSpec((None, n_all, QKP_W), lambda b: (b, 0, 0)),
                  full(rope_tab)] + [full(p) for p in params],
        out_specs=pl.BlockSpec((None, n_all, W_MLA), lambda b: (b, 0, 0)),
        scratch_shapes=[pltpu.VMEM((MLA_HEADS, n_all, HEAD_W), BF16), pltpu.VMEM((MLA_HEADS, n_all, HEAD_W), BF16),
                        pltpu.VMEM((n_all, MLA_HEADS * MLA_V), BF16)],
        compiler_params=pltpu.CompilerParams(dimension_semantics=("arbitrary",)),
        name="mla_attention",
    )(ckv, qkp, rope_tab, *params)


SUBLANE = 8
RUN_ROWS = 1280
RUN_BITS = (256, 128, 64, 32, 16, 8)
PACK_W = D_MODEL // 2
RT_EXPERT, RT_WEIGHT, RT_SLOT = 0, TOP_K, 2 * TOP_K
RT_ROWS = 16
META_SRC, META_CNT, META_BASE = 0, 1, 2
I32 = jnp.int32
U32 = jnp.uint32


def _mix_out_kernel(ya_ref, yp_ref, ys_ref, ym_ref, x_ref, g1_ref, sh_ref, sc_ref, ng_ref, wo_ref, rwt_ref, rbt_ref,
                    tri_ref, upper_ref, lower_ref, xo_ref, h_ref, rt_ref, meta_ref, tot_ref, carry_s):
    first = jnp.logical_and(pl.program_id(0) == 0, pl.program_id(1) == 0)

    @pl.when(first)
    def _():
        carry_s[...] = jnp.zeros_like(carry_s)

    acc = jnp.dot(ya_ref[...], wo_ref[0:W_CONV, :], preferred_element_type=F32)
    acc += jnp.dot(yp_ref[...], wo_ref[W_CONV:W_CONV + W_POOL, :], preferred_element_type=F32)
    acc += jnp.dot(ys_ref[...], wo_ref[W_CONV + W_POOL:D_MIX - W_MLA, :], preferred_element_type=F32)
    acc += jnp.dot(ym_ref[...], wo_ref[D_MIX - W_MLA:, :], preferred_element_type=F32)
    x = x_ref[...] + g1_ref[...] * acc
    xo_ref[...] = x
    y = x * lax.rsqrt(jnp.mean(x * x, axis=-1, keepdims=True) + EPS) * ng_ref[...]
    h = (y * (1.0 + sc_ref[...]) + sh_ref[...]).astype(BF16)
    h_ref[...] = h
    logits = lax.dot_general(rwt_ref[...], h, (((1,), (1,)), ((), ())), preferred_element_type=F32) + rbt_ref[...]
    eid = lax.broadcasted_iota(I32, logits.shape, 0)
    onehot = jnp.zeros(logits.shape, F32)
    vals, hits, rows = [], [], []
    for k in range(TOP_K):
        m = jnp.max(logits, axis=0, keepdims=True)
        idx = jnp.min(jnp.where(logits == m, eid, N_EXPERTS), axis=0, keepdims=True)
        hit = eid == idx
        logits = jnp.where(hit, -jnp.inf, logits)
        onehot = jnp.where(hit, 1.0, onehot)
        vals.append(m)
        hits.append(hit)
        rows.append(idx.astype(F32))
    ex = [jnp.exp(v - vals[0]) for v in vals]
    den = ex[0] + ex[1] + ex[2] + ex[3]
    rows += [e / den for e in ex]
    oh = onehot.astype(BF16)
    before = jnp.dot(oh, tri_ref[...], preferred_element_type=F32)
    cnt_col = jnp.sum(onehot, axis=1, keepdims=True)
    cnt8_col = jnp.floor((cnt_col + (SUBLANE - 1)) * (1.0 / SUBLANE)) * SUBLANE
    src_col = jnp.dot(lower_ref[...], jnp.broadcast_to(cnt8_col, (N_EXPERTS, LANE)).astype(BF16),
                      preferred_element_type=F32)[:, 0:1]
    slot_of = before + src_col
    for k in range(TOP_K):
        rows.append(jnp.sum(jnp.where(hits[k], slot_of, 0.0), axis=0, keepdims=True))
    rows += [jnp.zeros_like(rows[0])] * (RT_ROWS - len(rows))
    rt_ref[...] = jnp.concatenate(rows, axis=0)
    ones = jnp.ones((SUBLANE, TOK_ROWS), BF16)
    cnt = lax.dot_general(ones, oh, (((1,), (1,)), ((), ())), preferred_element_type=F32)
    cnt = jnp.concatenate([cnt, jnp.zeros((SUBLANE, LANE - N_EXPERTS), F32)], axis=1)
    cnt8 = jnp.floor((cnt + (SUBLANE - 1)) * (1.0 / SUBLANE)) * SUBLANE
    src = jnp.dot(cnt8.astype(BF16), upper_ref[...], preferred_element_type=F32)
    base = carry_s[...]
    row = lax.broadcasted_iota(I32, (SUBLANE, LANE), 0)
    meta = jnp.where(row == META_SRC, src, jnp.where(row == META_CNT, cnt8, jnp.where(row == META_BASE, base, 0.0)))
    meta_ref[...] = meta.astype(I32)
    carry_s[...] = base + cnt8
    tot_ref[...] = (base + cnt8).astype(I32)


def mix_out(ya, yp, ys, ym, xc, mod3, norm_g, w_out, router_w, router_b, n_ctx, skip_ctx):
    bsz, n_all, dim = xc.shape
    nct = n_ctx // TOK_ROWS
    off = nct if skip_ctx else 0
    n_out = n_all - off * TOK_ROWS
    tiles = n_out // TOK_ROWS
    tile_in = lambda w: pl.BlockSpec((None, TOK_ROWS, w), lambda b, i: (b, i + off, 0))
    tile_out = lambda w: pl.BlockSpec((None, TOK_ROWS, w), lambda b, i: (b, i, 0))

    def mod_spec(chunk):
        return pl.BlockSpec((None, 1, dim), lambda b, i: (jnp.where(i + off < nct, bsz, b), 0, chunk))
    rwt = router_w.T.astype(BF16)
    rbt = router_b.reshape(N_EXPERTS, 1).astype(F32)
    r = jnp.arange(TOK_ROWS)
    tri = (r[:, None] < r[None, :]).astype(BF16)
    q = jnp.arange(LANE)
    upper = (q[:, None] < q[None, :]).astype(BF16)
    e = jnp.arange(N_EXPERTS)
    lower = (e[None, :] < e[:, None]).astype(BF16)
    full = lambda a: pl.BlockSpec(a.shape, lambda b, i: (0,) * a.ndim)
    wo = w_out.astype(BF16)
    ng = norm_g.reshape(1, dim)
    return pl.pallas_call(
        _mix_out_kernel,
        out_shape=(jax.ShapeDtypeStruct((bsz, n_out, dim), F32), jax.ShapeDtypeStruct((bsz, n_out, dim), BF16),
                   jax.ShapeDtypeStruct((bsz * tiles, RT_ROWS, TOK_ROWS), F32),
                   jax.ShapeDtypeStruct((bsz * tiles, SUBLANE, LANE), I32),
                   jax.ShapeDtypeStruct((SUBLANE, LANE), I32)),
        grid=(bsz, tiles),
        in_specs=[tile_in(W_CONV), tile_in(W_POOL), tile_in(W_SSM), tile_in(W_MLA), tile_in(dim),
                  mod_spec(2), mod_spec(3), mod_spec(4), full(ng), full(wo), full(rwt), full(rbt), full(tri),
                  full(upper), full(lower)],
        out_specs=(tile_out(dim), tile_out(dim),
                   pl.BlockSpec((None, RT_ROWS, TOK_ROWS), lambda b, i: (b * tiles + i, 0, 0)),
                   pl.BlockSpec((None, SUBLANE, LANE), lambda b, i: (b * tiles + i, 0, 0)),
                   pl.BlockSpec((SUBLANE, LANE), lambda b, i: (0, 0))),
        scratch_shapes=[pltpu.VMEM((SUBLANE, LANE), F32)],
        compiler_params=pltpu.CompilerParams(dimension_semantics=("arbitrary", "arbitrary")),
        name="mix_out",
    )(ya, yp, ys, ym, xc, mod3, mod3, mod3, ng, wo, rwt, rbt, tri, upper, lower)


def expert_layout(meta, tot, n_tok):
    n_tiles = meta.shape[0]
    tot8 = tot[0, :N_EXPERTS]
    region = (tot8 + MOE_BLOCK - 1) // MOE_BLOCK * MOE_BLOCK
    g_end = jnp.cumsum(region)
    g_start = g_end - region
    n_blocks = -(-(n_tok * TOP_K + n_tiles * N_EXPERTS * (SUBLANE - 1)) // MOE_BLOCK) + N_EXPERTS
    block_start = jnp.arange(n_blocks) * MOE_BLOCK
    block_e = jnp.minimum(jnp.sum(g_end[None, :] <= block_start[:, None], axis=1), N_EXPERTS - 1).astype(I32)
    n_used = (g_end[-1] // MOE_BLOCK).astype(I32).reshape(1)
    g_start_l = jnp.zeros((LANE,), I32).at[:N_EXPERTS].set(g_start.astype(I32))
    runs = jnp.stack([meta[:, META_SRC], meta[:, META_CNT], meta[:, META_BASE] + g_start_l], axis=1)
    tails = jnp.zeros((2, LANE), I32).at[0, :N_EXPERTS].set((g_start + tot8).astype(I32))
    tails = tails.at[1, :N_EXPERTS].set((region - tot8).astype(I32))
    return runs, tails, block_e, n_used, n_blocks


def _run_copies(runs_ref, local, remote, sem, to_remote, wait):
    def per_expert(e, c):
        s = runs_ref[0, 0, e]
        n = runs_ref[0, 1, e]
        d = runs_ref[0, 2, e]
        for bit in RUN_BITS:
            take = (n & bit) != 0
            loc = local.at[pl.ds(pl.multiple_of(s, SUBLANE), bit), :]
            rem = remote.at[pl.ds(pl.multiple_of(d, SUBLANE), bit), :]
            cp = pltpu.make_async_copy(loc, rem, sem) if to_remote else pltpu.make_async_copy(rem, loc, sem)

            @pl.when(take)
            def _():
                if wait:
                    cp.wait()
                else:
                    cp.start()
            s = s + jnp.where(take, bit, 0)
            d = d + jnp.where(take, bit, 0)
        return c
    lax.fori_loop(0, N_EXPERTS, per_expert, 0)


def _dispatch_kernel(runs_ref, runs_prev_ref, tails_ref, h_ref, slot_ref, xs_ref, buf, zbuf, sems):
    i = pl.program_id(0)
    cur = i % 2

    @pl.when(i == 0)
    def _():
        zbuf[...] = jnp.zeros_like(zbuf)

        def tail_copies(wait):
            def per_expert(e, c):
                d = tails_ref[0, e]
                n = tails_ref[1, e]
                for bit in RUN_BITS:
                    take = (n & bit) != 0
                    cp = pltpu.make_async_copy(zbuf.at[pl.ds(0, bit), :],
                                               xs_ref.at[pl.ds(pl.multiple_of(d, SUBLANE), bit), :], sems.at[2])

                    @pl.when(take)
                    def _():
                        if wait:
                            cp.wait()
                        else:
                            cp.start()
                    d = d + jnp.where(take, bit, 0)
                return c
            lax.fori_loop(0, N_EXPERTS, per_expert, 0)
        tail_copies(False)
        tail_copies(True)

    j = lax.broadcasted_iota(I32, (RUN_ROWS, TOK_ROWS), 0)
    hit = j == slot_ref[0:1, :]
    for k in range(1, TOP_K):
        hit = jnp.logical_or(hit, j == slot_ref[k:k + 1, :])
    p = jnp.where(hit, 1.0, 0.0).astype(BF16)
    lo = jnp.dot(p, h_ref[:, :PACK_W], preferred_element_type=F32)
    hi = jnp.dot(p, h_ref[:, PACK_W:], preferred_element_type=F32)
    lo_bits = lax.shift_right_logical(pltpu.bitcast(lo, U32), jnp.uint32(16))
    hi_bits = pltpu.bitcast(hi, U32) & jnp.uint32(0xFFFF0000)
    buf[cur] = hi_bits | lo_bits
    _run_copies(runs_ref, buf.at[cur], xs_ref, sems.at[cur], True, False)

    @pl.when(i > 0)
    def _():
        _run_copies(runs_prev_ref, buf.at[1 - cur], xs_ref, sems.at[1 - cur], True, True)

    @pl.when(i == pl.num_programs(0) - 1)
    def _():
        _run_copies(runs_ref, buf.at[cur], xs_ref, sems.at[cur], True, True)


def dispatch(h, slots_t, runs, tails, n_blocks):
    n_tok, dim = h.shape
    n_tiles = n_tok // TOK_ROWS
    return pl.pallas_call(
        _dispatch_kernel,
        out_shape=jax.ShapeDtypeStruct((n_blocks * MOE_BLOCK, PACK_W), U32),
        grid=(n_tiles,),
        in_specs=[pl.BlockSpec((1, 3, LANE), lambda i: (i, 0, 0), memory_space=pltpu.SMEM),
                  pl.BlockSpec((1, 3, LANE), lambda i: (jnp.maximum(i - 1, 0), 0, 0), memory_space=pltpu.SMEM),
                  pl.BlockSpec((2, LANE), lambda i: (0, 0), memory_space=pltpu.SMEM),
                  pl.BlockSpec((TOK_ROWS, dim), lambda i: (i, 0)),
                  pl.BlockSpec((None, SUBLANE, TOK_ROWS), lambda i: (i, 0, 0))],
        out_specs=pl.BlockSpec(memory_space=pl.ANY),
        scratch_shapes=[pltpu.VMEM((2, RUN_ROWS, PACK_W), U32), pltpu.VMEM((RUN_BITS[0], PACK_W), U32),
                        pltpu.SemaphoreType.DMA((3,))],
        compiler_params=pltpu.CompilerParams(dimension_semantics=("arbitrary",)),
        name="moe_dispatch",
    )(runs, runs, tails, h, slots_t)


def _expert_block_kernel(be_ref, nu_ref, x_ref, win_ref, bin_ref, wout_ref, bout_ref, o_ref):
    i = pl.program_id(0)

    @pl.when(i < nu_ref[0])
    def _():
        w = x_ref[...]
        x_lo = pltpu.bitcast(lax.shift_left(w, jnp.uint32(16)), F32).astype(BF16)
        x_hi = pltpu.bitcast(w & jnp.uint32(0xFFFF0000), F32).astype(BF16)
        gu = (jnp.dot(x_lo, win_ref[0, :PACK_W, :], preferred_element_type=F32)
              + jnp.dot(x_hi, win_ref[0, PACK_W:, :], preferred_element_type=F32) + bin_ref[0])
        gate = jnp.minimum(gu[:, :D_EXPERT], SWIGLU_LIMIT)
        up = jnp.clip(gu[:, D_EXPERT:], -SWIGLU_LIMIT, SWIGLU_LIMIT)
        act = (up + 1.0) * gate * jax.nn.sigmoid(SWIGLU_ALPHA * gate)
        o_ref[...] = jnp.dot(act.astype(BF16), wout_ref[0], preferred_element_type=F32) + bout_ref[0]

    @pl.when(i >= nu_ref[0])
    def _():
        o_ref[...] = jnp.zeros_like(o_ref)


def expert_blocks(xs, block_e, n_used, w_in, b_in, w_out, b_out):
    n_pad = xs.shape[0]
    dim = w_out.shape[-1]
    n_blocks = n_pad // MOE_BLOCK

    def row_map(i, be, nu):
        return (jnp.minimum(i, nu[0] - 1), 0)

    def exp_map3(i, be, nu):
        return (be[i], 0, 0)

    return pl.pallas_call(
        _expert_block_kernel,
        out_shape=jax.ShapeDtypeStruct((n_pad, dim), F32),
        grid_spec=pltpu.PrefetchScalarGridSpec(
            num_scalar_prefetch=2,
            grid=(n_blocks,),
            in_specs=[
                pl.BlockSpec((MOE_BLOCK, PACK_W), row_map),
                pl.BlockSpec((1, dim, 2 * D_EXPERT), exp_map3),
                pl.BlockSpec((1, 1, 2 * D_EXPERT), exp_map3),
                pl.BlockSpec((1, D_EXPERT, dim), exp_map3),
                pl.BlockSpec((1, 1, dim), exp_map3),
            ],
            out_specs=pl.BlockSpec((MOE_BLOCK, dim), lambda i, be, nu: (i, 0)),
        ),
        compiler_params=pltpu.CompilerParams(dimension_semantics=("arbitrary",)),
        name="expert_blocks",
    )(block_e, n_used, xs, w_in, b_in[:, None, :], w_out, b_out[:, None, :])


def _combine_kernel(runs_ref, runs_next_ref, ys_ref, rt_ref, x_ref, g_ref, o_ref, ybuf, sems):
    i = pl.program_id(0)
    cur = i % 2

    @pl.when(i == 0)
    def _():
        ybuf[...] = jnp.zeros_like(ybuf)
        _run_copies(runs_ref, ybuf.at[0], ys_ref, sems.at[0], False, False)

    @pl.when(i + 1 < pl.num_programs(0))
    def _():
        _run_copies(runs_next_ref, ybuf.at[1 - cur], ys_ref, sems.at[1 - cur], False, False)

    _run_copies(runs_ref, ybuf.at[cur], ys_ref, sems.at[cur], False, True)
    rt = rt_ref[...]
    j = lax.broadcasted_iota(I32, (TOK_ROWS, RUN_ROWS), 1).astype(F32)
    wmat = jnp.zeros((TOK_ROWS, RUN_ROWS), F32)
    for k in range(TOP_K):
        wmat = jnp.where(j == rt[:, RT_SLOT + k:RT_SLOT + k + 1], rt[:, RT_WEIGHT + k:RT_WEIGHT + k + 1], wmat)
    w_hi = wmat.astype(BF16)
    w_lo = (wmat - w_hi.astype(F32)).astype(BF16)
    y = ybuf[cur].astype(BF16)
    m = jnp.dot(w_hi, y, preferred_element_type=F32) + jnp.dot(w_lo, y, preferred_element_type=F32)
    o_ref[...] = x_ref[...] + g_ref[...] * m


def combine(ys, route, runs, xn, mod3, n_ctx_tiles):
    bsz, n_rows, dim = xn.shape
    tiles = n_rows // TOK_ROWS
    tile = lambda w: pl.BlockSpec((None, TOK_ROWS, w), lambda i: (i // tiles, i % tiles, 0))
    g_spec = pl.BlockSpec((None, 1, dim), lambda i: (jnp.where(i % tiles < n_ctx_tiles, bsz, i // tiles), 0, 5))
    return pl.pallas_call(
        _combine_kernel,
        out_shape=jax.ShapeDtypeStruct(xn.shape, F32),
        grid=(bsz * tiles,),
        in_specs=[pl.BlockSpec((1, 3, LANE), lambda i: (i, 0, 0), memory_space=pltpu.SMEM),
                  pl.BlockSpec((1, 3, LANE), lambda i: (jnp.minimum(i + 1, bsz * tiles - 1), 0, 0),
                               memory_space=pltpu.SMEM),
                  pl.BlockSpec(memory_space=pl.ANY), tile(RT_ROWS), tile(dim), g_spec],
        out_specs=tile(dim),
        scratch_shapes=[pltpu.VMEM((2, RUN_ROWS, dim), F32), pltpu.SemaphoreType.DMA((2,))],
        compiler_params=pltpu.CompilerParams(dimension_semantics=("arbitrary",)),
        name="moe_combine",
    )(runs, runs, ys, route, xn, mod3)


def routed_experts(xn, h2, route_t, meta, tot, mod3, n_ctx_tiles, w_in, b_in, w_out, b_out):
    bsz, n_rows, dim = xn.shape
    n_tok = bsz * n_rows
    runs, tails, block_e, n_used, n_blocks = expert_layout(meta, tot, n_tok)
    slots_t = route_t[:, RT_SLOT:RT_SLOT + SUBLANE, :].astype(I32)
    route = jnp.swapaxes(route_t, 1, 2).reshape(bsz, n_rows, RT_ROWS)
    xs = dispatch(h2.reshape(n_tok, dim), slots_t, runs, tails, n_blocks)
    ys = expert_blocks(xs, block_e, n_used, w_in.astype(BF16), b_in, w_out.astype(BF16), b_out)
    return combine(ys, route, runs, xn, mod3, n_ctx_tiles)


def kernel(x, c, ctx, c_ctx, ada_w, ada_b, norm1_g, norm2_g, w_mix_in, w_mix_out,
           conv_dw, conv_dw_b, conv_ln_g, conv_ln_b, conv_pw, pool_w, pool_scale,
           ssm_a_re, ssm_a_im, ssm_log_dt, ssm_b_re, ssm_b_im, ssm_c_re, ssm_c_im,
           ssm_d, ssm_glu_w, ssm_glu_b,
           mla_q_a_g, mla_wq_b, mla_kv_a_g, mla_wkv_b, mla_q_g, mla_k_g,
           router_w, router_b, exp_w_in, exp_b_in, exp_w_out, exp_b_out):
    bsz, n_lat, dim = x.shape
    n_ctx = ctx.shape[1]
    xc = jnp.concatenate([ctx, x], axis=1)
    c_all = jnp.zeros((MOD_ROWS, dim), F32).at[:bsz].set(c).at[bsz].set(c_ctx)
    rope_tab = rope_lane_tables(n_ctx, n_lat)
    for l in range(DEPTH):
        last = l == DEPTH - 1
        mod = modulation(c_all, ada_w[l], ada_b[l])
        mod3 = mod.reshape(MOD_ROWS, 1, 6 * dim)

        conv_in, pool_in, ssm_in, ckv, qkp = mix_in(xc, mod3, norm1_g[l], mix_in_weights(w_mix_in[l]), n_ctx)
        ya, yp = local_mixers(conv_in, pool_in, n_ctx, conv_dw[l], conv_dw_b[l], conv_ln_g[l], conv_ln_b[l],
                              conv_pw[l], pool_w[l], pool_scale[l])
        ys = s5_mixer_pallas(ssm_in, n_ctx, ssm_a_re[l], ssm_a_im[l], ssm_log_dt[l],
                             ssm_b_re[l], ssm_b_im[l], ssm_c_re[l], ssm_c_im[l],
                             ssm_d[l], ssm_glu_w[l], ssm_glu_b[l])
        ym = mla_attention(ckv, qkp, rope_tab,
                           mla_layout_params(mla_q_a_g[l], mla_wq_b[l], mla_kv_a_g[l], mla_wkv_b[l],
                                             mla_q_g[l], mla_k_g[l]), n_ctx, not last)
        xn, h2, route, meta, tot = mix_out(ya, yp, ys, ym, xc, mod3, norm2_g[l], w_mix_out[l],
                                           router_w[l], router_b[l], n_ctx, last)
        xc = routed_experts(xn, h2, route, meta, tot, mod3, 0 if last else n_ctx // TOK_ROWS,
                            exp_w_in[l], exp_b_in[l], exp_w_out[l], exp_b_out[l])
    return xc
```

```python
import functools

import jax
import jax.numpy as jnp
import numpy as np
from jax import lax
from jax.experimental import pallas as pl
from jax.experimental.pallas import tpu as pltpu

D_MODEL = 1024
DEPTH = 2
GRID_W = 64
EPS = 1e-6

D_MIX = D_MODEL
W_CONV = D_MIX // 4
W_POOL = D_MIX // 4
W_SSM = D_MIX // 4
W_MLA = D_MIX - W_CONV - W_POOL - W_SSM
CONV_K = 31
POOL_WINDOWS = (2, 4, 8, 16)
POOL_CH = W_POOL // len(POOL_WINDOWS)
SSM_CH = 16
SSM_GROUPS = W_SSM // SSM_CH
SSM_STATE = 64
MLA_V = 64
MLA_HEADS = W_MLA // MLA_V
MLA_NOPE = 64
MLA_ROPE = 32
MLA_QK = MLA_NOPE + MLA_ROPE
MLA_Q_RANK = 192
MLA_KV_RANK = 128
ROPE_AXIS = MLA_ROPE // 2
ROPE_BASE = 10000.0
N_EXPERTS = 32
TOP_K = 4
D_EXPERT = D_MODEL
SWIGLU_LIMIT = 7.0
SWIGLU_ALPHA = 1.702
MOE_BLOCK = 512

OFF_POOL = 2 * W_CONV
OFF_Q = OFF_POOL + W_POOL
OFF_SSM = OFF_Q + MLA_Q_RANK
OFF_KV = OFF_SSM + W_SSM
OFF_KPE = OFF_KV + MLA_KV_RANK
N_IN = OFF_KPE + MLA_ROPE

F32 = jnp.float32
BF16 = jnp.bfloat16
LANE = 128
SUBLANE = 8
QKP_W = 256


MOD_ROWS = 40
MOD_COLS = 1024


def _mod_kernel(c_ref, w_ref, b_ref, o_ref):
    c = c_ref[...]
    s = (c * jax.nn.sigmoid(c)).astype(BF16)
    o_ref[...] = jnp.dot(s, w_ref[...].astype(BF16), preferred_element_type=F32) + b_ref[...]


def modulation(c_all, ada_w, ada_b):
    n_out = ada_w.shape[1]
    return pl.pallas_call(
        _mod_kernel,
        out_shape=jax.ShapeDtypeStruct((MOD_ROWS, n_out), F32),
        grid=(n_out // MOD_COLS,),
        in_specs=[pl.BlockSpec((MOD_ROWS, D_MODEL), lambda j: (0, 0)),
                  pl.BlockSpec((D_MODEL, MOD_COLS), lambda j: (0, j)),
                  pl.BlockSpec((1, MOD_COLS), lambda j: (0, j))],
        out_specs=pl.BlockSpec((MOD_ROWS, MOD_COLS), lambda j: (0, j)),
        compiler_params=pltpu.CompilerParams(dimension_semantics=("arbitrary",)),
        name="modulation",
    )(c_all, ada_w, ada_b.reshape(1, n_out))


TOK_ROWS = 256
IN_SPLITS = (2 * W_CONV, W_POOL, W_SSM, MLA_KV_RANK, QKP_W)


def _mix_in_kernel(x_ref, sh_ref, sc_ref, g_ref, w_ref, *outs):
    x = x_ref[...]
    y = x * lax.rsqrt(jnp.mean(x * x, axis=-1, keepdims=True) + EPS) * g_ref[...]
    h = y * (1.0 + sc_ref[...]) + sh_ref[...]
    p = jnp.dot(h.astype(BF16), w_ref[...], preferred_element_type=F32)
    off = 0
    for o_ref, w in zip(outs, IN_SPLITS):
        o_ref[...] = p[:, off:off + w]
        off += w


def mix_in_weights(w_mix_in):
    cols = [w_mix_in[:, :OFF_POOL], w_mix_in[:, OFF_POOL:OFF_Q], w_mix_in[:, OFF_SSM:OFF_KV],
            w_mix_in[:, OFF_KV:OFF_KPE], w_mix_in[:, OFF_Q:OFF_SSM], w_mix_in[:, OFF_KPE:],
            jnp.zeros((D_MODEL, QKP_W - MLA_Q_RANK - MLA_ROPE), w_mix_in.dtype)]
    return jnp.concatenate(cols, axis=1).astype(BF16)


def mix_in(xc, mod3, norm_g, w_in_p, n_ctx):
    bsz, n_all, dim = xc.shape
    assert n_ctx % TOK_ROWS == 0 and n_all % TOK_ROWS == 0 and bsz < MOD_ROWS
    nct = n_ctx // TOK_ROWS
    tile = lambda w: pl.BlockSpec((None, TOK_ROWS, w), lambda b, i: (b, i, 0))

    def mod_spec(chunk):
        return pl.BlockSpec((None, 1, dim), lambda b, i: (jnp.where(i < nct, bsz, b), 0, chunk))
    return pl.pallas_call(
        _mix_in_kernel,
        out_shape=tuple(jax.ShapeDtypeStruct((bsz, n_all, w), F32) for w in IN_SPLITS),
        grid=(bsz, n_all // TOK_ROWS),
        in_specs=[tile(dim), mod_spec(0), mod_spec(1),
                  pl.BlockSpec((1, dim), lambda b, i: (0, 0)),
                  pl.BlockSpec(w_in_p.shape, lambda b, i: (0, 0))],
        out_specs=tuple(tile(w) for w in IN_SPLITS),
        compiler_params=pltpu.CompilerParams(dimension_semantics=("arbitrary", "arbitrary")),
        name="mix_in",
    )(xc, mod3, mod3, norm_g.reshape(1, dim), w_in_p)


LM_PAD = 16
LM_ROWS = 64
LM_DBL_ROWS = 32


def _local_mixers_kernel(cin_ref, pin_ref, dw_ref, dwb_ref, lng_ref, lnb_ref, pw_ref, pwb_ref, psc_ref,
                         ya_ref, yp_ref, upad, ppad, s2, s4, s8, phase, *, n_ctx):
    n_all = cin_ref.shape[0]
    n_lat = n_all - n_ctx
    n_pad = upad.shape[0]
    lat0 = n_ctx + 2 * LM_PAD
    zeros = jnp.zeros((LM_PAD, W_CONV), F32)
    for buf in (upad, ppad):
        buf[0:LM_PAD, :] = zeros
        buf[LM_PAD + n_ctx:lat0, :] = zeros
        buf[lat0 + n_lat:n_pad, :] = zeros

    def pad_row(i):
        return pl.multiple_of(i * LM_ROWS + jnp.where(i < n_ctx // LM_ROWS, LM_PAD, 2 * LM_PAD), 8)

    def fill(i, _):
        r0 = pl.multiple_of(i * LM_ROWS, LM_ROWS)
        c = cin_ref[pl.ds(r0, LM_ROWS), :]
        dst = pl.ds(pad_row(i), LM_ROWS)
        upad[dst, :] = c[:, :W_CONV] * jax.nn.sigmoid(c[:, W_CONV:])
        ppad[dst, :] = pin_ref[pl.ds(r0, LM_ROWS), :]
        return 0
    lax.fori_loop(0, n_all // LM_ROWS, fill, 0)

    def doubling(src, dst, back, fwd):
        dst[0:8, :] = zeros[:8]
        dst[n_pad - 8:n_pad, :] = zeros[:8]

        def body(i, _):
            r0 = pl.multiple_of(8 + i * LM_DBL_ROWS, 8)
            h = src[pl.ds(r0 - 8, LM_DBL_ROWS + 16), :]
            dst[pl.ds(r0, LM_DBL_ROWS), :] = (h[8 - back:8 - back + LM_DBL_ROWS, :]
                                              + h[8 + fwd:8 + fwd + LM_DBL_ROWS, :])
            return 0
        lax.fori_loop(0, (n_pad - 16) // LM_DBL_ROWS, body, 0)
    doubling(ppad, s2, 1, 0)
    doubling(s2, s4, 1, 1)
    doubling(s4, s8, 2, 2)

    lane = lax.broadcasted_iota(jnp.int32, (LM_ROWS, W_POOL), 1)
    grp = lane // POOL_CH
    back = jnp.where(grp == 0, 1, jnp.where(grp == 1, 2, jnp.where(grp == 2, 4, 8)))
    fwd = back - 1

    def tile(i, _):
        r0 = pl.multiple_of(i * LM_ROWS, LM_ROWS)
        p0 = pad_row(i)
        halo = upad[pl.ds(p0 - LM_PAD, LM_ROWS + 2 * LM_PAD), :]
        span = LM_ROWS + 2 * LM_PAD - SUBLANE
        for b in range(SUBLANE):
            phase[b] = halo[b:b + span, :]
        acc = jnp.zeros((LM_ROWS, W_CONV), F32)
        for k in range(CONV_K):
            o = LM_PAD + k - CONV_K // 2
            a = o // SUBLANE * SUBLANE
            acc = acc + phase[o % SUBLANE, a:a + LM_ROWS, :] * dw_ref[k:k + 1, :]
        acc = acc + dwb_ref[...]
        mu = jnp.mean(acc, axis=-1, keepdims=True)
        var = jnp.mean(jnp.square(acc - mu), axis=-1, keepdims=True)
        v = (acc - mu) * lax.rsqrt(var + EPS) * lng_ref[...] + lnb_ref[...]
        v = v * jax.nn.sigmoid(v)
        ya_ref[pl.ds(r0, LM_ROWS), :] = jnp.dot(v.astype(BF16), pw_ref[...],
                                                preferred_element_type=F32).astype(ya_ref.dtype)
        rows = pl.ds(p0, LM_ROWS)
        h8 = s8[pl.ds(p0 - 8, LM_ROWS + 16), :]
        s16 = h8[4:4 + LM_ROWS, :] + h8[12:12 + LM_ROWS, :]
        wsum = jnp.where(grp == 0, s2[rows, :], jnp.where(grp == 1, s4[rows, :],
                                                          jnp.where(grp == 2, s8[rows, :], s16)))
        seg_len = jnp.where(i < n_ctx // LM_ROWS, n_ctx, n_lat)
        t = (r0 - jnp.where(i < n_ctx // LM_ROWS, 0, n_ctx)) + lax.broadcasted_iota(jnp.int32, (LM_ROWS, W_POOL), 0)
        cnt = jnp.minimum(t + fwd, seg_len - 1) - jnp.maximum(t - back, 0) + 1
        pooled = wsum / cnt.astype(F32) - ppad[rows, :]
        y = jnp.dot(pooled.astype(BF16), pwb_ref[...], preferred_element_type=F32) * psc_ref[...]
        yp_ref[pl.ds(r0, LM_ROWS), :] = y.astype(yp_ref.dtype)
        return 0
    lax.fori_loop(0, n_all // LM_ROWS, tile, 0)


def local_mixers(conv_in, pool_in, n_ctx, conv_dw, conv_dw_b, ln_g, ln_b, conv_pw, pool_w, pool_scale):
    bsz, n_all, _ = conv_in.shape
    assert n_ctx % LM_ROWS == 0 and n_all % LM_ROWS == 0
    n_pad = n_all + 3 * LM_PAD
    assert (n_pad - 16) % LM_DBL_ROWS == 0
    dw = jnp.zeros((CONV_K + 1, W_CONV), F32).at[:CONV_K].set(conv_dw)
    eye = jnp.eye(len(POOL_WINDOWS), dtype=F32)
    pwb = jnp.einsum('gcd,gh->gchd', pool_w, eye).reshape(W_POOL, W_POOL).astype(BF16)
    row = lambda v: v.reshape(1, -1).astype(F32)
    params = [dw, row(conv_dw_b), row(ln_g), row(ln_b), conv_pw.astype(BF16), pwb, row(pool_scale)]

    def full(arr):
        return pl.BlockSpec(arr.shape, lambda b: (0,) * arr.ndim)
    seq = lambda w: pl.BlockSpec((None, n_all, w), lambda b: (b, 0, 0))
    out = jax.ShapeDtypeStruct((bsz, n_all, W_CONV), BF16)
    return pl.pallas_call(
        functools.partial(_local_mixers_kernel, n_ctx=n_ctx),
        out_shape=(out, out),
        grid=(bsz,),
        in_specs=[seq(2 * W_CONV), seq(W_POOL)] + [full(p) for p in params],
        out_specs=(seq(W_CONV), seq(W_POOL)),
        scratch_shapes=[pltpu.VMEM((n_pad, W_CONV), F32) for _ in range(5)]
        + [pltpu.VMEM((SUBLANE, LM_ROWS + 2 * LM_PAD - SUBLANE, W_CONV), F32)],
        compiler_params=pltpu.CompilerParams(dimension_semantics=("arbitrary",)),
        name="local_mixers",
    )(conv_in, pool_in, *params)


S5_NS = SSM_GROUPS * SSM_STATE
S5_TC = 128
S5_BG = 8
S5_LANES = 512


def _s5_scan_kernel(u_ref, perm_ref, a_ref, bd_ref, cd_ref, *rest, reverse, final):
    if final:
        yp_ref, permt_ref, d_ref, gw_ref, gb_ref, y_ref, bu_s, hh_s, h_s = rest
    else:
        y_ref, bu_s, hh_s, h_s = rest
    tc = u_ref.shape[1]
    rows = S5_BG * tc
    j = pl.program_id(1)

    @pl.when(j == 0)
    def _():
        h_s[...] = jnp.zeros_like(h_s)

    u = u_ref[...].reshape(rows, W_SSM)
    u_hi = u.astype(BF16)
    if final:
        u_lo = (u - u_hi.astype(F32)).astype(BF16)
        u_tm = jnp.dot(perm_ref[...], jnp.concatenate([u_hi, u_lo], axis=1), preferred_element_type=F32)
        u_hi_tm = u_tm[:, :W_SSM].astype(BF16)
        u_f32_tm = u_tm[:, :W_SSM] + u_tm[:, W_SSM:]
    else:
        u_hi_tm = jnp.dot(perm_ref[...], u_hi, preferred_element_type=F32).astype(BF16)
    bu_s[...] = jnp.dot(u_hi_tm, bd_ref[...], preferred_element_type=F32)

    for hf in range(S5_NS // S5_LANES):
        re_cols = pl.ds(hf * S5_LANES, S5_LANES)
        im_cols = pl.ds(S5_NS + hf * S5_LANES, S5_LANES)
        a_re = jnp.broadcast_to(a_ref[0:1, hf * S5_LANES:(hf + 1) * S5_LANES], (S5_BG, S5_LANES))
        a_im = jnp.broadcast_to(a_ref[1:2, hf * S5_LANES:(hf + 1) * S5_LANES], (S5_BG, S5_LANES))

        def step(i, carry):
            h_re, h_im = carry
            t = (tc - 1 - i) if reverse else i
            r8 = pl.ds(pl.multiple_of(t * S5_BG, S5_BG), S5_BG)
            n_re = a_re * h_re - a_im * h_im + bu_s[r8, re_cols]
            n_im = a_re * h_im + a_im * h_re + bu_s[r8, im_cols]
            hh_s[r8, re_cols] = n_re
            hh_s[r8, im_cols] = n_im
            return n_re, n_im

        h_re, h_im = lax.fori_loop(0, tc, step, (h_s[:, re_cols], h_s[:, im_cols]), unroll=8)
        h_s[:, re_cols] = h_re
        h_s[:, im_cols] = h_im

    y = jnp.dot(hh_s[...].astype(BF16), cd_ref[...], preferred_element_type=F32)
    if final:
        y = y + yp_ref[...] + d_ref[...] * u_f32_tm
        z = jax.nn.gelu(y)
        y = z * jax.nn.sigmoid(jnp.dot(z.astype(BF16), gw_ref[...], preferred_element_type=F32) + gb_ref[...])
        y = jnp.dot(permt_ref[...], y.astype(BF16), preferred_element_type=F32)
        y_ref[...] = y.astype(BF16).reshape(S5_BG, tc, W_SSM)
    else:
        y_ref[...] = y


def s5_direction(u_all, perm, a, bd, cd, n_ctx, reverse, final_args=None):
    bsz, n_all, _ = u_all.shape
    tc = S5_TC
    n_chunks = n_all // tc
    ctx_chunks = n_ctx // tc
    rows = S5_BG * tc
    assert n_all % tc == 0 and n_ctx % tc == 0 and bsz % S5_BG == 0

    if reverse:
        def chunk(j):
            return jnp.where(j < ctx_chunks, ctx_chunks - 1 - j, n_chunks - 1 + ctx_chunks - j)
    else:
        def chunk(j):
            return j
    seq_spec = pl.BlockSpec((S5_BG, tc, W_SSM), lambda g, j: (g, chunk(j), 0))
    part_spec = pl.BlockSpec((None, None, rows, W_SSM), lambda g, j: (g, chunk(j), 0, 0))

    def full(arr):
        return pl.BlockSpec(arr.shape, lambda g, j: (0,) * arr.ndim)
    in_specs = [seq_spec, full(perm), full(a), full(bd), full(cd)]
    args = [u_all, perm, a, bd, cd]
    if final_args is not None:
        y_prev, d, glu_w, glu_b = final_args
        perm_t = perm.T
        in_specs += [part_spec, full(perm_t), full(d), full(glu_w), full(glu_b)]
        args += [y_prev, perm_t, d, glu_w, glu_b]
        out_shape = jax.ShapeDtypeStruct(u_all.shape, BF16)
        out_spec = seq_spec
    else:
        out_shape = jax.ShapeDtypeStruct((bsz // S5_BG, n_chunks, rows, W_SSM), F32)
        out_spec = part_spec
    return pl.pallas_call(
        functools.partial(_s5_scan_kernel, reverse=reverse, final=final_args is not None),
        out_shape=out_shape,
        grid=(bsz // S5_BG, n_chunks),
        in_specs=in_specs,
        out_specs=out_spec,
        scratch_shapes=[pltpu.VMEM((rows, 2 * S5_NS), F32), pltpu.VMEM((rows, 2 * S5_NS), F32),
                        pltpu.VMEM((S5_BG, 2 * S5_NS), F32)],
        compiler_params=pltpu.CompilerParams(dimension_semantics=("arbitrary", "arbitrary")),
        name="s5_scan_bwd" if reverse else "s5_scan_fwd",
    )(*args)


def s5_time_major_perm(tc):
    r = jnp.arange(S5_BG * tc)
    src = (r % S5_BG) * tc + r // S5_BG
    return (src[:, None] == r[None, :]).astype(BF16)


def s5_params(a_re, a_im, log_dt, b_re, b_im, c_re, c_im):
    a_re = jnp.minimum(a_re.astype(F32), -1e-4)
    a_im = a_im.astype(F32)
    dt = jnp.exp(log_dt.astype(F32))[:, None]
    mag = jnp.exp(a_re * dt)
    ab_re = mag * jnp.cos(a_im * dt)
    ab_im = mag * jnp.sin(a_im * dt)
    den = a_re * a_re + a_im * a_im
    f_re = ((ab_re - 1.0) * a_re + ab_im * a_im) / den
    f_im = (ab_im * a_re - (ab_re - 1.0) * a_im) / den
    bb_re = f_re[..., None] * b_re - f_im[..., None] * b_im
    bb_im = f_re[..., None] * b_im + f_im[..., None] * b_re
    eye = jnp.eye(SSM_GROUPS, dtype=F32)

    def in_map(bb):
        return jnp.einsum('gpc,gh->gchp', bb, eye).reshape(W_SSM, S5_NS)

    def out_map(cc):
        return jnp.einsum('gcp,gh->gphc', cc, eye).reshape(S5_NS, W_SSM)
    a = jnp.stack([ab_re.reshape(-1), ab_im.reshape(-1)])
    bd = jnp.concatenate([in_map(bb_re), in_map(bb_im)], axis=1).astype(BF16)
    cd = jnp.concatenate([out_map(c_re.astype(F32)), -out_map(c_im.astype(F32))], axis=0).astype(BF16)
    return a, bd, cd


def s5_mixer_pallas(u_all, n_ctx, a_re, a_im, log_dt, b_re, b_im, c_re, c_im, d, glu_w, glu_b):
    pf = s5_params(a_re[0], a_im[0], log_dt[0], b_re[0], b_im[0], c_re[0], c_im[0])
    pb = s5_params(a_re[1], a_im[1], log_dt[1], b_re[1], b_im[1], c_re[1], c_im[1])
    perm = s5_time_major_perm(S5_TC)
    y_f = s5_direction(u_all, perm, *pf, n_ctx, False)
    return s5_direction(u_all, perm, *pb, n_ctx, True,
                        (y_f, d.reshape(1, W_SSM).astype(F32), glu_w.astype(BF16), glu_b.reshape(1, W_SSM)))


HEAD_W = LANE
ATT_PREP_ROWS = 256
ATT_Q_ROWS = 256


def _head_dim_order():
    half = ROPE_AXIS // 2
    r = MLA_NOPE
    first = list(range(r, r + half)) + list(range(r + 2 * half, r + 3 * half))
    second = list(range(r + half, r + 2 * half)) + list(range(r + 3 * half, r + 4 * half))
    split = LANE // 2 - MLA_ROPE // 2
    return np.array(first + list(range(split)) + second + list(range(split, MLA_NOPE)))


def _to_head_lanes(w):
    order = _head_dim_order()
    pad = [(0, 0)] * (w.ndim - 1) + [(0, HEAD_W - len(order))]
    return jnp.pad(jnp.take(w, order, axis=-1), pad)


def mla_layout_params(q_a_g, wq_b, kv_a_g, wkv_b, q_g, k_g):
    wq = _to_head_lanes(wq_b.reshape(MLA_Q_RANK, MLA_HEADS, MLA_QK))
    wq_p = jnp.pad(wq, ((0, QKP_W - MLA_Q_RANK), (0, 0), (0, 0))).reshape(QKP_W, MLA_HEADS * HEAD_W).astype(BF16)
    qa_g = jnp.pad(q_a_g, (0, QKP_W - MLA_Q_RANK)).reshape(1, QKP_W)
    wkv = wkv_b.reshape(MLA_KV_RANK, MLA_HEADS, MLA_NOPE + MLA_V)
    wk = jnp.pad(wkv[:, :, :MLA_NOPE], ((0, 0), (0, 0), (0, MLA_ROPE)))
    wkv_p = jnp.concatenate([_to_head_lanes(wk).reshape(MLA_KV_RANK, -1),
                             wkv[:, :, MLA_NOPE:].reshape(MLA_KV_RANK, -1)], axis=1).astype(BF16)
    order = _head_dim_order()
    place = np.zeros((QKP_W, MLA_HEADS, HEAD_W), np.float32)
    for lane_i, dim in enumerate(order):
        if dim >= MLA_NOPE:
            place[MLA_Q_RANK + dim - MLA_NOPE, :, lane_i] = 1.0
    place = jnp.asarray(place.reshape(QKP_W, MLA_HEADS * HEAD_W), BF16)
    return (wq_p, qa_g, wkv_p, kv_a_g.reshape(1, MLA_KV_RANK), place,
            _to_head_lanes(q_g.reshape(1, MLA_QK)), _to_head_lanes(k_g.reshape(1, MLA_QK)))


def rope_lane_tables(n_ctx, n_lat):
    t = jnp.arange(n_lat)
    row = (t // GRID_W).astype(F32)
    col = (t % GRID_W).astype(F32)
    inv = ROPE_BASE ** (-jnp.arange(0, ROPE_AXIS, 2, dtype=F32) / ROPE_AXIS)
    ang = jnp.concatenate([row[:, None] * inv, col[:, None] * inv], axis=1)
    cos, sin = jnp.cos(ang), jnp.sin(ang)
    half = MLA_ROPE // 2
    one = jnp.ones((n_lat, LANE // 2 - half), F32)
    zero = jnp.zeros((n_lat, LANE // 2 - half), F32)
    cmul = jnp.concatenate([cos, one, cos, one], axis=1)
    smul = jnp.concatenate([-sin, zero, sin, zero], axis=1)
    ctx = [jnp.ones((n_ctx, HEAD_W), F32), jnp.zeros((n_ctx, HEAD_W), F32)]
    return jnp.stack([jnp.concatenate([c, x], axis=0) for c, x in zip(ctx, (cmul, smul))])


def _mla_kernel(ckv_ref, qkp_ref, rope_ref, wq_ref, qag_ref, wkv_ref, kvag_ref, place_ref, qg_ref, kg_ref,
                o_ref, q_s, k_s, v_s, *, n_ctx, ctx_queries):
    n_all = ckv_ref.shape[0]

    def rope(x, r0):
        rows = pl.ds(r0, ATT_PREP_ROWS)
        swapped = pltpu.roll(x, HEAD_W // 2, axis=1)
        return x * rope_ref[0, rows, :] + swapped * rope_ref[1, rows, :]

    def head_norm(x, g):
        ss = jnp.sum(x * x, axis=-1, keepdims=True) * (1.0 / MLA_QK)
        return x * lax.rsqrt(ss + EPS) * g

    def prep(i, _):
        r0 = pl.multiple_of(i * ATT_PREP_ROWS, ATT_PREP_ROWS)
        rows = pl.ds(r0, ATT_PREP_ROWS)
        ckv = ckv_ref[rows, :]
        kvn = ckv * lax.rsqrt(jnp.mean(ckv * ckv, axis=-1, keepdims=True) + EPS) * kvag_ref[...]
        kv = jnp.dot(kvn.astype(BF16), wkv_ref[...], preferred_element_type=F32)
        qkp = qkp_ref[rows, :]
        p_hi = qkp.astype(BF16)
        p_lo = (qkp - p_hi.astype(F32)).astype(BF16)
        kpe = (jnp.dot(p_hi, place_ref[...], preferred_element_type=F32)
               + jnp.dot(p_lo, place_ref[...], preferred_element_type=F32))
        lane = lax.broadcasted_iota(jnp.int32, qkp.shape, 1)
        qsq = jnp.where(lane < MLA_Q_RANK, qkp * qkp, 0.0)
        qn = qkp * lax.rsqrt(jnp.sum(qsq, axis=-1, keepdims=True) * (1.0 / MLA_Q_RANK) + EPS) * qag_ref[...]
        q = jnp.dot(qn.astype(BF16), wq_ref[...], preferred_element_type=F32)
        for h in range(MLA_HEADS):
            blk = slice(h * HEAD_W, (h + 1) * HEAD_W)
            k_h = rope(head_norm(kv[:, blk] + kpe[:, blk], kg_ref[...]), r0)
            q_h = rope(head_norm(q[:, blk], qg_ref[...]), r0) * (MLA_QK ** -0.5)
            k_s[h, rows, :] = k_h.astype(BF16)
            q_s[h, rows, :] = q_h.astype(BF16)
        v_s[rows, :] = kv[:, MLA_HEADS * HEAD_W:].astype(BF16)
        return 0

    lax.fori_loop(0, n_all // ATT_PREP_ROWS, prep, 0)

    lane = lax.broadcasted_iota(jnp.int32, (ATT_Q_ROWS, LANE), 1)

    def attend_rows(r0, n_keys):
        outs = []
        for pair in range(MLA_HEADS // 2):
            v2 = v_s[pl.ds(0, n_keys), pair * LANE:(pair + 1) * LANE]
            o2 = []
            for h in (2 * pair, 2 * pair + 1):
                q = q_s[h, pl.ds(r0, ATT_Q_ROWS), :]
                k = k_s[h, pl.ds(0, n_keys), :]
                s = lax.dot_general(q, k, (((1,), (1,)), ((), ())), preferred_element_type=F32)
                p = jnp.exp(s - jnp.max(s, axis=-1, keepdims=True))
                den = jnp.sum(p, axis=-1, keepdims=True)
                o2.append(jnp.dot(p.astype(BF16), v2, preferred_element_type=F32) / den)
            outs.append(jnp.where(lane < MLA_V, o2[0], o2[1]))
        return jnp.concatenate(outs, axis=1)

    for i in range(n_ctx // ATT_Q_ROWS):
        r0 = i * ATT_Q_ROWS
        if ctx_queries:
            o_ref[pl.ds(r0, ATT_Q_ROWS), :] = attend_rows(r0, n_ctx).astype(o_ref.dtype)
        else:
            o_ref[pl.ds(r0, ATT_Q_ROWS), :] = jnp.zeros((ATT_Q_ROWS, W_MLA), o_ref.dtype)

    def q_tile(i, _):
        r0 = pl.multiple_of(n_ctx + i * ATT_Q_ROWS, ATT_Q_ROWS)
        o_ref[pl.ds(r0, ATT_Q_ROWS), :] = attend_rows(r0, n_all).astype(o_ref.dtype)
        return 0

    lax.fori_loop(0, (n_all - n_ctx) // ATT_Q_ROWS, q_tile, 0, unroll=2)


def mla_attention(ckv, qkp, rope_tab, params, n_ctx, ctx_queries):
    bsz, n_all, _ = ckv.shape
    assert n_all % ATT_PREP_ROWS == 0 and n_ctx % ATT_Q_ROWS == 0 and (n_all - n_ctx) % ATT_Q_ROWS == 0

    def full(arr):
        return pl.BlockSpec(arr.shape, lambda b: (0,) * arr.ndim)
    return pl.pallas_call(
        functools.partial(_mla_kernel, n_ctx=n_ctx, ctx_queries=ctx_queries),
        out_shape=jax.ShapeDtypeStruct((bsz, n_all, W_MLA), BF16),
        grid=(bsz,),
        in_specs=[pl.BlockSpec((None, n_all, MLA_KV_RANK), lambda b: (b, 0, 0)),
                  pl.BlockSpec((None, n_all, QKP_W), lambda b: (b, 0, 0)),
                  full(rope_tab)] + [full(p) for p in params],
        out_specs=pl.BlockSpec((None, n_all, W_MLA), lambda b: (b, 0, 0)),
        scratch_shapes=[pltpu.VMEM((MLA_HEADS, n_all, HEAD_W), BF16), pltpu.VMEM((MLA_HEADS, n_all, HEAD_W), BF16),
                        pltpu.VMEM((n_all, MLA_HEADS * MLA_V), BF16)],
        compiler_params=pltpu.CompilerParams(dimension_semantics=("arbitrary",)),
        name="mla_attention",
    )(ckv, qkp, rope_tab, *params)


RUN_ROWS = 1280
RUN_BITS = (256, 128, 64, 32, 16, 8)
PACK_W = D_MODEL // 2
RT_EXPERT, RT_WEIGHT, RT_SLOT = 0, TOP_K, 2 * TOP_K
RT_ROWS = 16
META_SRC, META_CNT, META_BASE = 0, 1, 2
I32 = jnp.int32
U32 = jnp.uint32


def _mix_out_kernel(ya_ref, yp_ref, ys_ref, ym_ref, x_ref, g1_ref, sh_ref, sc_ref, ng_ref, wo_ref, rwt_ref, rbt_ref,
                    tri_ref, upper_ref, lower_ref, xo_ref, h_ref, rt_ref, meta_ref, tot_ref, carry_s):
    first = jnp.logical_and(pl.program_id(0) == 0, pl.program_id(1) == 0)

    @pl.when(first)
    def _():
        carry_s[...] = jnp.zeros_like(carry_s)

    acc = jnp.dot(ya_ref[...], wo_ref[0:W_CONV, :], preferred_element_type=F32)
    acc += jnp.dot(yp_ref[...], wo_ref[W_CONV:W_CONV + W_POOL, :], preferred_element_type=F32)
    acc += jnp.dot(ys_ref[...], wo_ref[W_CONV + W_POOL:D_MIX - W_MLA, :], preferred_element_type=F32)
    acc += jnp.dot(ym_ref[...], wo_ref[D_MIX - W_MLA:, :], preferred_element_type=F32)
    x = x_ref[...] + g1_ref[...] * acc
    xo_ref[...] = x
    y = x * lax.rsqrt(jnp.mean(x * x, axis=-1, keepdims=True) + EPS) * ng_ref[...]
    h = (y * (1.0 + sc_ref[...]) + sh_ref[...]).astype(BF16)
    h_ref[...] = h
    logits = lax.dot_general(rwt_ref[...], h, (((1,), (1,)), ((), ())), preferred_element_type=F32) + rbt_ref[...]
    eid = lax.broadcasted_iota(I32, logits.shape, 0)
    onehot = jnp.zeros(logits.shape, F32)
    vals, hits, rows = [], [], []
    for k in range(TOP_K):
        m = jnp.max(logits, axis=0, keepdims=True)
        idx = jnp.min(jnp.where(logits == m, eid, N_EXPERTS), axis=0, keepdims=True)
        hit = eid == idx
        logits = jnp.where(hit, -jnp.inf, logits)
        onehot = jnp.where(hit, 1.0, onehot)
        vals.append(m)
        hits.append(hit)
        rows.append(idx.astype(F32))
    ex = [jnp.exp(v - vals[0]) for v in vals]
    den = ex[0] + ex[1] + ex[2] + ex[3]
    rows += [e / den for e in ex]
    oh = onehot.astype(BF16)
    before = jnp.dot(oh, tri_ref[...], preferred_element_type=F32)
    cnt_col = jnp.sum(onehot, axis=1, keepdims=True)
    cnt8_col = jnp.floor((cnt_col + (SUBLANE - 1)) * (1.0 / SUBLANE)) * SUBLANE
    src_col = jnp.dot(lower_ref[...], jnp.broadcast_to(cnt8_col, (N_EXPERTS, LANE)).astype(BF16),
                      preferred_element_type=F32)[:, 0:1]
    slot_of = before + src_col
    for k in range(TOP_K):
        rows.append(jnp.sum(jnp.where(hits[k], slot_of, 0.0), axis=0, keepdims=True))
    rows += [jnp.zeros_like(rows[0])] * (RT_ROWS - len(rows))
    rt_ref[...] = jnp.concatenate(rows, axis=0)
    ones = jnp.ones((SUBLANE, TOK_ROWS), BF16)
    cnt = lax.dot_general(ones, oh, (((1,), (1,)), ((), ())), preferred_element_type=F32)
    cnt = jnp.concatenate([cnt, jnp.zeros((SUBLANE, LANE - N_EXPERTS), F32)], axis=1)
    cnt8 = jnp.floor((cnt + (SUBLANE - 1)) * (1.0 / SUBLANE)) * SUBLANE
    src = jnp.dot(cnt8.astype(BF16), upper_ref[...], preferred_element_type=F32)
    base = carry_s[...]
    row = lax.broadcasted_iota(I32, (SUBLANE, LANE), 0)
    meta = jnp.where(row == META_SRC, src, jnp.where(row == META_CNT, cnt8, jnp.where(row == META_BASE, base, 0.0)))
    meta_ref[...] = meta.astype(I32)
    carry_s[...] = base + cnt8
    tot_ref[...] = (base + cnt8).astype(I32)


def mix_out(ya, yp, ys, ym, xc, mod3, norm_g, w_out, router_w, router_b, n_ctx, skip_ctx):
    bsz, n_all, dim = xc.shape
    nct = n_ctx // TOK_ROWS
    off = nct if skip_ctx else 0
    n_out = n_all - off * TOK_ROWS
    tiles = n_out // TOK_ROWS
    tile_in = lambda w: pl.BlockSpec((None, TOK_ROWS, w), lambda b, i: (b, i + off, 0))
    tile_out = lambda w: pl.BlockSpec((None, TOK_ROWS, w), lambda b, i: (b, i, 0))

    def mod_spec(chunk):
        return pl.BlockSpec((None, 1, dim), lambda b, i: (jnp.where(i + off < nct, bsz, b), 0, chunk))
    rwt = router_w.T.astype(BF16)
    rbt = router_b.reshape(N_EXPERTS, 1).astype(F32)
    r = jnp.arange(TOK_ROWS)
    tri = (r[:, None] < r[None, :]).astype(BF16)
    q = jnp.arange(LANE)
    upper = (q[:, None] < q[None, :]).astype(BF16)
    e = jnp.arange(N_EXPERTS)
    lower = (e[None, :] < e[:, None]).astype(BF16)
    full = lambda a: pl.BlockSpec(a.shape, lambda b, i: (0,) * a.ndim)
    wo = w_out.astype(BF16)
    ng = norm_g.reshape(1, dim)
    return pl.pallas_call(
        _mix_out_kernel,
        out_shape=(jax.ShapeDtypeStruct((bsz, n_out, dim), F32), jax.ShapeDtypeStruct((bsz, n_out, dim), BF16),
                   jax.ShapeDtypeStruct((bsz * tiles, RT_ROWS, TOK_ROWS), F32),
                   jax.ShapeDtypeStruct((bsz * tiles, SUBLANE, LANE), I32),
                   jax.ShapeDtypeStruct((SUBLANE, LANE), I32)),
        grid=(bsz, tiles),
        in_specs=[tile_in(W_CONV), tile_in(W_POOL), tile_in(W_SSM), tile_in(W_MLA), tile_in(dim),
                  mod_spec(2), mod_spec(3), mod_spec(4), full(ng), full(wo), full(rwt), full(rbt), full(tri),
                  full(upper), full(lower)],
        out_specs=(tile_out(dim), tile_out(dim),
                   pl.BlockSpec((None, RT_ROWS, TOK_ROWS), lambda b, i: (b * tiles + i, 0, 0)),
                   pl.BlockSpec((None, SUBLANE, LANE), lambda b, i: (b * tiles + i, 0, 0)),
                   pl.BlockSpec((SUBLANE, LANE), lambda b, i: (0, 0))),
        scratch_shapes=[pltpu.VMEM((SUBLANE, LANE), F32)],
        compiler_params=pltpu.CompilerParams(dimension_semantics=("arbitrary", "arbitrary")),
        name="mix_out",
    )(ya, yp, ys, ym, xc, mod3, mod3, mod3, ng, wo, rwt, rbt, tri, upper, lower)


def expert_layout(meta, tot, n_tok):
    n_tiles = meta.shape[0]
    tot8 = tot[0, :N_EXPERTS]
    region = (tot8 + MOE_BLOCK - 1) // MOE_BLOCK * MOE_BLOCK
    g_end = jnp.cumsum(region)
    g_start = g_end - region
    n_blocks = -(-(n_tok * TOP_K + n_tiles * N_EXPERTS * (SUBLANE - 1)) // MOE_BLOCK) + N_EXPERTS
    block_start = jnp.arange(n_blocks) * MOE_BLOCK
    block_e = jnp.minimum(jnp.sum(g_end[None, :] <= block_start[:, None], axis=1), N_EXPERTS - 1).astype(I32)
    n_used = (g_end[-1] // MOE_BLOCK).astype(I32).reshape(1)
    g_start_l = jnp.zeros((LANE,), I32).at[:N_EXPERTS].set(g_start.astype(I32))
    runs = jnp.stack([meta[:, META_SRC], meta[:, META_CNT], meta[:, META_BASE] + g_start_l], axis=1)
    tails = jnp.zeros((2, LANE), I32).at[0, :N_EXPERTS].set((g_start + tot8).astype(I32))
    tails = tails.at[1, :N_EXPERTS].set((region - tot8).astype(I32))
    return runs, tails, block_e, n_used, n_blocks


def _run_copies(runs_ref, local, remote, sem, to_remote, wait):
    def per_expert(e, c):
        s = runs_ref[0, 0, e]
        n = runs_ref[0, 1, e]
        d = runs_ref[0, 2, e]
        for bit in RUN_BITS:
            take = (n & bit) != 0
            loc = local.at[pl.ds(pl.multiple_of(s, SUBLANE), bit), :]
            rem = remote.at[pl.ds(pl.multiple_of(d, SUBLANE), bit), :]
            cp = pltpu.make_async_copy(loc, rem, sem) if to_remote else pltpu.make_async_copy(rem, loc, sem)

            @pl.when(take)
            def _():
                if wait:
                    cp.wait()
                else:
                    cp.start()
            s = s + jnp.where(take, bit, 0)
            d = d + jnp.where(take, bit, 0)
        return c
    lax.fori_loop(0, N_EXPERTS, per_expert, 0)


def _dispatch_kernel(runs_ref, runs_prev_ref, tails_ref, h_ref, slot_ref, xs_ref, buf, zbuf, sems):
    i = pl.program_id(0)
    cur = i % 2

    @pl.when(i == 0)
    def _():
        zbuf[...] = jnp.zeros_like(zbuf)

        def tail_copies(wait):
            def per_expert(e, c):
                d = tails_ref[0, e]
                n = tails_ref[1, e]
                for bit in RUN_BITS:
                    take = (n & bit) != 0
                    cp = pltpu.make_async_copy(zbuf.at[pl.ds(0, bit), :],
                                               xs_ref.at[pl.ds(pl.multiple_of(d, SUBLANE), bit), :], sems.at[2])

                    @pl.when(take)
                    def _():
                        if wait:
                            cp.wait()
                        else:
                            cp.start()
                    d = d + jnp.where(take, bit, 0)
                return c
            lax.fori_loop(0, N_EXPERTS, per_expert, 0)
        tail_copies(False)
        tail_copies(True)

    j = lax.broadcasted_iota(I32, (RUN_ROWS, TOK_ROWS), 0)
    hit = j == slot_ref[0:1, :]
    for k in range(1, TOP_K):
        hit = jnp.logical_or(hit, j == slot_ref[k:k + 1, :])
    p = jnp.where(hit, 1.0, 0.0).astype(BF16)
    lo = jnp.dot(p, h_ref[:, :PACK_W], preferred_element_type=F32)
    hi = jnp.dot(p, h_ref[:, PACK_W:], preferred_element_type=F32)
    lo_bits = lax.shift_right_logical(pltpu.bitcast(lo, U32), jnp.uint32(16))
    hi_bits = pltpu.bitcast(hi, U32) & jnp.uint32(0xFFFF0000)
    buf[cur] = hi_bits | lo_bits
    _run_copies(runs_ref, buf.at[cur], xs_ref, sems.at[cur], True, False)

    @pl.when(i > 0)
    def _():
        _run_copies(runs_prev_ref, buf.at[1 - cur], xs_ref, sems.at[1 - cur], True, True)

    @pl.when(i == pl.num_programs(0) - 1)
    def _():
        _run_copies(runs_ref, buf.at[cur], xs_ref, sems.at[cur], True, True)


def dispatch(h, slots_t, runs, tails, n_blocks):
    n_tok, dim = h.shape
    n_tiles = n_tok // TOK_ROWS
    return pl.pallas_call(
        _dispatch_kernel,
        out_shape=jax.ShapeDtypeStruct((n_blocks * MOE_BLOCK, PACK_W), U32),
        grid=(n_tiles,),
        in_specs=[pl.BlockSpec((1, 3, LANE), lambda i: (i, 0, 0), memory_space=pltpu.SMEM),
                  pl.BlockSpec((1, 3, LANE), lambda i: (jnp.maximum(i - 1, 0), 0, 0), memory_space=pltpu.SMEM),
                  pl.BlockSpec((2, LANE), lambda i: (0, 0), memory_space=pltpu.SMEM),
                  pl.BlockSpec((TOK_ROWS, dim), lambda i: (i, 0)),
                  pl.BlockSpec((None, SUBLANE, TOK_ROWS), lambda i: (i, 0, 0))],
        out_specs=pl.BlockSpec(memory_space=pl.ANY),
        scratch_shapes=[pltpu.VMEM((2, RUN_ROWS, PACK_W), U32), pltpu.VMEM((RUN_BITS[0], PACK_W), U32),
                        pltpu.SemaphoreType.DMA((3,))],
        compiler_params=pltpu.CompilerParams(dimension_semantics=("arbitrary",)),
        name="moe_dispatch",
    )(runs, runs, tails, h, slots_t)


def _expert_block_kernel(be_ref, nu_ref, x_ref, win_ref, bin_ref, wout_ref, bout_ref, o_ref, win_s, wout_s):
    i = pl.program_id(0)
    used = i < nu_ref[0]
    new_expert = jnp.logical_or(i == 0, be_ref[i] != be_ref[jnp.maximum(i - 1, 0)])

    @pl.when(jnp.logical_and(used, new_expert))
    def _():
        win_s[...] = win_ref[...].astype(BF16)
        wout_s[...] = wout_ref[...].astype(BF16)

    @pl.when(used)
    def _():
        w = x_ref[...]
        x_lo = pltpu.bitcast(lax.shift_left(w, jnp.uint32(16)), F32).astype(BF16)
        x_hi = pltpu.bitcast(w & jnp.uint32(0xFFFF0000), F32).astype(BF16)
        gu = (jnp.dot(x_lo, win_s[:PACK_W, :], preferred_element_type=F32)
              + jnp.dot(x_hi, win_s[PACK_W:, :], preferred_element_type=F32) + bin_ref[...])
        gate = jnp.minimum(gu[:, :D_EXPERT], SWIGLU_LIMIT)
        up = jnp.clip(gu[:, D_EXPERT:], -SWIGLU_LIMIT, SWIGLU_LIMIT)
        act = (up + 1.0) * gate * jax.nn.sigmoid(SWIGLU_ALPHA * gate)
        o_ref[...] = jnp.dot(act.astype(BF16), wout_s[...], preferred_element_type=F32) + bout_ref[...]

    @pl.when(jnp.logical_not(used))
    def _():
        o_ref[...] = jnp.zeros_like(o_ref)


def expert_blocks(xs, block_e, n_used, layer, w_in, b_in, w_out, b_out):
    n_pad = xs.shape[0]
    dim = w_out.shape[-1]
    n_blocks = n_pad // MOE_BLOCK

    def row_map(i, be, nu):
        return (jnp.minimum(i, nu[0] - 1), 0)

    def exp_map(i, be, nu):
        return (layer, be[i], 0, 0)

    return pl.pallas_call(
        _expert_block_kernel,
        out_shape=jax.ShapeDtypeStruct((n_pad, dim), F32),
        grid_spec=pltpu.PrefetchScalarGridSpec(
            num_scalar_prefetch=2,
            grid=(n_blocks,),
            in_specs=[
                pl.BlockSpec((MOE_BLOCK, PACK_W), row_map),
                pl.BlockSpec((None, None, dim, 2 * D_EXPERT), exp_map),
                pl.BlockSpec((None, None, 1, 2 * D_EXPERT), exp_map),
                pl.BlockSpec((None, None, D_EXPERT, dim), exp_map),
                pl.BlockSpec((None, None, 1, dim), exp_map),
            ],
            out_specs=pl.BlockSpec((MOE_BLOCK, dim), lambda i, be, nu: (i, 0)),
            scratch_shapes=[pltpu.VMEM((dim, 2 * D_EXPERT), BF16), pltpu.VMEM((D_EXPERT, dim), BF16)],
        ),
        compiler_params=pltpu.CompilerParams(dimension_semantics=("arbitrary",)),
        name="expert_blocks",
    )(block_e, n_used, xs, w_in, b_in[:, :, None, :], w_out, b_out[:, :, None, :])


def _combine_kernel(runs_ref, runs_next_ref, ys_ref, rt_ref, x_ref, g_ref, o_ref, ybuf, sems):
    i = pl.program_id(0)
    cur = i % 2

    @pl.when(i == 0)
    def _():
        ybuf[...] = jnp.zeros_like(ybuf)
        _run_copies(runs_ref, ybuf.at[0], ys_ref, sems.at[0], False, False)

    @pl.when(i + 1 < pl.num_programs(0))
    def _():
        _run_copies(runs_next_ref, ybuf.at[1 - cur], ys_ref, sems.at[1 - cur], False, False)

    _run_copies(runs_ref, ybuf.at[cur], ys_ref, sems.at[cur], False, True)
    rt = rt_ref[...]
    j = lax.broadcasted_iota(I32, (TOK_ROWS, RUN_ROWS), 1).astype(F32)
    wmat = jnp.zeros((TOK_ROWS, RUN_ROWS), F32)
    for k in range(TOP_K):
        wmat = jnp.where(j == rt[:, RT_SLOT + k:RT_SLOT + k + 1], rt[:, RT_WEIGHT + k:RT_WEIGHT + k + 1], wmat)
    w_hi = wmat.astype(BF16)
    w_lo = (wmat - w_hi.astype(F32)).astype(BF16)
    y = ybuf[cur].astype(BF16)
    m = jnp.dot(w_hi, y, preferred_element_type=F32) + jnp.dot(w_lo, y, preferred_element_type=F32)
    o_ref[...] = x_ref[...] + g_ref[...] * m


def combine(ys, route, runs, xn, mod3, n_ctx_tiles):
    bsz, n_rows, dim = xn.shape
    tiles = n_rows // TOK_ROWS
    tile = lambda w: pl.BlockSpec((None, TOK_ROWS, w), lambda i: (i // tiles, i % tiles, 0))
    g_spec = pl.BlockSpec((None, 1, dim), lambda i: (jnp.where(i % tiles < n_ctx_tiles, bsz, i // tiles), 0, 5))
    return pl.pallas_call(
        _combine_kernel,
        out_shape=jax.ShapeDtypeStruct(xn.shape, F32),
        grid=(bsz * tiles,),
        in_specs=[pl.BlockSpec((1, 3, LANE), lambda i: (i, 0, 0), memory_space=pltpu.SMEM),
                  pl.BlockSpec((1, 3, LANE), lambda i: (jnp.minimum(i + 1, bsz * tiles - 1), 0, 0),
                               memory_space=pltpu.SMEM),
                  pl.BlockSpec(memory_space=pl.ANY), tile(RT_ROWS), tile(dim), g_spec],
        out_specs=tile(dim),
        scratch_shapes=[pltpu.VMEM((2, RUN_ROWS, dim), F32), pltpu.SemaphoreType.DMA((2,))],
        compiler_params=pltpu.CompilerParams(dimension_semantics=("arbitrary",)),
        name="moe_combine",
    )(runs, runs, ys, route, xn, mod3)


def routed_experts(xn, h2, route_t, meta, tot, mod3, n_ctx_tiles, layer, w_in, b_in, w_out, b_out):
    bsz, n_rows, dim = xn.shape
    n_tok = bsz * n_rows
    runs, tails, block_e, n_used, n_blocks = expert_layout(meta, tot, n_tok)
    slots_t = route_t[:, RT_SLOT:RT_SLOT + SUBLANE, :].astype(I32)
    route = jnp.swapaxes(route_t, 1, 2).reshape(bsz, n_rows, RT_ROWS)
    xs = dispatch(h2.reshape(n_tok, dim), slots_t, runs, tails, n_blocks)
    ys = expert_blocks(xs, block_e, n_used, layer, w_in, b_in, w_out, b_out)
    return combine(ys, route, runs, xn, mod3, n_ctx_tiles)


def kernel(x, c, ctx, c_ctx, ada_w, ada_b, norm1_g, norm2_g, w_mix_in, w_mix_out,
           conv_dw, conv_dw_b, conv_ln_g, conv_ln_b, conv_pw, pool_w, pool_scale,
           ssm_a_re, ssm_a_im, ssm_log_dt, ssm_b_re, ssm_b_im, ssm_c_re, ssm_c_im,
           ssm_d, ssm_glu_w, ssm_glu_b,
           mla_q_a_g, mla_wq_b, mla_kv_a_g, mla_wkv_b, mla_q_g, mla_k_g,
           router_w, router_b, exp_w_in, exp_b_in, exp_w_out, exp_b_out):
    bsz, n_lat, dim = x.shape
    n_ctx = ctx.shape[1]
    xc = jnp.concatenate([ctx, x], axis=1)
    c_all = jnp.zeros((MOD_ROWS, dim), F32).at[:bsz].set(c).at[bsz].set(c_ctx)
    rope_tab = rope_lane_tables(n_ctx, n_lat)
    for l in range(DEPTH):
        last = l == DEPTH - 1
        mod = modulation(c_all, ada_w[l], ada_b[l])
        mod3 = mod.reshape(MOD_ROWS, 1, 6 * dim)

        conv_in, pool_in, ssm_in, ckv, qkp = mix_in(xc, mod3, norm1_g[l], mix_in_weights(w_mix_in[l]), n_ctx)
        ya, yp = local_mixers(conv_in, pool_in, n_ctx, conv_dw[l], conv_dw_b[l], conv_ln_g[l], conv_ln_b[l],
                              conv_pw[l], pool_w[l], pool_scale[l])
        ys = s5_mixer_pallas(ssm_in, n_ctx, ssm_a_re[l], ssm_a_im[l], ssm_log_dt[l],
                             ssm_b_re[l], ssm_b_im[l], ssm_c_re[l], ssm_c_im[l],
                             ssm_d[l], ssm_glu_w[l], ssm_glu_b[l])
        ym = mla_attention(ckv, qkp, rope_tab,
                           mla_layout_params(mla_q_a_g[l], mla_wq_b[l], mla_kv_a_g[l], mla_wkv_b[l],
                                             mla_q_g[l], mla_k_g[l]), n_ctx, not last)
        xn, h2, route, meta, tot = mix_out(ya, yp, ys, ym, xc, mod3, norm2_g[l], w_mix_out[l],
                                           router_w[l], router_b[l], n_ctx, last)
        xc = routed_experts(xn, h2, route, meta, tot, mod3, 0 if last else n_ctx // TOK_ROWS,
                            l, exp_w_in, exp_b_in, exp_w_out, exp_b_out)
    return xc
```

```python
import functools

import jax
import jax.numpy as jnp
import numpy as np
from jax import lax
from jax.experimental import pallas as pl
from jax.experimental.pallas import tpu as pltpu

D_MODEL = 1024
DEPTH = 2
GRID_W = 64
EPS = 1e-6

D_MIX = D_MODEL
W_CONV = D_MIX // 4
W_POOL = D_MIX // 4
W_SSM = D_MIX // 4
W_MLA = D_MIX - W_CONV - W_POOL - W_SSM
CONV_K = 31
POOL_WINDOWS = (2, 4, 8, 16)
POOL_CH = W_POOL // len(POOL_WINDOWS)
SSM_CH = 16
SSM_GROUPS = W_SSM // SSM_CH
SSM_STATE = 64
MLA_V = 64
MLA_HEADS = W_MLA // MLA_V
MLA_NOPE = 64
MLA_ROPE = 32
MLA_QK = MLA_NOPE + MLA_ROPE
MLA_Q_RANK = 192
MLA_KV_RANK = 128
ROPE_AXIS = MLA_ROPE // 2
ROPE_BASE = 10000.0
N_EXPERTS = 32
TOP_K = 4
D_EXPERT = D_MODEL
SWIGLU_LIMIT = 7.0
SWIGLU_ALPHA = 1.702
MOE_BLOCK = 512

OFF_POOL = 2 * W_CONV
OFF_Q = OFF_POOL + W_POOL
OFF_SSM = OFF_Q + MLA_Q_RANK
OFF_KV = OFF_SSM + W_SSM
OFF_KPE = OFF_KV + MLA_KV_RANK
N_IN = OFF_KPE + MLA_ROPE

F32 = jnp.float32
BF16 = jnp.bfloat16
LANE = 128
SUBLANE = 8
QKP_W = 256


MOD_ROWS = 40
MOD_COLS = 1024


def _mod_kernel(c_ref, w_ref, b_ref, o_ref):
    c = c_ref[...]
    s = (c * jax.nn.sigmoid(c)).astype(BF16)
    o_ref[...] = jnp.dot(s, w_ref[...].astype(BF16), preferred_element_type=F32) + b_ref[...]


def modulation(c_all, ada_w, ada_b):
    n_out = ada_w.shape[1]
    return pl.pallas_call(
        _mod_kernel,
        out_shape=jax.ShapeDtypeStruct((MOD_ROWS, n_out), F32),
        grid=(n_out // MOD_COLS,),
        in_specs=[pl.BlockSpec((MOD_ROWS, D_MODEL), lambda j: (0, 0)),
                  pl.BlockSpec((D_MODEL, MOD_COLS), lambda j: (0, j)),
                  pl.BlockSpec((1, MOD_COLS), lambda j: (0, j))],
        out_specs=pl.BlockSpec((MOD_ROWS, MOD_COLS), lambda j: (0, j)),
        compiler_params=pltpu.CompilerParams(dimension_semantics=("arbitrary",)),
        name="modulation",
    )(c_all, ada_w, ada_b.reshape(1, n_out))


TOK_ROWS = 256
IN_SPLITS = (2 * W_CONV, W_POOL, W_SSM, MLA_KV_RANK, QKP_W)


def _mix_in_kernel(x_ref, sh_ref, sc_ref, g_ref, w_ref, *outs):
    x = x_ref[...]
    y = x * lax.rsqrt(jnp.mean(x * x, axis=-1, keepdims=True) + EPS) * g_ref[...]
    h = y * (1.0 + sc_ref[...]) + sh_ref[...]
    p = jnp.dot(h.astype(BF16), w_ref[...], preferred_element_type=F32)
    off = 0
    for o_ref, w in zip(outs, IN_SPLITS):
        o_ref[...] = p[:, off:off + w]
        off += w


def mix_in_weights(w_mix_in):
    cols = [w_mix_in[:, :OFF_POOL], w_mix_in[:, OFF_POOL:OFF_Q], w_mix_in[:, OFF_SSM:OFF_KV],
            w_mix_in[:, OFF_KV:OFF_KPE], w_mix_in[:, OFF_Q:OFF_SSM], w_mix_in[:, OFF_KPE:],
            jnp.zeros((D_MODEL, QKP_W - MLA_Q_RANK - MLA_ROPE), w_mix_in.dtype)]
    return jnp.concatenate(cols, axis=1).astype(BF16)


def mix_in(xc, mod3, norm_g, w_in_p, n_ctx):
    bsz, n_all, dim = xc.shape
    assert n_ctx % TOK_ROWS == 0 and n_all % TOK_ROWS == 0 and bsz < MOD_ROWS
    nct = n_ctx // TOK_ROWS
    tile = lambda w: pl.BlockSpec((None, TOK_ROWS, w), lambda b, i: (b, i, 0))

    def mod_spec(chunk):
        return pl.BlockSpec((None, 1, dim), lambda b, i: (jnp.where(i < nct, bsz, b), 0, chunk))
    return pl.pallas_call(
        _mix_in_kernel,
        out_shape=tuple(jax.ShapeDtypeStruct((bsz, n_all, w), F32) for w in IN_SPLITS),
        grid=(bsz, n_all // TOK_ROWS),
        in_specs=[tile(dim), mod_spec(0), mod_spec(1),
                  pl.BlockSpec((1, dim), lambda b, i: (0, 0)),
                  pl.BlockSpec(w_in_p.shape, lambda b, i: (0, 0))],
        out_specs=tuple(tile(w) for w in IN_SPLITS),
        compiler_params=pltpu.CompilerParams(dimension_semantics=("arbitrary", "arbitrary")),
        name="mix_in",
    )(xc, mod3, mod3, norm_g.reshape(1, dim), w_in_p)


LM_PAD = 16
LM_ROWS = 64
LM_DBL_ROWS = 32


def _local_mixers_kernel(cin_ref, pin_ref, dw_ref, dwb_ref, lng_ref, lnb_ref, pw_ref, pwb_ref, psc_ref,
                         ya_ref, yp_ref, upad, ppad, s2, s4, s8, phase, *, n_ctx):
    n_all = cin_ref.shape[0]
    n_lat = n_all - n_ctx
    n_pad = upad.shape[0]
    lat0 = n_ctx + 2 * LM_PAD
    zeros = jnp.zeros((LM_PAD, W_CONV), F32)
    for buf in (upad, ppad):
        buf[0:LM_PAD, :] = zeros
        buf[LM_PAD + n_ctx:lat0, :] = zeros
        buf[lat0 + n_lat:n_pad, :] = zeros

    def pad_row(i):
        return pl.multiple_of(i * LM_ROWS + jnp.where(i < n_ctx // LM_ROWS, LM_PAD, 2 * LM_PAD), 8)

    def fill(i, _):
        r0 = pl.multiple_of(i * LM_ROWS, LM_ROWS)
        c = cin_ref[pl.ds(r0, LM_ROWS), :]
        dst = pl.ds(pad_row(i), LM_ROWS)
        upad[dst, :] = c[:, :W_CONV] * jax.nn.sigmoid(c[:, W_CONV:])
        ppad[dst, :] = pin_ref[pl.ds(r0, LM_ROWS), :]
        return 0
    lax.fori_loop(0, n_all // LM_ROWS, fill, 0)

    def doubling(src, dst, back, fwd):
        dst[0:8, :] = zeros[:8]
        dst[n_pad - 8:n_pad, :] = zeros[:8]

        def body(i, _):
            r0 = pl.multiple_of(8 + i * LM_DBL_ROWS, 8)
            h = src[pl.ds(r0 - 8, LM_DBL_ROWS + 16), :]
            dst[pl.ds(r0, LM_DBL_ROWS), :] = (h[8 - back:8 - back + LM_DBL_ROWS, :]
                                              + h[8 + fwd:8 + fwd + LM_DBL_ROWS, :])
            return 0
        lax.fori_loop(0, (n_pad - 16) // LM_DBL_ROWS, body, 0)
    doubling(ppad, s2, 1, 0)
    doubling(s2, s4, 1, 1)
    doubling(s4, s8, 2, 2)

    lane = lax.broadcasted_iota(jnp.int32, (LM_ROWS, W_POOL), 1)
    grp = lane // POOL_CH
    back = jnp.where(grp == 0, 1, jnp.where(grp == 1, 2, jnp.where(grp == 2, 4, 8)))
    fwd = back - 1

    def tile(i, _):
        r0 = pl.multiple_of(i * LM_ROWS, LM_ROWS)
        p0 = pad_row(i)
        halo = upad[pl.ds(p0 - LM_PAD, LM_ROWS + 2 * LM_PAD), :]
        span = LM_ROWS + 2 * LM_PAD - SUBLANE
        for b in range(SUBLANE):
            phase[b] = halo[b:b + span, :]
        acc = jnp.zeros((LM_ROWS, W_CONV), F32)
        for k in range(CONV_K):
            o = LM_PAD + k - CONV_K // 2
            a = o // SUBLANE * SUBLANE
            acc = acc + phase[o % SUBLANE, a:a + LM_ROWS, :] * dw_ref[k:k + 1, :]
        acc = acc + dwb_ref[...]
        mu = jnp.mean(acc, axis=-1, keepdims=True)
        var = jnp.mean(jnp.square(acc - mu), axis=-1, keepdims=True)
        v = (acc - mu) * lax.rsqrt(var + EPS) * lng_ref[...] + lnb_ref[...]
        v = v * jax.nn.sigmoid(v)
        ya_ref[pl.ds(r0, LM_ROWS), :] = jnp.dot(v.astype(BF16), pw_ref[...],
                                                preferred_element_type=F32).astype(ya_ref.dtype)
        rows = pl.ds(p0, LM_ROWS)
        h8 = s8[pl.ds(p0 - 8, LM_ROWS + 16), :]
        s16 = h8[4:4 + LM_ROWS, :] + h8[12:12 + LM_ROWS, :]
        wsum = jnp.where(grp == 0, s2[rows, :], jnp.where(grp == 1, s4[rows, :],
                                                          jnp.where(grp == 2, s8[rows, :], s16)))
        seg_len = jnp.where(i < n_ctx // LM_ROWS, n_ctx, n_lat)
        t = (r0 - jnp.where(i < n_ctx // LM_ROWS, 0, n_ctx)) + lax.broadcasted_iota(jnp.int32, (LM_ROWS, W_POOL), 0)
        cnt = jnp.minimum(t + fwd, seg_len - 1) - jnp.maximum(t - back, 0) + 1
        pooled = wsum / cnt.astype(F32) - ppad[rows, :]
        y = jnp.dot(pooled.astype(BF16), pwb_ref[...], preferred_element_type=F32) * psc_ref[...]
        yp_ref[pl.ds(r0, LM_ROWS), :] = y.astype(yp_ref.dtype)
        return 0
    lax.fori_loop(0, n_all // LM_ROWS, tile, 0)


def local_mixers(conv_in, pool_in, n_ctx, conv_dw, conv_dw_b, ln_g, ln_b, conv_pw, pool_w, pool_scale):
    bsz, n_all, _ = conv_in.shape
    assert n_ctx % LM_ROWS == 0 and n_all % LM_ROWS == 0
    n_pad = n_all + 3 * LM_PAD
    assert (n_pad - 16) % LM_DBL_ROWS == 0
    dw = jnp.zeros((CONV_K + 1, W_CONV), F32).at[:CONV_K].set(conv_dw)
    eye = jnp.eye(len(POOL_WINDOWS), dtype=F32)
    pwb = jnp.einsum('gcd,gh->gchd', pool_w, eye).reshape(W_POOL, W_POOL).astype(BF16)
    row = lambda v: v.reshape(1, -1).astype(F32)
    params = [dw, row(conv_dw_b), row(ln_g), row(ln_b), conv_pw.astype(BF16), pwb, row(pool_scale)]

    def full(arr):
        return pl.BlockSpec(arr.shape, lambda b: (0,) * arr.ndim)
    seq = lambda w: pl.BlockSpec((None, n_all, w), lambda b: (b, 0, 0))
    out = jax.ShapeDtypeStruct((bsz, n_all, W_CONV), BF16)
    return pl.pallas_call(
        functools.partial(_local_mixers_kernel, n_ctx=n_ctx),
        out_shape=(out, out),
        grid=(bsz,),
        in_specs=[seq(2 * W_CONV), seq(W_POOL)] + [full(p) for p in params],
        out_specs=(seq(W_CONV), seq(W_POOL)),
        scratch_shapes=[pltpu.VMEM((n_pad, W_CONV), F32) for _ in range(5)]
        + [pltpu.VMEM((SUBLANE, LM_ROWS + 2 * LM_PAD - SUBLANE, W_CONV), F32)],
        compiler_params=pltpu.CompilerParams(dimension_semantics=("arbitrary",)),
        name="local_mixers",
    )(conv_in, pool_in, *params)


S5_NS = SSM_GROUPS * SSM_STATE
S5_TC = 128
S5_BG = 8
S5_LANES = 512


def _s5_scan_kernel(u_ref, perm_ref, a_ref, bd_ref, cd_ref, *rest, reverse, final):
    if final:
        yp_ref, permt_ref, d_ref, gw_ref, gb_ref, y_ref, bu_s, hh_s, h_s = rest
    else:
        y_ref, bu_s, hh_s, h_s = rest
    tc = u_ref.shape[1]
    rows = S5_BG * tc
    j = pl.program_id(1)

    @pl.when(j == 0)
    def _():
        h_s[...] = jnp.zeros_like(h_s)

    u = u_ref[...].reshape(rows, W_SSM)
    u_hi = u.astype(BF16)
    if final:
        u_lo = (u - u_hi.astype(F32)).astype(BF16)
        u_tm = jnp.dot(perm_ref[...], jnp.concatenate([u_hi, u_lo], axis=1), preferred_element_type=F32)
        u_hi_tm = u_tm[:, :W_SSM].astype(BF16)
        u_f32_tm = u_tm[:, :W_SSM] + u_tm[:, W_SSM:]
    else:
        u_hi_tm = jnp.dot(perm_ref[...], u_hi, preferred_element_type=F32).astype(BF16)
    bu_s[...] = jnp.dot(u_hi_tm, bd_ref[...], preferred_element_type=F32)

    for hf in range(S5_NS // S5_LANES):
        re_cols = pl.ds(hf * S5_LANES, S5_LANES)
        im_cols = pl.ds(S5_NS + hf * S5_LANES, S5_LANES)
        a_re = jnp.broadcast_to(a_ref[0:1, hf * S5_LANES:(hf + 1) * S5_LANES], (S5_BG, S5_LANES))
        a_im = jnp.broadcast_to(a_ref[1:2, hf * S5_LANES:(hf + 1) * S5_LANES], (S5_BG, S5_LANES))

        def step(i, carry):
            h_re, h_im = carry
            t = (tc - 1 - i) if reverse else i
            r8 = pl.ds(pl.multiple_of(t * S5_BG, S5_BG), S5_BG)
            n_re = a_re * h_re - a_im * h_im + bu_s[r8, re_cols]
            n_im = a_re * h_im + a_im * h_re + bu_s[r8, im_cols]
            hh_s[r8, re_cols] = n_re
            hh_s[r8, im_cols] = n_im
            return n_re, n_im

        h_re, h_im = lax.fori_loop(0, tc, step, (h_s[:, re_cols], h_s[:, im_cols]), unroll=8)
        h_s[:, re_cols] = h_re
        h_s[:, im_cols] = h_im

    y = jnp.dot(hh_s[...].astype(BF16), cd_ref[...], preferred_element_type=F32)
    if final:
        y = y + yp_ref[...] + d_ref[...] * u_f32_tm
        z = jax.nn.gelu(y)
        y = z * jax.nn.sigmoid(jnp.dot(z.astype(BF16), gw_ref[...], preferred_element_type=F32) + gb_ref[...])
        y = jnp.dot(permt_ref[...], y.astype(BF16), preferred_element_type=F32)
        y_ref[...] = y.astype(BF16).reshape(S5_BG, tc, W_SSM)
    else:
        y_ref[...] = y


def s5_direction(u_all, perm, a, bd, cd, n_ctx, reverse, final_args=None):
    bsz, n_all, _ = u_all.shape
    tc = S5_TC
    n_chunks = n_all // tc
    ctx_chunks = n_ctx // tc
    rows = S5_BG * tc
    assert n_all % tc == 0 and n_ctx % tc == 0 and bsz % S5_BG == 0

    if reverse:
        def chunk(j):
            return jnp.where(j < ctx_chunks, ctx_chunks - 1 - j, n_chunks - 1 + ctx_chunks - j)
    else:
        def chunk(j):
            return j
    seq_spec = pl.BlockSpec((S5_BG, tc, W_SSM), lambda g, j: (g, chunk(j), 0))
    part_spec = pl.BlockSpec((None, None, rows, W_SSM), lambda g, j: (g, chunk(j), 0, 0))

    def full(arr):
        return pl.BlockSpec(arr.shape, lambda g, j: (0,) * arr.ndim)
    in_specs = [seq_spec, full(perm), full(a), full(bd), full(cd)]
    args = [u_all, perm, a, bd, cd]
    if final_args is not None:
        y_prev, d, glu_w, glu_b = final_args
        perm_t = perm.T
        in_specs += [part_spec, full(perm_t), full(d), full(glu_w), full(glu_b)]
        args += [y_prev, perm_t, d, glu_w, glu_b]
        out_shape = jax.ShapeDtypeStruct(u_all.shape, BF16)
        out_spec = seq_spec
    else:
        out_shape = jax.ShapeDtypeStruct((bsz // S5_BG, n_chunks, rows, W_SSM), F32)
        out_spec = part_spec
    return pl.pallas_call(
        functools.partial(_s5_scan_kernel, reverse=reverse, final=final_args is not None),
        out_shape=out_shape,
        grid=(bsz // S5_BG, n_chunks),
        in_specs=in_specs,
        out_specs=out_spec,
        scratch_shapes=[pltpu.VMEM((rows, 2 * S5_NS), F32), pltpu.VMEM((rows, 2 * S5_NS), F32),
                        pltpu.VMEM((S5_BG, 2 * S5_NS), F32)],
        compiler_params=pltpu.CompilerParams(dimension_semantics=("arbitrary", "arbitrary")),
        name="s5_scan_bwd" if reverse else "s5_scan_fwd",
    )(*args)


def s5_time_major_perm(tc):
    r = jnp.arange(S5_BG * tc)
    src = (r % S5_BG) * tc + r // S5_BG
    return (src[:, None] == r[None, :]).astype(BF16)


def s5_params(a_re, a_im, log_dt, b_re, b_im, c_re, c_im):
    a_re = jnp.minimum(a_re.astype(F32), -1e-4)
    a_im = a_im.astype(F32)
    dt = jnp.exp(log_dt.astype(F32))[:, None]
    mag = jnp.exp(a_re * dt)
    ab_re = mag * jnp.cos(a_im * dt)
    ab_im = mag * jnp.sin(a_im * dt)
    den = a_re * a_re + a_im * a_im
    f_re = ((ab_re - 1.0) * a_re + ab_im * a_im) / den
    f_im = (ab_im * a_re - (ab_re - 1.0) * a_im) / den
    bb_re = f_re[..., None] * b_re - f_im[..., None] * b_im
    bb_im = f_re[..., None] * b_im + f_im[..., None] * b_re
    eye = jnp.eye(SSM_GROUPS, dtype=F32)

    def in_map(bb):
        return jnp.einsum('gpc,gh->gchp', bb, eye).reshape(W_SSM, S5_NS)

    def out_map(cc):
        return jnp.einsum('gcp,gh->gphc', cc, eye).reshape(S5_NS, W_SSM)
    a = jnp.stack([ab_re.reshape(-1), ab_im.reshape(-1)])
    bd = jnp.concatenate([in_map(bb_re), in_map(bb_im)], axis=1).astype(BF16)
    cd = jnp.concatenate([out_map(c_re.astype(F32)), -out_map(c_im.astype(F32))], axis=0).astype(BF16)
    return a, bd, cd


def s5_mixer_pallas(u_all, n_ctx, a_re, a_im, log_dt, b_re, b_im, c_re, c_im, d, glu_w, glu_b):
    pf = s5_params(a_re[0], a_im[0], log_dt[0], b_re[0], b_im[0], c_re[0], c_im[0])
    pb = s5_params(a_re[1], a_im[1], log_dt[1], b_re[1], b_im[1], c_re[1], c_im[1])
    perm = s5_time_major_perm(S5_TC)
    y_f = s5_direction(u_all, perm, *pf, n_ctx, False)
    return s5_direction(u_all, perm, *pb, n_ctx, True,
                        (y_f, d.reshape(1, W_SSM).astype(F32), glu_w.astype(BF16), glu_b.reshape(1, W_SSM)))


HEAD_W = LANE
ATT_PREP_ROWS = 256
ATT_Q_ROWS = 256


def _head_dim_order():
    half = ROPE_AXIS // 2
    r = MLA_NOPE
    first = list(range(r, r + half)) + list(range(r + 2 * half, r + 3 * half))
    second = list(range(r + half, r + 2 * half)) + list(range(r + 3 * half, r + 4 * half))
    split = LANE // 2 - MLA_ROPE // 2
    return np.array(first + list(range(split)) + second + list(range(split, MLA_NOPE)))


def _to_head_lanes(w):
    order = _head_dim_order()
    pad = [(0, 0)] * (w.ndim - 1) + [(0, HEAD_W - len(order))]
    return jnp.pad(jnp.take(w, order, axis=-1), pad)


def mla_layout_params(q_a_g, wq_b, kv_a_g, wkv_b, q_g, k_g):
    wq = _to_head_lanes(wq_b.reshape(MLA_Q_RANK, MLA_HEADS, MLA_QK))
    wq_p = jnp.pad(wq, ((0, QKP_W - MLA_Q_RANK), (0, 0), (0, 0))).reshape(QKP_W, MLA_HEADS * HEAD_W).astype(BF16)
    qa_g = jnp.pad(q_a_g, (0, QKP_W - MLA_Q_RANK)).reshape(1, QKP_W)
    wkv = wkv_b.reshape(MLA_KV_RANK, MLA_HEADS, MLA_NOPE + MLA_V)
    wk = jnp.pad(wkv[:, :, :MLA_NOPE], ((0, 0), (0, 0), (0, MLA_ROPE)))
    wkv_p = jnp.concatenate([_to_head_lanes(wk).reshape(MLA_KV_RANK, -1),
                             wkv[:, :, MLA_NOPE:].reshape(MLA_KV_RANK, -1)], axis=1).astype(BF16)
    order = _head_dim_order()
    place = np.zeros((QKP_W, MLA_HEADS, HEAD_W), np.float32)
    for lane_i, dim in enumerate(order):
        if dim >= MLA_NOPE:
            place[MLA_Q_RANK + dim - MLA_NOPE, :, lane_i] = 1.0
    place = jnp.asarray(place.reshape(QKP_W, MLA_HEADS * HEAD_W), BF16)
    return (wq_p, qa_g, wkv_p, kv_a_g.reshape(1, MLA_KV_RANK), place,
            _to_head_lanes(q_g.reshape(1, MLA_QK)), _to_head_lanes(k_g.reshape(1, MLA_QK)))


def rope_lane_tables(n_ctx, n_lat):
    t = jnp.arange(n_lat)
    row = (t // GRID_W).astype(F32)
    col = (t % GRID_W).astype(F32)
    inv = ROPE_BASE ** (-jnp.arange(0, ROPE_AXIS, 2, dtype=F32) / ROPE_AXIS)
    ang = jnp.concatenate([row[:, None] * inv, col[:, None] * inv], axis=1)
    cos, sin = jnp.cos(ang), jnp.sin(ang)
    half = MLA_ROPE // 2
    one = jnp.ones((n_lat, LANE // 2 - half), F32)
    zero = jnp.zeros((n_lat, LANE // 2 - half), F32)
    cmul = jnp.concatenate([cos, one, cos, one], axis=1)
    smul = jnp.concatenate([-sin, zero, sin, zero], axis=1)
    ctx = [jnp.ones((n_ctx, HEAD_W), F32), jnp.zeros((n_ctx, HEAD_W), F32)]
    return jnp.stack([jnp.concatenate([c, x], axis=0) for c, x in zip(ctx, (cmul, smul))])


def _mla_kernel(ckv_ref, qkp_ref, rope_ref, wq_ref, qag_ref, wkv_ref, kvag_ref, place_ref, qg_ref, kg_ref,
                o_ref, q_s, k_s, v_s, *, n_ctx, ctx_queries):
    n_all = ckv_ref.shape[0]

    def rope(x, r0):
        rows = pl.ds(r0, ATT_PREP_ROWS)
        swapped = pltpu.roll(x, HEAD_W // 2, axis=1)
        return x * rope_ref[0, rows, :] + swapped * rope_ref[1, rows, :]

    def head_norm(x, g):
        ss = jnp.sum(x * x, axis=-1, keepdims=True) * (1.0 / MLA_QK)
        return x * lax.rsqrt(ss + EPS) * g

    def prep(i, _):
        r0 = pl.multiple_of(i * ATT_PREP_ROWS, ATT_PREP_ROWS)
        rows = pl.ds(r0, ATT_PREP_ROWS)
        ckv = ckv_ref[rows, :]
        kvn = ckv * lax.rsqrt(jnp.mean(ckv * ckv, axis=-1, keepdims=True) + EPS) * kvag_ref[...]
        kv = jnp.dot(kvn.astype(BF16), wkv_ref[...], preferred_element_type=F32)
        qkp = qkp_ref[rows, :]
        p_hi = qkp.astype(BF16)
        p_lo = (qkp - p_hi.astype(F32)).astype(BF16)
        kpe = (jnp.dot(p_hi, place_ref[...], preferred_element_type=F32)
               + jnp.dot(p_lo, place_ref[...], preferred_element_type=F32))
        lane = lax.broadcasted_iota(jnp.int32, qkp.shape, 1)
        qsq = jnp.where(lane < MLA_Q_RANK, qkp * qkp, 0.0)
        qn = qkp * lax.rsqrt(jnp.sum(qsq, axis=-1, keepdims=True) * (1.0 / MLA_Q_RANK) + EPS) * qag_ref[...]
        q = jnp.dot(qn.astype(BF16), wq_ref[...], preferred_element_type=F32)
        for h in range(MLA_HEADS):
            blk = slice(h * HEAD_W, (h + 1) * HEAD_W)
            k_h = rope(head_norm(kv[:, blk] + kpe[:, blk], kg_ref[...]), r0)
            q_h = rope(head_norm(q[:, blk], qg_ref[...]), r0) * (MLA_QK ** -0.5)
            k_s[h, rows, :] = k_h.astype(BF16)
            q_s[h, rows, :] = q_h.astype(BF16)
        v_s[rows, :] = kv[:, MLA_HEADS * HEAD_W:].astype(BF16)
        return 0

    lax.fori_loop(0, n_all // ATT_PREP_ROWS, prep, 0)

    lane = lax.broadcasted_iota(jnp.int32, (ATT_Q_ROWS, LANE), 1)

    def attend_rows(r0, n_keys):
        outs = []
        for pair in range(MLA_HEADS // 2):
            v2 = v_s[pl.ds(0, n_keys), pair * LANE:(pair + 1) * LANE]
            o2 = []
            for h in (2 * pair, 2 * pair + 1):
                q = q_s[h, pl.ds(r0, ATT_Q_ROWS), :]
                k = k_s[h, pl.ds(0, n_keys), :]
                s = lax.dot_general(q, k, (((1,), (1,)), ((), ())), preferred_element_type=F32)
                p = jnp.exp(s - jnp.max(s, axis=-1, keepdims=True))
                den = jnp.sum(p, axis=-1, keepdims=True)
                o2.append(jnp.dot(p.astype(BF16), v2, preferred_element_type=F32) / den)
            outs.append(jnp.where(lane < MLA_V, o2[0], o2[1]))
        return jnp.concatenate(outs, axis=1)

    for i in range(n_ctx // ATT_Q_ROWS):
        r0 = i * ATT_Q_ROWS
        if ctx_queries:
            o_ref[pl.ds(r0, ATT_Q_ROWS), :] = attend_rows(r0, n_ctx).astype(o_ref.dtype)
        else:
            o_ref[pl.ds(r0, ATT_Q_ROWS), :] = jnp.zeros((ATT_Q_ROWS, W_MLA), o_ref.dtype)

    def q_tile(i, _):
        r0 = pl.multiple_of(n_ctx + i * ATT_Q_ROWS, ATT_Q_ROWS)
        o_ref[pl.ds(r0, ATT_Q_ROWS), :] = attend_rows(r0, n_all).astype(o_ref.dtype)
        return 0

    lax.fori_loop(0, (n_all - n_ctx) // ATT_Q_ROWS, q_tile, 0, unroll=2)


def mla_attention(ckv, qkp, rope_tab, params, n_ctx, ctx_queries):
    bsz, n_all, _ = ckv.shape
    assert n_all % ATT_PREP_ROWS == 0 and n_ctx % ATT_Q_ROWS == 0 and (n_all - n_ctx) % ATT_Q_ROWS == 0

    def full(arr):
        return pl.BlockSpec(arr.shape, lambda b: (0,) * arr.ndim)
    return pl.pallas_call(
        functools.partial(_mla_kernel, n_ctx=n_ctx, ctx_queries=ctx_queries),
        out_shape=jax.ShapeDtypeStruct((bsz, n_all, W_MLA), BF16),
        grid=(bsz,),
        in_specs=[pl.BlockSpec((None, n_all, MLA_KV_RANK), lambda b: (b, 0, 0)),
                  pl.BlockSpec((None, n_all, QKP_W), lambda b: (b, 0, 0)),
                  full(rope_tab)] + [full(p) for p in params],
        out_specs=pl.BlockSpec((None, n_all, W_MLA), lambda b: (b, 0, 0)),
        scratch_shapes=[pltpu.VMEM((MLA_HEADS, n_all, HEAD_W), BF16), pltpu.VMEM((MLA_HEADS, n_all, HEAD_W), BF16),
                        pltpu.VMEM((n_all, MLA_HEADS * MLA_V), BF16)],
        compiler_params=pltpu.CompilerParams(dimension_semantics=("arbitrary",)),
        name="mla_attention",
    )(ckv, qkp, rope_tab, *params)


RUN_ROWS = 1280
RUN_BITS = (256, 128, 64, 32, 16, 8)
PACK_W = D_MODEL // 2
RT_EXPERT, RT_WEIGHT, RT_SLOT = 0, TOP_K, 2 * TOP_K
RT_ROWS = 16
META_SRC, META_CNT, META_BASE = 0, 1, 2
I32 = jnp.int32
U32 = jnp.uint32


def _mix_out_kernel(ya_ref, yp_ref, ys_ref, ym_ref, x_ref, g1_ref, sh_ref, sc_ref, ng_ref, wo_ref, rwt_ref, rbt_ref,
                    tri_ref, upper_ref, lower_ref, xo_ref, h_ref, rt_ref, meta_ref, tot_ref, carry_s):
    first = jnp.logical_and(pl.program_id(0) == 0, pl.program_id(1) == 0)

    @pl.when(first)
    def _():
        carry_s[...] = jnp.zeros_like(carry_s)

    acc = jnp.dot(ya_ref[...], wo_ref[0:W_CONV, :], preferred_element_type=F32)
    acc += jnp.dot(yp_ref[...], wo_ref[W_CONV:W_CONV + W_POOL, :], preferred_element_type=F32)
    acc += jnp.dot(ys_ref[...], wo_ref[W_CONV + W_POOL:D_MIX - W_MLA, :], preferred_element_type=F32)
    acc += jnp.dot(ym_ref[...], wo_ref[D_MIX - W_MLA:, :], preferred_element_type=F32)
    x = x_ref[...] + g1_ref[...] * acc
    xo_ref[...] = x
    y = x * lax.rsqrt(jnp.mean(x * x, axis=-1, keepdims=True) + EPS) * ng_ref[...]
    h = (y * (1.0 + sc_ref[...]) + sh_ref[...]).astype(BF16)
    h_ref[...] = h
    logits = lax.dot_general(rwt_ref[...], h, (((1,), (1,)), ((), ())), preferred_element_type=F32) + rbt_ref[...]
    eid = lax.broadcasted_iota(I32, logits.shape, 0)
    onehot = jnp.zeros(logits.shape, F32)
    vals, hits, rows = [], [], []
    for k in range(TOP_K):
        m = jnp.max(logits, axis=0, keepdims=True)
        idx = jnp.min(jnp.where(logits == m, eid, N_EXPERTS), axis=0, keepdims=True)
        hit = eid == idx
        logits = jnp.where(hit, -jnp.inf, logits)
        onehot = jnp.where(hit, 1.0, onehot)
        vals.append(m)
        hits.append(hit)
        rows.append(idx.astype(F32))
    ex = [jnp.exp(v - vals[0]) for v in vals]
    den = ex[0] + ex[1] + ex[2] + ex[3]
    rows += [e / den for e in ex]
    oh = onehot.astype(BF16)
    before = jnp.dot(oh, tri_ref[...], preferred_element_type=F32)
    cnt_col = jnp.sum(onehot, axis=1, keepdims=True)
    cnt8_col = jnp.floor((cnt_col + (SUBLANE - 1)) * (1.0 / SUBLANE)) * SUBLANE
    src_col = jnp.dot(lower_ref[...], jnp.broadcast_to(cnt8_col, (N_EXPERTS, LANE)).astype(BF16),
                      preferred_element_type=F32)[:, 0:1]
    slot_of = before + src_col
    for k in range(TOP_K):
        rows.append(jnp.sum(jnp.where(hits[k], slot_of, 0.0), axis=0, keepdims=True))
    rows += [jnp.zeros_like(rows[0])] * (RT_ROWS - len(rows))
    rt_ref[...] = jnp.concatenate(rows, axis=0)
    ones = jnp.ones((SUBLANE, TOK_ROWS), BF16)
    cnt = lax.dot_general(ones, oh, (((1,), (1,)), ((), ())), preferred_element_type=F32)
    cnt = jnp.concatenate([cnt, jnp.zeros((SUBLANE, LANE - N_EXPERTS), F32)], axis=1)
    cnt8 = jnp.floor((cnt + (SUBLANE - 1)) * (1.0 / SUBLANE)) * SUBLANE
    src = jnp.dot(cnt8.astype(BF16), upper_ref[...], preferred_element_type=F32)
    base = carry_s[...]
    row = lax.broadcasted_iota(I32, (SUBLANE, LANE), 0)
    meta = jnp.where(row == META_SRC, src, jnp.where(row == META_CNT, cnt8, jnp.where(row == META_BASE, base, 0.0)))
    meta_ref[...] = meta.astype(I32)
    carry_s[...] = base + cnt8
    tot_ref[...] = (base + cnt8).astype(I32)


def mix_out(ya, yp, ys, ym, xc, mod3, norm_g, w_out, router_w, router_b, n_ctx, skip_ctx):
    bsz, n_all, dim = xc.shape
    nct = n_ctx // TOK_ROWS
    off = nct if skip_ctx else 0
    n_out = n_all - off * TOK_ROWS
    tiles = n_out // TOK_ROWS
    tile_in = lambda w: pl.BlockSpec((None, TOK_ROWS, w), lambda b, i: (b, i + off, 0))
    tile_out = lambda w: pl.BlockSpec((None, TOK_ROWS, w), lambda b, i: (b, i, 0))

    def mod_spec(chunk):
        return pl.BlockSpec((None, 1, dim), lambda b, i: (jnp.where(i + off < nct, bsz, b), 0, chunk))
    rwt = router_w.T.astype(BF16)
    rbt = router_b.reshape(N_EXPERTS, 1).astype(F32)
    r = jnp.arange(TOK_ROWS)
    tri = (r[:, None] < r[None, :]).astype(BF16)
    q = jnp.arange(LANE)
    upper = (q[:, None] < q[None, :]).astype(BF16)
    e = jnp.arange(N_EXPERTS)
    lower = (e[None, :] < e[:, None]).astype(BF16)
    full = lambda a: pl.BlockSpec(a.shape, lambda b, i: (0,) * a.ndim)
    wo = w_out.astype(BF16)
    ng = norm_g.reshape(1, dim)
    return pl.pallas_call(
        _mix_out_kernel,
        out_shape=(jax.ShapeDtypeStruct((bsz, n_out, dim), F32), jax.ShapeDtypeStruct((bsz, n_out, dim), BF16),
                   jax.ShapeDtypeStruct((bsz * tiles, RT_ROWS, TOK_ROWS), F32),
                   jax.ShapeDtypeStruct((bsz * tiles, SUBLANE, LANE), I32),
                   jax.ShapeDtypeStruct((SUBLANE, LANE), I32)),
        grid=(bsz, tiles),
        in_specs=[tile_in(W_CONV), tile_in(W_POOL), tile_in(W_SSM), tile_in(W_MLA), tile_in(dim),
                  mod_spec(2), mod_spec(3), mod_spec(4), full(ng), full(wo), full(rwt), full(rbt), full(tri),
                  full(upper), full(lower)],
        out_specs=(tile_out(dim), tile_out(dim),
                   pl.BlockSpec((None, RT_ROWS, TOK_ROWS), lambda b, i: (b * tiles + i, 0, 0)),
                   pl.BlockSpec((None, SUBLANE, LANE), lambda b, i: (b * tiles + i, 0, 0)),
                   pl.BlockSpec((SUBLANE, LANE), lambda b, i: (0, 0))),
        scratch_shapes=[pltpu.VMEM((SUBLANE, LANE), F32)],
        compiler_params=pltpu.CompilerParams(dimension_semantics=("arbitrary", "arbitrary")),
        name="mix_out",
    )(ya, yp, ys, ym, xc, mod3, mod3, mod3, ng, wo, rwt, rbt, tri, upper, lower)


def expert_layout(meta, tot, n_tok):
    n_tiles = meta.shape[0]
    tot8 = tot[0, :N_EXPERTS]
    region = (tot8 + MOE_BLOCK - 1) // MOE_BLOCK * MOE_BLOCK
    g_end = jnp.cumsum(region)
    g_start = g_end - region
    n_blocks = -(-(n_tok * TOP_K + n_tiles * N_EXPERTS * (SUBLANE - 1)) // MOE_BLOCK) + N_EXPERTS
    block_start = jnp.arange(n_blocks) * MOE_BLOCK
    block_e = jnp.minimum(jnp.sum(g_end[None, :] <= block_start[:, None], axis=1), N_EXPERTS - 1).astype(I32)
    n_used = (g_end[-1] // MOE_BLOCK).astype(I32).reshape(1)
    g_start_l = jnp.zeros((LANE,), I32).at[:N_EXPERTS].set(g_start.astype(I32))
    total = jnp.sum(meta[:, META_CNT, :N_EXPERTS], axis=1)
    cnt = meta[:, META_CNT].at[:, N_EXPERTS].set(total)
    runs = jnp.stack([meta[:, META_SRC], cnt, meta[:, META_BASE] + g_start_l], axis=1)
    tails = jnp.zeros((2, LANE), I32).at[0, :N_EXPERTS].set((g_start + tot8).astype(I32))
    tails = tails.at[1, :N_EXPERTS].set((region - tot8).astype(I32))
    return runs, tails, block_e, n_used, n_blocks


TILE_BITS = (1024, 512, 256, 128, 64, 32, 16, 8)


def _start_run_copies(runs_ref, local, remote, sem, to_remote):
    def per_expert(e, c):
        s = runs_ref[0, 0, e]
        n = runs_ref[0, 1, e]
        d = runs_ref[0, 2, e]
        for bit in RUN_BITS:
            take = (n & bit) != 0
            loc = local.at[pl.ds(pl.multiple_of(s, SUBLANE), bit), :]
            rem = remote.at[pl.ds(pl.multiple_of(d, SUBLANE), bit), :]
            cp = pltpu.make_async_copy(loc, rem, sem) if to_remote else pltpu.make_async_copy(rem, loc, sem)

            @pl.when(take)
            def _():
                cp.start()
            s = s + jnp.where(take, bit, 0)
            d = d + jnp.where(take, bit, 0)
        return c
    lax.fori_loop(0, N_EXPERTS, per_expert, 0)


def _wait_run_copies(runs_ref, local, remote, sem, to_remote):
    total = runs_ref[0, 1, N_EXPERTS]
    for bit in TILE_BITS:
        loc = local.at[pl.ds(0, bit), :]
        rem = remote.at[pl.ds(0, bit), :]
        cp = pltpu.make_async_copy(loc, rem, sem) if to_remote else pltpu.make_async_copy(rem, loc, sem)

        @pl.when((total & bit) != 0)
        def _():
            cp.wait()


def _dispatch_kernel(runs_ref, runs_prev_ref, tails_ref, h_ref, slot_ref, xs_ref, buf, zbuf, sems):
    i = pl.program_id(0)
    cur = i % 2

    @pl.when(i == 0)
    def _():
        zbuf[...] = jnp.zeros_like(zbuf)

        def tail_copies(wait):
            def per_expert(e, c):
                d = tails_ref[0, e]
                n = tails_ref[1, e]
                for bit in RUN_BITS:
                    take = (n & bit) != 0
                    cp = pltpu.make_async_copy(zbuf.at[pl.ds(0, bit), :],
                                               xs_ref.at[pl.ds(pl.multiple_of(d, SUBLANE), bit), :], sems.at[2])

                    @pl.when(take)
                    def _():
                        if wait:
                            cp.wait()
                        else:
                            cp.start()
                    d = d + jnp.where(take, bit, 0)
                return c
            lax.fori_loop(0, N_EXPERTS, per_expert, 0)
        tail_copies(False)
        tail_copies(True)

    j = lax.broadcasted_iota(I32, (RUN_ROWS, TOK_ROWS), 0)
    hit = j == slot_ref[0:1, :]
    for k in range(1, TOP_K):
        hit = jnp.logical_or(hit, j == slot_ref[k:k + 1, :])
    p = jnp.where(hit, 1.0, 0.0).astype(BF16)
    lo = jnp.dot(p, h_ref[:, :PACK_W], preferred_element_type=F32)
    hi = jnp.dot(p, h_ref[:, PACK_W:], preferred_element_type=F32)
    lo_bits = lax.shift_right_logical(pltpu.bitcast(lo, U32), jnp.uint32(16))
    hi_bits = pltpu.bitcast(hi, U32) & jnp.uint32(0xFFFF0000)
    buf[cur] = hi_bits | lo_bits
    _start_run_copies(runs_ref, buf.at[cur], xs_ref, sems.at[cur], True)

    @pl.when(i > 0)
    def _():
        _wait_run_copies(runs_prev_ref, buf.at[1 - cur], xs_ref, sems.at[1 - cur], True)

    @pl.when(i == pl.num_programs(0) - 1)
    def _():
        _wait_run_copies(runs_ref, buf.at[cur], xs_ref, sems.at[cur], True)


def dispatch(h, slots_t, runs, tails, n_blocks):
    n_tok, dim = h.shape
    n_tiles = n_tok // TOK_ROWS
    return pl.pallas_call(
        _dispatch_kernel,
        out_shape=jax.ShapeDtypeStruct((n_blocks * MOE_BLOCK, PACK_W), U32),
        grid=(n_tiles,),
        in_specs=[pl.BlockSpec((1, 3, LANE), lambda i: (i, 0, 0), memory_space=pltpu.SMEM),
                  pl.BlockSpec((1, 3, LANE), lambda i: (jnp.maximum(i - 1, 0), 0, 0), memory_space=pltpu.SMEM),
                  pl.BlockSpec((2, LANE), lambda i: (0, 0), memory_space=pltpu.SMEM),
                  pl.BlockSpec((TOK_ROWS, dim), lambda i: (i, 0)),
                  pl.BlockSpec((None, SUBLANE, TOK_ROWS), lambda i: (i, 0, 0))],
        out_specs=pl.BlockSpec(memory_space=pl.ANY),
        scratch_shapes=[pltpu.VMEM((2, RUN_ROWS, PACK_W), U32), pltpu.VMEM((RUN_BITS[0], PACK_W), U32),
                        pltpu.SemaphoreType.DMA((3,))],
        compiler_params=pltpu.CompilerParams(dimension_semantics=("arbitrary",)),
        name="moe_dispatch",
    )(runs, runs, tails, h, slots_t)


def _expert_block_kernel(be_ref, nu_ref, x_ref, win_ref, bin_ref, wout_ref, bout_ref, o_ref, win_s, wout_s):
    i = pl.program_id(0)
    used = i < nu_ref[0]
    new_expert = jnp.logical_or(i == 0, be_ref[i] != be_ref[jnp.maximum(i - 1, 0)])

    @pl.when(jnp.logical_and(used, new_expert))
    def _():
        win_s[...] = win_ref[...].astype(BF16)
        wout_s[...] = wout_ref[...].astype(BF16)

    @pl.when(used)
    def _():
        w = x_ref[...]
        x_lo = pltpu.bitcast(lax.shift_left(w, jnp.uint32(16)), F32).astype(BF16)
        x_hi = pltpu.bitcast(w & jnp.uint32(0xFFFF0000), F32).astype(BF16)
        gu = (jnp.dot(x_lo, win_s[:PACK_W, :], preferred_element_type=F32)
              + jnp.dot(x_hi, win_s[PACK_W:, :], preferred_element_type=F32) + bin_ref[...])
        gate = jnp.minimum(gu[:, :D_EXPERT], SWIGLU_LIMIT)
        up = jnp.clip(gu[:, D_EXPERT:], -SWIGLU_LIMIT, SWIGLU_LIMIT)
        act = (up + 1.0) * gate * jax.nn.sigmoid(SWIGLU_ALPHA * gate)
        o_ref[...] = jnp.dot(act.astype(BF16), wout_s[...], preferred_element_type=F32) + bout_ref[...]

    @pl.when(jnp.logical_not(used))
    def _():
        o_ref[...] = jnp.zeros_like(o_ref)


def expert_blocks(xs, block_e, n_used, layer, w_in, b_in, w_out, b_out):
    n_pad = xs.shape[0]
    dim = w_out.shape[-1]
    n_blocks = n_pad // MOE_BLOCK

    def row_map(i, be, nu):
        return (jnp.minimum(i, nu[0] - 1), 0)

    def exp_map(i, be, nu):
        return (layer, be[i], 0, 0)

    return pl.pallas_call(
        _expert_block_kernel,
        out_shape=jax.ShapeDtypeStruct((n_pad, dim), F32),
        grid_spec=pltpu.PrefetchScalarGridSpec(
            num_scalar_prefetch=2,
            grid=(n_blocks,),
            in_specs=[
                pl.BlockSpec((MOE_BLOCK, PACK_W), row_map),
                pl.BlockSpec((None, None, dim, 2 * D_EXPERT), exp_map),
                pl.BlockSpec((None, None, 1, 2 * D_EXPERT), exp_map),
                pl.BlockSpec((None, None, D_EXPERT, dim), exp_map),
                pl.BlockSpec((None, None, 1, dim), exp_map),
            ],
            out_specs=pl.BlockSpec((MOE_BLOCK, dim), lambda i, be, nu: (i, 0)),
            scratch_shapes=[pltpu.VMEM((dim, 2 * D_EXPERT), BF16), pltpu.VMEM((D_EXPERT, dim), BF16)],
        ),
        compiler_params=pltpu.CompilerParams(dimension_semantics=("arbitrary",)),
        name="expert_blocks",
    )(block_e, n_used, xs, w_in, b_in[:, :, None, :], w_out, b_out[:, :, None, :])


def _combine_kernel(runs_ref, runs_next_ref, ys_ref, rt_ref, x_ref, g_ref, o_ref, ybuf, sems):
    i = pl.program_id(0)
    cur = i % 2

    @pl.when(i == 0)
    def _():
        ybuf[...] = jnp.zeros_like(ybuf)
        _start_run_copies(runs_ref, ybuf.at[0], ys_ref, sems.at[0], False)

    @pl.when(i + 1 < pl.num_programs(0))
    def _():
        _start_run_copies(runs_next_ref, ybuf.at[1 - cur], ys_ref, sems.at[1 - cur], False)

    _wait_run_copies(runs_ref, ybuf.at[cur], ys_ref, sems.at[cur], False)
    rt = rt_ref[...]
    j = lax.broadcasted_iota(I32, (TOK_ROWS, RUN_ROWS), 1).astype(F32)
    wmat = jnp.zeros((TOK_ROWS, RUN_ROWS), F32)
    for k in range(TOP_K):
        wmat = jnp.where(j == rt[:, RT_SLOT + k:RT_SLOT + k + 1], rt[:, RT_WEIGHT + k:RT_WEIGHT + k + 1], wmat)
    w_hi = wmat.astype(BF16)
    w_lo = (wmat - w_hi.astype(F32)).astype(BF16)
    y = ybuf[cur].astype(BF16)
    m = jnp.dot(w_hi, y, preferred_element_type=F32) + jnp.dot(w_lo, y, preferred_element_type=F32)
    o_ref[...] = x_ref[...] + g_ref[...] * m


def combine(ys, route, runs, xn, mod3, n_ctx_tiles):
    bsz, n_rows, dim = xn.shape
    tiles = n_rows // TOK_ROWS
    tile = lambda w: pl.BlockSpec((None, TOK_ROWS, w), lambda i: (i // tiles, i % tiles, 0))
    g_spec = pl.BlockSpec((None, 1, dim), lambda i: (jnp.where(i % tiles < n_ctx_tiles, bsz, i // tiles), 0, 5))
    return pl.pallas_call(
        _combine_kernel,
        out_shape=jax.ShapeDtypeStruct(xn.shape, F32),
        grid=(bsz * tiles,),
        in_specs=[pl.BlockSpec((1, 3, LANE), lambda i: (i, 0, 0), memory_space=pltpu.SMEM),
                  pl.BlockSpec((1, 3, LANE), lambda i: (jnp.minimum(i + 1, bsz * tiles - 1), 0, 0),
                               memory_space=pltpu.SMEM),
                  pl.BlockSpec(memory_space=pl.ANY), tile(RT_ROWS), tile(dim), g_spec],
        out_specs=tile(dim),
        scratch_shapes=[pltpu.VMEM((2, RUN_ROWS, dim), F32), pltpu.SemaphoreType.DMA((2,))],
        compiler_params=pltpu.CompilerParams(dimension_semantics=("arbitrary",)),
        name="moe_combine",
    )(runs, runs, ys, route, xn, mod3)


def routed_experts(xn, h2, route_t, meta, tot, mod3, n_ctx_tiles, layer, w_in, b_in, w_out, b_out):
    bsz, n_rows, dim = xn.shape
    n_tok = bsz * n_rows
    runs, tails, block_e, n_used, n_blocks = expert_layout(meta, tot, n_tok)
    slots_t = route_t[:, RT_SLOT:RT_SLOT + SUBLANE, :].astype(I32)
    route = jnp.swapaxes(route_t, 1, 2).reshape(bsz, n_rows, RT_ROWS)
    xs = dispatch(h2.reshape(n_tok, dim), slots_t, runs, tails, n_blocks)
    ys = expert_blocks(xs, block_e, n_used, layer, w_in, b_in, w_out, b_out)
    return combine(ys, route, runs, xn, mod3, n_ctx_tiles)


def kernel(x, c, ctx, c_ctx, ada_w, ada_b, norm1_g, norm2_g, w_mix_in, w_mix_out,
           conv_dw, conv_dw_b, conv_ln_g, conv_ln_b, conv_pw, pool_w, pool_scale,
           ssm_a_re, ssm_a_im, ssm_log_dt, ssm_b_re, ssm_b_im, ssm_c_re, ssm_c_im,
           ssm_d, ssm_glu_w, ssm_glu_b,
           mla_q_a_g, mla_wq_b, mla_kv_a_g, mla_wkv_b, mla_q_g, mla_k_g,
           router_w, router_b, exp_w_in, exp_b_in, exp_w_out, exp_b_out):
    bsz, n_lat, dim = x.shape
    n_ctx = ctx.shape[1]
    xc = jnp.concatenate([ctx, x], axis=1)
    c_all = jnp.zeros((MOD_ROWS, dim), F32).at[:bsz].set(c).at[bsz].set(c_ctx)
    rope_tab = rope_lane_tables(n_ctx, n_lat)
    for l in range(DEPTH):
        last = l == DEPTH - 1
        mod = modulation(c_all, ada_w[l], ada_b[l])
        mod3 = mod.reshape(MOD_ROWS, 1, 6 * dim)

        conv_in, pool_in, ssm_in, ckv, qkp = mix_in(xc, mod3, norm1_g[l], mix_in_weights(w_mix_in[l]), n_ctx)
        ya, yp = local_mixers(conv_in, pool_in, n_ctx, conv_dw[l], conv_dw_b[l], conv_ln_g[l], conv_ln_b[l],
                              conv_pw[l], pool_w[l], pool_scale[l])
        ys = s5_mixer_pallas(ssm_in, n_ctx, ssm_a_re[l], ssm_a_im[l], ssm_log_dt[l],
                             ssm_b_re[l], ssm_b_im[l], ssm_c_re[l], ssm_c_im[l],
                             ssm_d[l], ssm_glu_w[l], ssm_glu_b[l])
        ym = mla_attention(ckv, qkp, rope_tab,
                           mla_layout_params(mla_q_a_g[l], mla_wq_b[l], mla_kv_a_g[l], mla_wkv_b[l],
                                             mla_q_g[l], mla_k_g[l]), n_ctx, not last)
        xn, h2, route, meta, tot = mix_out(ya, yp, ys, ym, xc, mod3, norm2_g[l], w_mix_out[l],
                                           router_w[l], router_b[l], n_ctx, last)
        xc = routed_experts(xn, h2, route, meta, tot, mod3, 0 if last else n_ctx // TOK_ROWS,
                            l, exp_w_in, exp_b_in, exp_w_out, exp_b_out)
    return xc
```

```python
import functools

import jax
import jax.numpy as jnp
import numpy as np
from jax import lax
from jax.experimental import pallas as pl
from jax.experimental.pallas import tpu as pltpu

D_MODEL = 1024
DEPTH = 2
GRID_W = 64
EPS = 1e-6

D_MIX = D_MODEL
W_CONV = D_MIX // 4
W_POOL = D_MIX // 4
W_SSM = D_MIX // 4
W_MLA = D_MIX - W_CONV - W_POOL - W_SSM
CONV_K = 31
POOL_WINDOWS = (2, 4, 8, 16)
POOL_CH = W_POOL // len(POOL_WINDOWS)
SSM_CH = 16
SSM_GROUPS = W_SSM // SSM_CH
SSM_STATE = 64
MLA_V = 64
MLA_HEADS = W_MLA // MLA_V
MLA_NOPE = 64
MLA_ROPE = 32
MLA_QK = MLA_NOPE + MLA_ROPE
MLA_Q_RANK = 192
MLA_KV_RANK = 128
ROPE_AXIS = MLA_ROPE // 2
ROPE_BASE = 10000.0
N_EXPERTS = 32
TOP_K = 4
D_EXPERT = D_MODEL
SWIGLU_LIMIT = 7.0
SWIGLU_ALPHA = 1.702
MOE_BLOCK = 512

OFF_POOL = 2 * W_CONV
OFF_Q = OFF_POOL + W_POOL
OFF_SSM = OFF_Q + MLA_Q_RANK
OFF_KV = OFF_SSM + W_SSM
OFF_KPE = OFF_KV + MLA_KV_RANK
N_IN = OFF_KPE + MLA_ROPE

F32 = jnp.float32
BF16 = jnp.bfloat16
LANE = 128
SUBLANE = 8
QKP_W = 256


MOD_ROWS = 40
MOD_COLS = 1024


def _mod_kernel(c_ref, w_ref, b_ref, o_ref):
    c = c_ref[...]
    s = (c * jax.nn.sigmoid(c)).astype(BF16)
    o_ref[...] = jnp.dot(s, w_ref[...].astype(BF16), preferred_element_type=F32) + b_ref[...]


def modulation(c_all, ada_w, ada_b):
    n_out = ada_w.shape[1]
    return pl.pallas_call(
        _mod_kernel,
        out_shape=jax.ShapeDtypeStruct((MOD_ROWS, n_out), F32),
        grid=(n_out // MOD_COLS,),
        in_specs=[pl.BlockSpec((MOD_ROWS, D_MODEL), lambda j: (0, 0)),
                  pl.BlockSpec((D_MODEL, MOD_COLS), lambda j: (0, j)),
                  pl.BlockSpec((1, MOD_COLS), lambda j: (0, j))],
        out_specs=pl.BlockSpec((MOD_ROWS, MOD_COLS), lambda j: (0, j)),
        compiler_params=pltpu.CompilerParams(dimension_semantics=("arbitrary",)),
        name="modulation",
    )(c_all, ada_w, ada_b.reshape(1, n_out))


TOK_ROWS = 256
IN_SPLITS = (2 * W_CONV, W_POOL, W_SSM, MLA_KV_RANK, QKP_W)


def _mix_in_kernel(x_ref, sh_ref, sc_ref, g_ref, w_ref, *outs):
    x = x_ref[...]
    y = x * lax.rsqrt(jnp.mean(x * x, axis=-1, keepdims=True) + EPS) * g_ref[...]
    h = y * (1.0 + sc_ref[...]) + sh_ref[...]
    p = jnp.dot(h.astype(BF16), w_ref[...], preferred_element_type=F32)
    off = 0
    for o_ref, w in zip(outs, IN_SPLITS):
        o_ref[...] = p[:, off:off + w]
        off += w


def mix_in_weights(w_mix_in):
    cols = [w_mix_in[:, :OFF_POOL], w_mix_in[:, OFF_POOL:OFF_Q], w_mix_in[:, OFF_SSM:OFF_KV],
            w_mix_in[:, OFF_KV:OFF_KPE], w_mix_in[:, OFF_Q:OFF_SSM], w_mix_in[:, OFF_KPE:],
            jnp.zeros((D_MODEL, QKP_W - MLA_Q_RANK - MLA_ROPE), w_mix_in.dtype)]
    return jnp.concatenate(cols, axis=1).astype(BF16)


def mix_in(xc, mod3, norm_g, w_in_p, n_ctx):
    bsz, n_all, dim = xc.shape
    assert n_ctx % TOK_ROWS == 0 and n_all % TOK_ROWS == 0 and bsz < MOD_ROWS
    nct = n_ctx // TOK_ROWS
    tile = lambda w: pl.BlockSpec((None, TOK_ROWS, w), lambda b, i: (b, i, 0))

    def mod_spec(chunk):
        return pl.BlockSpec((None, 1, dim), lambda b, i: (jnp.where(i < nct, bsz, b), 0, chunk))
    return pl.pallas_call(
        _mix_in_kernel,
        out_shape=tuple(jax.ShapeDtypeStruct((bsz, n_all, w), F32) for w in IN_SPLITS),
        grid=(bsz, n_all // TOK_ROWS),
        in_specs=[tile(dim), mod_spec(0), mod_spec(1),
                  pl.BlockSpec((1, dim), lambda b, i: (0, 0)),
                  pl.BlockSpec(w_in_p.shape, lambda b, i: (0, 0))],
        out_specs=tuple(tile(w) for w in IN_SPLITS),
        compiler_params=pltpu.CompilerParams(dimension_semantics=("arbitrary", "arbitrary")),
        name="mix_in",
    )(xc, mod3, mod3, norm_g.reshape(1, dim), w_in_p)


LM_PAD = 16
LM_ROWS = 64
LM_DBL_ROWS = 32


def _local_mixers_kernel(cin_ref, pin_ref, dw_ref, dwb_ref, lng_ref, lnb_ref, pw_ref, pwb_ref, psc_ref,
                         ya_ref, yp_ref, upad, ppad, s2, s4, s8, phase, *, n_ctx):
    n_all = cin_ref.shape[0]
    n_lat = n_all - n_ctx
    n_pad = upad.shape[0]
    lat0 = n_ctx + 2 * LM_PAD
    zeros = jnp.zeros((LM_PAD, W_CONV), F32)
    for buf in (upad, ppad):
        buf[0:LM_PAD, :] = zeros
        buf[LM_PAD + n_ctx:lat0, :] = zeros
        buf[lat0 + n_lat:n_pad, :] = zeros

    def pad_row(i):
        return pl.multiple_of(i * LM_ROWS + jnp.where(i < n_ctx // LM_ROWS, LM_PAD, 2 * LM_PAD), 8)

    def fill(i, _):
        r0 = pl.multiple_of(i * LM_ROWS, LM_ROWS)
        c = cin_ref[pl.ds(r0, LM_ROWS), :]
        dst = pl.ds(pad_row(i), LM_ROWS)
        upad[dst, :] = c[:, :W_CONV] * jax.nn.sigmoid(c[:, W_CONV:])
        ppad[dst, :] = pin_ref[pl.ds(r0, LM_ROWS), :]
        return 0
    lax.fori_loop(0, n_all // LM_ROWS, fill, 0)

    def doubling(src, dst, back, fwd):
        dst[0:8, :] = zeros[:8]
        dst[n_pad - 8:n_pad, :] = zeros[:8]

        def body(i, _):
            r0 = pl.multiple_of(8 + i * LM_DBL_ROWS, 8)
            h = src[pl.ds(r0 - 8, LM_DBL_ROWS + 16), :]
            dst[pl.ds(r0, LM_DBL_ROWS), :] = (h[8 - back:8 - back + LM_DBL_ROWS, :]
                                              + h[8 + fwd:8 + fwd + LM_DBL_ROWS, :])
            return 0
        lax.fori_loop(0, (n_pad - 16) // LM_DBL_ROWS, body, 0)
    doubling(ppad, s2, 1, 0)
    doubling(s2, s4, 1, 1)
    doubling(s4, s8, 2, 2)

    lane = lax.broadcasted_iota(jnp.int32, (LM_ROWS, W_POOL), 1)
    grp = lane // POOL_CH
    back = jnp.where(grp == 0, 1, jnp.where(grp == 1, 2, jnp.where(grp == 2, 4, 8)))
    fwd = back - 1

    def tile(i, _):
        r0 = pl.multiple_of(i * LM_ROWS, LM_ROWS)
        p0 = pad_row(i)
        halo = upad[pl.ds(p0 - LM_PAD, LM_ROWS + 2 * LM_PAD), :]
        span = LM_ROWS + 2 * LM_PAD - SUBLANE
        for b in range(SUBLANE):
            phase[b] = halo[b:b + span, :]
        acc = jnp.zeros((LM_ROWS, W_CONV), F32)
        for k in range(CONV_K):
            o = LM_PAD + k - CONV_K // 2
            a = o // SUBLANE * SUBLANE
            acc = acc + phase[o % SUBLANE, a:a + LM_ROWS, :] * dw_ref[k:k + 1, :]
        acc = acc + dwb_ref[...]
        mu = jnp.mean(acc, axis=-1, keepdims=True)
        var = jnp.mean(jnp.square(acc - mu), axis=-1, keepdims=True)
        v = (acc - mu) * lax.rsqrt(var + EPS) * lng_ref[...] + lnb_ref[...]
        v = v * jax.nn.sigmoid(v)
        ya_ref[pl.ds(r0, LM_ROWS), :] = jnp.dot(v.astype(BF16), pw_ref[...],
                                                preferred_element_type=F32).astype(ya_ref.dtype)
        rows = pl.ds(p0, LM_ROWS)
        h8 = s8[pl.ds(p0 - 8, LM_ROWS + 16), :]
        s16 = h8[4:4 + LM_ROWS, :] + h8[12:12 + LM_ROWS, :]
        wsum = jnp.where(grp == 0, s2[rows, :], jnp.where(grp == 1, s4[rows, :],
                                                          jnp.where(grp == 2, s8[rows, :], s16)))
        seg_len = jnp.where(i < n_ctx // LM_ROWS, n_ctx, n_lat)
        t = (r0 - jnp.where(i < n_ctx // LM_ROWS, 0, n_ctx)) + lax.broadcasted_iota(jnp.int32, (LM_ROWS, W_POOL), 0)
        cnt = jnp.minimum(t + fwd, seg_len - 1) - jnp.maximum(t - back, 0) + 1
        pooled = wsum / cnt.astype(F32) - ppad[rows, :]
        y = jnp.dot(pooled.astype(BF16), pwb_ref[...], preferred_element_type=F32) * psc_ref[...]
        yp_ref[pl.ds(r0, LM_ROWS), :] = y.astype(yp_ref.dtype)
        return 0
    lax.fori_loop(0, n_all // LM_ROWS, tile, 0)


def local_mixers(conv_in, pool_in, n_ctx, conv_dw, conv_dw_b, ln_g, ln_b, conv_pw, pool_w, pool_scale):
    bsz, n_all, _ = conv_in.shape
    assert n_ctx % LM_ROWS == 0 and n_all % LM_ROWS == 0
    n_pad = n_all + 3 * LM_PAD
    assert (n_pad - 16) % LM_DBL_ROWS == 0
    dw = jnp.zeros((CONV_K + 1, W_CONV), F32).at[:CONV_K].set(conv_dw)
    eye = jnp.eye(len(POOL_WINDOWS), dtype=F32)
    pwb = jnp.einsum('gcd,gh->gchd', pool_w, eye).reshape(W_POOL, W_POOL).astype(BF16)
    row = lambda v: v.reshape(1, -1).astype(F32)
    params = [dw, row(conv_dw_b), row(ln_g), row(ln_b), conv_pw.astype(BF16), pwb, row(pool_scale)]

    def full(arr):
        return pl.BlockSpec(arr.shape, lambda b: (0,) * arr.ndim)
    seq = lambda w: pl.BlockSpec((None, n_all, w), lambda b: (b, 0, 0))
    out = jax.ShapeDtypeStruct((bsz, n_all, W_CONV), BF16)
    return pl.pallas_call(
        functools.partial(_local_mixers_kernel, n_ctx=n_ctx),
        out_shape=(out, out),
        grid=(bsz,),
        in_specs=[seq(2 * W_CONV), seq(W_POOL)] + [full(p) for p in params],
        out_specs=(seq(W_CONV), seq(W_POOL)),
        scratch_shapes=[pltpu.VMEM((n_pad, W_CONV), F32) for _ in range(5)]
        + [pltpu.VMEM((SUBLANE, LM_ROWS + 2 * LM_PAD - SUBLANE, W_CONV), F32)],
        compiler_params=pltpu.CompilerParams(dimension_semantics=("arbitrary",)),
        name="local_mixers",
    )(conv_in, pool_in, *params)


S5_NS = SSM_GROUPS * SSM_STATE
S5_TC = 128
S5_BG = 8
S5_LANES = 512


def _dot2(a, b):
    m = a.shape[0] // 2
    return jnp.concatenate([jnp.dot(a[:m], b, preferred_element_type=F32),
                            jnp.dot(a[m:], b, preferred_element_type=F32)], axis=0)


def _s5_scan_kernel(u_ref, perm_ref, a_ref, bd_ref, cd_ref, *rest, reverse, final):
    if final:
        yp_ref, permt_ref, d_ref, gw_ref, gb_ref, y_ref, bu_s, hh_s, h_s = rest
    else:
        y_ref, bu_s, hh_s, h_s = rest
    tc = u_ref.shape[1]
    rows = S5_BG * tc
    j = pl.program_id(1)

    @pl.when(j == 0)
    def _():
        h_s[...] = jnp.zeros_like(h_s)

    u = u_ref[...].reshape(rows, W_SSM)
    u_hi = u.astype(BF16)
    if final:
        u_lo = (u - u_hi.astype(F32)).astype(BF16)
        u_tm = _dot2(perm_ref[...], jnp.concatenate([u_hi, u_lo], axis=1))
        u_hi_tm = u_tm[:, :W_SSM].astype(BF16)
        u_f32_tm = u_tm[:, :W_SSM] + u_tm[:, W_SSM:]
    else:
        u_hi_tm = _dot2(perm_ref[...], u_hi).astype(BF16)
    bu_s[...] = jnp.dot(u_hi_tm, bd_ref[...], preferred_element_type=F32)

    for hf in range(S5_NS // S5_LANES):
        re_cols = pl.ds(hf * S5_LANES, S5_LANES)
        im_cols = pl.ds(S5_NS + hf * S5_LANES, S5_LANES)
        a_re = jnp.broadcast_to(a_ref[0:1, hf * S5_LANES:(hf + 1) * S5_LANES], (S5_BG, S5_LANES))
        a_im = jnp.broadcast_to(a_ref[1:2, hf * S5_LANES:(hf + 1) * S5_LANES], (S5_BG, S5_LANES))

        def step(i, carry):
            h_re, h_im = carry
            t = (tc - 1 - i) if reverse else i
            r8 = pl.ds(pl.multiple_of(t * S5_BG, S5_BG), S5_BG)
            n_re = a_re * h_re - a_im * h_im + bu_s[r8, re_cols]
            n_im = a_re * h_im + a_im * h_re + bu_s[r8, im_cols]
            hh_s[r8, re_cols] = n_re
            hh_s[r8, im_cols] = n_im
            return n_re, n_im

        h_re, h_im = lax.fori_loop(0, tc, step, (h_s[:, re_cols], h_s[:, im_cols]), unroll=8)
        h_s[:, re_cols] = h_re
        h_s[:, im_cols] = h_im

    y = _dot2(hh_s[...].astype(BF16), cd_ref[...])
    if final:
        y = y + yp_ref[...] + d_ref[...] * u_f32_tm
        z = jax.nn.gelu(y)
        y = z * jax.nn.sigmoid(_dot2(z.astype(BF16), gw_ref[...]) + gb_ref[...])
        y = _dot2(permt_ref[...], y.astype(BF16))
        y_ref[...] = y.astype(BF16).reshape(S5_BG, tc, W_SSM)
    else:
        y_ref[...] = y


def s5_direction(u_all, perm, a, bd, cd, n_ctx, reverse, final_args=None):
    bsz, n_all, _ = u_all.shape
    tc = S5_TC
    n_chunks = n_all // tc
    ctx_chunks = n_ctx // tc
    rows = S5_BG * tc
    assert n_all % tc == 0 and n_ctx % tc == 0 and bsz % S5_BG == 0

    if reverse:
        def chunk(j):
            return jnp.where(j < ctx_chunks, ctx_chunks - 1 - j, n_chunks - 1 + ctx_chunks - j)
    else:
        def chunk(j):
            return j
    seq_spec = pl.BlockSpec((S5_BG, tc, W_SSM), lambda g, j: (g, chunk(j), 0))
    part_spec = pl.BlockSpec((None, None, rows, W_SSM), lambda g, j: (g, chunk(j), 0, 0))

    def full(arr):
        return pl.BlockSpec(arr.shape, lambda g, j: (0,) * arr.ndim)
    in_specs = [seq_spec, full(perm), full(a), full(bd), full(cd)]
    args = [u_all, perm, a, bd, cd]
    if final_args is not None:
        y_prev, d, glu_w, glu_b = final_args
        perm_t = perm.T
        in_specs += [part_spec, full(perm_t), full(d), full(glu_w), full(glu_b)]
        args += [y_prev, perm_t, d, glu_w, glu_b]
        out_shape = jax.ShapeDtypeStruct(u_all.shape, BF16)
        out_spec = seq_spec
    else:
        out_shape = jax.ShapeDtypeStruct((bsz // S5_BG, n_chunks, rows, W_SSM), F32)
        out_spec = part_spec
    return pl.pallas_call(
        functools.partial(_s5_scan_kernel, reverse=reverse, final=final_args is not None),
        out_shape=out_shape,
        grid=(bsz // S5_BG, n_chunks),
        in_specs=in_specs,
        out_specs=out_spec,
        scratch_shapes=[pltpu.VMEM((rows, 2 * S5_NS), F32), pltpu.VMEM((rows, 2 * S5_NS), F32),
                        pltpu.VMEM((S5_BG, 2 * S5_NS), F32)],
        compiler_params=pltpu.CompilerParams(dimension_semantics=("arbitrary", "arbitrary")),
        name="s5_scan_bwd" if reverse else "s5_scan_fwd",
    )(*args)


def s5_time_major_perm(tc):
    r = jnp.arange(S5_BG * tc)
    src = (r % S5_BG) * tc + r // S5_BG
    return (src[:, None] == r[None, :]).astype(BF16)


def s5_params(a_re, a_im, log_dt, b_re, b_im, c_re, c_im):
    a_re = jnp.minimum(a_re.astype(F32), -1e-4)
    a_im = a_im.astype(F32)
    dt = jnp.exp(log_dt.astype(F32))[:, None]
    mag = jnp.exp(a_re * dt)
    ab_re = mag * jnp.cos(a_im * dt)
    ab_im = mag * jnp.sin(a_im * dt)
    den = a_re * a_re + a_im * a_im
    f_re = ((ab_re - 1.0) * a_re + ab_im * a_im) / den
    f_im = (ab_im * a_re - (ab_re - 1.0) * a_im) / den
    bb_re = f_re[..., None] * b_re - f_im[..., None] * b_im
    bb_im = f_re[..., None] * b_im + f_im[..., None] * b_re
    eye = jnp.eye(SSM_GROUPS, dtype=F32)

    def in_map(bb):
        return jnp.einsum('gpc,gh->gchp', bb, eye).reshape(W_SSM, S5_NS)

    def out_map(cc):
        return jnp.einsum('gcp,gh->gphc', cc, eye).reshape(S5_NS, W_SSM)
    a = jnp.stack([ab_re.reshape(-1), ab_im.reshape(-1)])
    bd = jnp.concatenate([in_map(bb_re), in_map(bb_im)], axis=1).astype(BF16)
    cd = jnp.concatenate([out_map(c_re.astype(F32)), -out_map(c_im.astype(F32))], axis=0).astype(BF16)
    return a, bd, cd


def s5_mixer_pallas(u_all, n_ctx, a_re, a_im, log_dt, b_re, b_im, c_re, c_im, d, glu_w, glu_b):
    pf = s5_params(a_re[0], a_im[0], log_dt[0], b_re[0], b_im[0], c_re[0], c_im[0])
    pb = s5_params(a_re[1], a_im[1], log_dt[1], b_re[1], b_im[1], c_re[1], c_im[1])
    perm = s5_time_major_perm(S5_TC)
    y_f = s5_direction(u_all, perm, *pf, n_ctx, False)
    return s5_direction(u_all, perm, *pb, n_ctx, True,
                        (y_f, d.reshape(1, W_SSM).astype(F32), glu_w.astype(BF16), glu_b.reshape(1, W_SSM)))


HEAD_W = LANE
ATT_PREP_ROWS = 256
ATT_Q_ROWS = 256


def _head_dim_order():
    half = ROPE_AXIS // 2
    r = MLA_NOPE
    first = list(range(r, r + half)) + list(range(r + 2 * half, r + 3 * half))
    second = list(range(r + half, r + 2 * half)) + list(range(r + 3 * half, r + 4 * half))
    split = LANE // 2 - MLA_ROPE // 2
    return np.array(first + list(range(split)) + second + list(range(split, MLA_NOPE)))


def _to_head_lanes(w):
    order = _head_dim_order()
    pad = [(0, 0)] * (w.ndim - 1) + [(0, HEAD_W - len(order))]
    return jnp.pad(jnp.take(w, order, axis=-1), pad)


def mla_layout_params(q_a_g, wq_b, kv_a_g, wkv_b, q_g, k_g):
    wq = _to_head_lanes(wq_b.reshape(MLA_Q_RANK, MLA_HEADS, MLA_QK))
    wq_p = jnp.pad(wq, ((0, QKP_W - MLA_Q_RANK), (0, 0), (0, 0))).reshape(QKP_W, MLA_HEADS * HEAD_W).astype(BF16)
    qa_g = jnp.pad(q_a_g, (0, QKP_W - MLA_Q_RANK)).reshape(1, QKP_W)
    wkv = wkv_b.reshape(MLA_KV_RANK, MLA_HEADS, MLA_NOPE + MLA_V)
    wk = jnp.pad(wkv[:, :, :MLA_NOPE], ((0, 0), (0, 0), (0, MLA_ROPE)))
    wkv_p = jnp.concatenate([_to_head_lanes(wk).reshape(MLA_KV_RANK, -1),
                             wkv[:, :, MLA_NOPE:].reshape(MLA_KV_RANK, -1)], axis=1).astype(BF16)
    order = _head_dim_order()
    place = np.zeros((QKP_W, MLA_HEADS, HEAD_W), np.float32)
    for lane_i, dim in enumerate(order):
        if dim >= MLA_NOPE:
            place[MLA_Q_RANK + dim - MLA_NOPE, :, lane_i] = 1.0
    place = jnp.asarray(place.reshape(QKP_W, MLA_HEADS * HEAD_W), BF16)
    return (wq_p, qa_g, wkv_p, kv_a_g.reshape(1, MLA_KV_RANK), place,
            _to_head_lanes(q_g.reshape(1, MLA_QK)), _to_head_lanes(k_g.reshape(1, MLA_QK)))


def rope_lane_tables(n_ctx, n_lat):
    t = jnp.arange(n_lat)
    row = (t // GRID_W).astype(F32)
    col = (t % GRID_W).astype(F32)
    inv = ROPE_BASE ** (-jnp.arange(0, ROPE_AXIS, 2, dtype=F32) / ROPE_AXIS)
    ang = jnp.concatenate([row[:, None] * inv, col[:, None] * inv], axis=1)
    cos, sin = jnp.cos(ang), jnp.sin(ang)
    half = MLA_ROPE // 2
    one = jnp.ones((n_lat, LANE // 2 - half), F32)
    zero = jnp.zeros((n_lat, LANE // 2 - half), F32)
    cmul = jnp.concatenate([cos, one, cos, one], axis=1)
    smul = jnp.concatenate([-sin, zero, sin, zero], axis=1)
    ctx = [jnp.ones((n_ctx, HEAD_W), F32), jnp.zeros((n_ctx, HEAD_W), F32)]
    return jnp.stack([jnp.concatenate([c, x], axis=0) for c, x in zip(ctx, (cmul, smul))])


def _mla_kernel(ckv_ref, qkp_ref, rope_ref, wq_ref, qag_ref, wkv_ref, kvag_ref, place_ref, qg_ref, kg_ref,
                o_ref, q_s, k_s, v_s, *, n_ctx, ctx_queries):
    n_all = ckv_ref.shape[0]

    def rope(x, r0):
        rows = pl.ds(r0, ATT_PREP_ROWS)
        swapped = pltpu.roll(x, HEAD_W // 2, axis=1)
        return x * rope_ref[0, rows, :] + swapped * rope_ref[1, rows, :]

    def head_norm(x, g):
        ss = jnp.sum(x * x, axis=-1, keepdims=True) * (1.0 / MLA_QK)
        return x * lax.rsqrt(ss + EPS) * g

    def prep(i, _):
        r0 = pl.multiple_of(i * ATT_PREP_ROWS, ATT_PREP_ROWS)
        rows = pl.ds(r0, ATT_PREP_ROWS)
        ckv = ckv_ref[rows, :]
        kvn = ckv * lax.rsqrt(jnp.mean(ckv * ckv, axis=-1, keepdims=True) + EPS) * kvag_ref[...]
        kv = jnp.dot(kvn.astype(BF16), wkv_ref[...], preferred_element_type=F32)
        qkp = qkp_ref[rows, :]
        p_hi = qkp.astype(BF16)
        p_lo = (qkp - p_hi.astype(F32)).astype(BF16)
        kpe = (jnp.dot(p_hi, place_ref[...], preferred_element_type=F32)
               + jnp.dot(p_lo, place_ref[...], preferred_element_type=F32))
        lane = lax.broadcasted_iota(jnp.int32, qkp.shape, 1)
        qsq = jnp.where(lane < MLA_Q_RANK, qkp * qkp, 0.0)
        qn = qkp * lax.rsqrt(jnp.sum(qsq, axis=-1, keepdims=True) * (1.0 / MLA_Q_RANK) + EPS) * qag_ref[...]
        q = jnp.dot(qn.astype(BF16), wq_ref[...], preferred_element_type=F32)
        for h in range(MLA_HEADS):
            blk = slice(h * HEAD_W, (h + 1) * HEAD_W)
            k_h = rope(head_norm(kv[:, blk] + kpe[:, blk], kg_ref[...]), r0)
            q_h = rope(head_norm(q[:, blk], qg_ref[...]), r0) * (MLA_QK ** -0.5)
            k_s[h, rows, :] = k_h.astype(BF16)
            q_s[h, rows, :] = q_h.astype(BF16)
        v_s[rows, :] = kv[:, MLA_HEADS * HEAD_W:].astype(BF16)
        return 0

    lax.fori_loop(0, n_all // ATT_PREP_ROWS, prep, 0)

    lane = lax.broadcasted_iota(jnp.int32, (ATT_Q_ROWS, LANE), 1)

    def attend_rows(r0, n_keys):
        outs = []
        for pair in range(MLA_HEADS // 2):
            v2 = v_s[pl.ds(0, n_keys), pair * LANE:(pair + 1) * LANE]
            o2 = []
            for h in (2 * pair, 2 * pair + 1):
                q = q_s[h, pl.ds(r0, ATT_Q_ROWS), :]
                k = k_s[h, pl.ds(0, n_keys), :]
                s = lax.dot_general(q, k, (((1,), (1,)), ((), ())), preferred_element_type=F32)
                p = jnp.exp(s - jnp.max(s, axis=-1, keepdims=True))
                den = jnp.sum(p, axis=-1, keepdims=True)
                o2.append(jnp.dot(p.astype(BF16), v2, preferred_element_type=F32) / den)
            outs.append(jnp.where(lane < MLA_V, o2[0], o2[1]))
        return jnp.concatenate(outs, axis=1)

    for i in range(n_ctx // ATT_Q_ROWS):
        r0 = i * ATT_Q_ROWS
        if ctx_queries:
            o_ref[pl.ds(r0, ATT_Q_ROWS), :] = attend_rows(r0, n_ctx).astype(o_ref.dtype)
        else:
            o_ref[pl.ds(r0, ATT_Q_ROWS), :] = jnp.zeros((ATT_Q_ROWS, W_MLA), o_ref.dtype)

    def q_tile(i, _):
        r0 = pl.multiple_of(n_ctx + i * ATT_Q_ROWS, ATT_Q_ROWS)
        o_ref[pl.ds(r0, ATT_Q_ROWS), :] = attend_rows(r0, n_all).astype(o_ref.dtype)
        return 0

    lax.fori_loop(0, (n_all - n_ctx) // ATT_Q_ROWS, q_tile, 0, unroll=2)


def mla_attention(ckv, qkp, rope_tab, params, n_ctx, ctx_queries):
    bsz, n_all, _ = ckv.shape
    assert n_all % ATT_PREP_ROWS == 0 and n_ctx % ATT_Q_ROWS == 0 and (n_all - n_ctx) % ATT_Q_ROWS == 0

    def full(arr):
        return pl.BlockSpec(arr.shape, lambda b: (0,) * arr.ndim)
    return pl.pallas_call(
        functools.partial(_mla_kernel, n_ctx=n_ctx, ctx_queries=ctx_queries),
        out_shape=jax.ShapeDtypeStruct((bsz, n_all, W_MLA), BF16),
        grid=(bsz,),
        in_specs=[pl.BlockSpec((None, n_all, MLA_KV_RANK), lambda b: (b, 0, 0)),
                  pl.BlockSpec((None, n_all, QKP_W), lambda b: (b, 0, 0)),
                  full(rope_tab)] + [full(p) for p in params],
        out_specs=pl.BlockSpec((None, n_all, W_MLA), lambda b: (b, 0, 0)),
        scratch_shapes=[pltpu.VMEM((MLA_HEADS, n_all, HEAD_W), BF16), pltpu.VMEM((MLA_HEADS, n_all, HEAD_W), BF16),
                        pltpu.VMEM((n_all, MLA_HEADS * MLA_V), BF16)],
        compiler_params=pltpu.CompilerParams(dimension_semantics=("arbitrary",)),
        name="mla_attention",
    )(ckv, qkp, rope_tab, *params)


RUN_ROWS = 1280
RUN_BITS = (256, 128, 64, 32, 16, 8)
PACK_W = D_MODEL // 2
RT_EXPERT, RT_WEIGHT, RT_SLOT = 0, TOP_K, 2 * TOP_K
RT_ROWS = 16
META_SRC, META_CNT, META_BASE = 0, 1, 2
I32 = jnp.int32
U32 = jnp.uint32


def _mix_out_kernel(ya_ref, yp_ref, ys_ref, ym_ref, x_ref, g1_ref, sh_ref, sc_ref, ng_ref, wo_ref, rwt_ref, rbt_ref,
                    tri_ref, upper_ref, lower_ref, xo_ref, h_ref, rt_ref, meta_ref, tot_ref, carry_s):
    first = jnp.logical_and(pl.program_id(0) == 0, pl.program_id(1) == 0)

    @pl.when(first)
    def _():
        carry_s[...] = jnp.zeros_like(carry_s)

    acc = jnp.dot(ya_ref[...], wo_ref[0:W_CONV, :], preferred_element_type=F32)
    acc += jnp.dot(yp_ref[...], wo_ref[W_CONV:W_CONV + W_POOL, :], preferred_element_type=F32)
    acc += jnp.dot(ys_ref[...], wo_ref[W_CONV + W_POOL:D_MIX - W_MLA, :], preferred_element_type=F32)
    acc += jnp.dot(ym_ref[...], wo_ref[D_MIX - W_MLA:, :], preferred_element_type=F32)
    x = x_ref[...] + g1_ref[...] * acc
    xo_ref[...] = x
    y = x * lax.rsqrt(jnp.mean(x * x, axis=-1, keepdims=True) + EPS) * ng_ref[...]
    h = (y * (1.0 + sc_ref[...]) + sh_ref[...]).astype(BF16)
    h_ref[...] = h
    logits = lax.dot_general(rwt_ref[...], h, (((1,), (1,)), ((), ())), preferred_element_type=F32) + rbt_ref[...]
    eid = lax.broadcasted_iota(I32, logits.shape, 0)
    onehot = jnp.zeros(logits.shape, F32)
    vals, hits, rows = [], [], []
    for k in range(TOP_K):
        m = jnp.max(logits, axis=0, keepdims=True)
        idx = jnp.min(jnp.where(logits == m, eid, N_EXPERTS), axis=0, keepdims=True)
        hit = eid == idx
        logits = jnp.where(hit, -jnp.inf, logits)
        onehot = jnp.where(hit, 1.0, onehot)
        vals.append(m)
        hits.append(hit)
        rows.append(idx.astype(F32))
    ex = [jnp.exp(v - vals[0]) for v in vals]
    den = ex[0] + ex[1] + ex[2] + ex[3]
    rows += [e / den for e in ex]
    oh = onehot.astype(BF16)
    before = jnp.dot(oh, tri_ref[...], preferred_element_type=F32)
    cnt_col = jnp.sum(onehot, axis=1, keepdims=True)
    cnt8_col = jnp.floor((cnt_col + (SUBLANE - 1)) * (1.0 / SUBLANE)) * SUBLANE
    src_col = jnp.dot(lower_ref[...], jnp.broadcast_to(cnt8_col, (N_EXPERTS, LANE)).astype(BF16),
                      preferred_element_type=F32)[:, 0:1]
    slot_of = before + src_col
    for k in range(TOP_K):
        rows.append(jnp.sum(jnp.where(hits[k], slot_of, 0.0), axis=0, keepdims=True))
    rows += [jnp.zeros_like(rows[0])] * (RT_ROWS - len(rows))
    rt_ref[...] = jnp.concatenate(rows, axis=0)
    ones = jnp.ones((SUBLANE, TOK_ROWS), BF16)
    cnt = lax.dot_general(ones, oh, (((1,), (1,)), ((), ())), preferred_element_type=F32)
    cnt = jnp.concatenate([cnt, jnp.zeros((SUBLANE, LANE - N_EXPERTS), F32)], axis=1)
    cnt8 = jnp.floor((cnt + (SUBLANE - 1)) * (1.0 / SUBLANE)) * SUBLANE
    src = jnp.dot(cnt8.astype(BF16), upper_ref[...], preferred_element_type=F32)
    base = carry_s[...]
    row = lax.broadcasted_iota(I32, (SUBLANE, LANE), 0)
    meta = jnp.where(row == META_SRC, src, jnp.where(row == META_CNT, cnt8, jnp.where(row == META_BASE, base, 0.0)))
    meta_ref[...] = meta.astype(I32)
    carry_s[...] = base + cnt8
    tot_ref[...] = (base + cnt8).astype(I32)


def mix_out(ya, yp, ys, ym, xc, mod3, norm_g, w_out, router_w, router_b, n_ctx, skip_ctx):
    bsz, n_all, dim = xc.shape
    nct = n_ctx // TOK_ROWS
    off = nct if skip_ctx else 0
    n_out = n_all - off * TOK_ROWS
    tiles = n_out // TOK_ROWS
    tile_in = lambda w: pl.BlockSpec((None, TOK_ROWS, w), lambda b, i: (b, i + off, 0))
    tile_out = lambda w: pl.BlockSpec((None, TOK_ROWS, w), lambda b, i: (b, i, 0))

    def mod_spec(chunk):
        return pl.BlockSpec((None, 1, dim), lambda b, i: (jnp.where(i + off < nct, bsz, b), 0, chunk))
    rwt = router_w.T.astype(BF16)
    rbt = router_b.reshape(N_EXPERTS, 1).astype(F32)
    r = jnp.arange(TOK_ROWS)
    tri = (r[:, None] < r[None, :]).astype(BF16)
    q = jnp.arange(LANE)
    upper = (q[:, None] < q[None, :]).astype(BF16)
    e = jnp.arange(N_EXPERTS)
    lower = (e[None, :] < e[:, None]).astype(BF16)
    full = lambda a: pl.BlockSpec(a.shape, lambda b, i: (0,) * a.ndim)
    wo = w_out.astype(BF16)
    ng = norm_g.reshape(1, dim)
    return pl.pallas_call(
        _mix_out_kernel,
        out_shape=(jax.ShapeDtypeStruct((bsz, n_out, dim), F32), jax.ShapeDtypeStruct((bsz, n_out, dim), BF16),
                   jax.ShapeDtypeStruct((bsz * tiles, RT_ROWS, TOK_ROWS), F32),
                   jax.ShapeDtypeStruct((bsz * tiles, SUBLANE, LANE), I32),
                   jax.ShapeDtypeStruct((SUBLANE, LANE), I32)),
        grid=(bsz, tiles),
        in_specs=[tile_in(W_CONV), tile_in(W_POOL), tile_in(W_SSM), tile_in(W_MLA), tile_in(dim),
                  mod_spec(2), mod_spec(3), mod_spec(4), full(ng), full(wo), full(rwt), full(rbt), full(tri),
                  full(upper), full(lower)],
        out_specs=(tile_out(dim), tile_out(dim),
                   pl.BlockSpec((None, RT_ROWS, TOK_ROWS), lambda b, i: (b * tiles + i, 0, 0)),
                   pl.BlockSpec((None, SUBLANE, LANE), lambda b, i: (b * tiles + i, 0, 0)),
                   pl.BlockSpec((SUBLANE, LANE), lambda b, i: (0, 0))),
        scratch_shapes=[pltpu.VMEM((SUBLANE, LANE), F32)],
        compiler_params=pltpu.CompilerParams(dimension_semantics=("arbitrary", "arbitrary")),
        name="mix_out",
    )(ya, yp, ys, ym, xc, mod3, mod3, mod3, ng, wo, rwt, rbt, tri, upper, lower)


def expert_layout(meta, tot, n_tok):
    n_tiles = meta.shape[0]
    tot8 = tot[0, :N_EXPERTS]
    region = (tot8 + MOE_BLOCK - 1) // MOE_BLOCK * MOE_BLOCK
    g_end = jnp.cumsum(region)
    g_start = g_end - region
    n_blocks = -(-(n_tok * TOP_K + n_tiles * N_EXPERTS * (SUBLANE - 1)) // MOE_BLOCK) + N_EXPERTS
    block_start = jnp.arange(n_blocks) * MOE_BLOCK
    block_e = jnp.minimum(jnp.sum(g_end[None, :] <= block_start[:, None], axis=1), N_EXPERTS - 1).astype(I32)
    n_used = (g_end[-1] // MOE_BLOCK).astype(I32).reshape(1)
    g_start_l = jnp.zeros((LANE,), I32).at[:N_EXPERTS].set(g_start.astype(I32))
    total = jnp.sum(meta[:, META_CNT, :N_EXPERTS], axis=1)
    cnt = meta[:, META_CNT].at[:, N_EXPERTS].set(total)
    runs = jnp.stack([meta[:, META_SRC], cnt, meta[:, META_BASE] + g_start_l], axis=1)
    tails = jnp.zeros((2, LANE), I32).at[0, :N_EXPERTS].set((g_start + tot8).astype(I32))
    tails = tails.at[1, :N_EXPERTS].set((region - tot8).astype(I32))
    return runs, tails, block_e, n_used, n_blocks


TILE_BITS = (1024, 512, 256, 128, 64, 32, 16, 8)


def _start_run_copies(runs_ref, local, remote, sem, to_remote):
    def per_expert(e, c):
        s = runs_ref[0, 0, e]
        n = runs_ref[0, 1, e]
        d = runs_ref[0, 2, e]
        for bit in RUN_BITS:
            take = (n & bit) != 0
            loc = local.at[pl.ds(pl.multiple_of(s, SUBLANE), bit), :]
            rem = remote.at[pl.ds(pl.multiple_of(d, SUBLANE), bit), :]
            cp = pltpu.make_async_copy(loc, rem, sem) if to_remote else pltpu.make_async_copy(rem, loc, sem)

            @pl.when(take)
            def _():
                cp.start()
            s = s + jnp.where(take, bit, 0)
            d = d + jnp.where(take, bit, 0)
        return c
    lax.fori_loop(0, N_EXPERTS, per_expert, 0)


def _wait_run_copies(runs_ref, local, remote, sem, to_remote):
    total = runs_ref[0, 1, N_EXPERTS]
    for bit in TILE_BITS:
        loc = local.at[pl.ds(0, bit), :]
        rem = remote.at[pl.ds(0, bit), :]
        cp = pltpu.make_async_copy(loc, rem, sem) if to_remote else pltpu.make_async_copy(rem, loc, sem)

        @pl.when((total & bit) != 0)
        def _():
            cp.wait()


def _dispatch_kernel(runs_ref, runs_prev_ref, tails_ref, h_ref, slot_ref, xs_ref, buf, zbuf, sems):
    i = pl.program_id(0)
    cur = i % 2

    @pl.when(i == 0)
    def _():
        zbuf[...] = jnp.zeros_like(zbuf)

        def tail_copies(wait):
            def per_expert(e, c):
                d = tails_ref[0, e]
                n = tails_ref[1, e]
                for bit in RUN_BITS:
                    take = (n & bit) != 0
                    cp = pltpu.make_async_copy(zbuf.at[pl.ds(0, bit), :],
                                               xs_ref.at[pl.ds(pl.multiple_of(d, SUBLANE), bit), :], sems.at[2])

                    @pl.when(take)
                    def _():
                        if wait:
                            cp.wait()
                        else:
                            cp.start()
                    d = d + jnp.where(take, bit, 0)
                return c
            lax.fori_loop(0, N_EXPERTS, per_expert, 0)
        tail_copies(False)
        tail_copies(True)

    j = lax.broadcasted_iota(I32, (RUN_ROWS, TOK_ROWS), 0)
    hit = j == slot_ref[0:1, :]
    for k in range(1, TOP_K):
        hit = jnp.logical_or(hit, j == slot_ref[k:k + 1, :])
    p = jnp.where(hit, 1.0, 0.0).astype(BF16)
    lo = jnp.dot(p, h_ref[:, :PACK_W], preferred_element_type=F32)
    hi = jnp.dot(p, h_ref[:, PACK_W:], preferred_element_type=F32)
    lo_bits = lax.shift_right_logical(pltpu.bitcast(lo, U32), jnp.uint32(16))
    hi_bits = pltpu.bitcast(hi, U32) & jnp.uint32(0xFFFF0000)
    buf[cur] = hi_bits | lo_bits
    _start_run_copies(runs_ref, buf.at[cur], xs_ref, sems.at[cur], True)

    @pl.when(i > 0)
    def _():
        _wait_run_copies(runs_prev_ref, buf.at[1 - cur], xs_ref, sems.at[1 - cur], True)

    @pl.when(i == pl.num_programs(0) - 1)
    def _():
        _wait_run_copies(runs_ref, buf.at[cur], xs_ref, sems.at[cur], True)


def dispatch(h, slots_t, runs, tails, n_blocks):
    n_tok, dim = h.shape
    n_tiles = n_tok // TOK_ROWS
    return pl.pallas_call(
        _dispatch_kernel,
        out_shape=jax.ShapeDtypeStruct((n_blocks * MOE_BLOCK, PACK_W), U32),
        grid=(n_tiles,),
        in_specs=[pl.BlockSpec((1, 3, LANE), lambda i: (i, 0, 0), memory_space=pltpu.SMEM),
                  pl.BlockSpec((1, 3, LANE), lambda i: (jnp.maximum(i - 1, 0), 0, 0), memory_space=pltpu.SMEM),
                  pl.BlockSpec((2, LANE), lambda i: (0, 0), memory_space=pltpu.SMEM),
                  pl.BlockSpec((TOK_ROWS, dim), lambda i: (i, 0)),
                  pl.BlockSpec((None, SUBLANE, TOK_ROWS), lambda i: (i, 0, 0))],
        out_specs=pl.BlockSpec(memory_space=pl.ANY),
        scratch_shapes=[pltpu.VMEM((2, RUN_ROWS, PACK_W), U32), pltpu.VMEM((RUN_BITS[0], PACK_W), U32),
                        pltpu.SemaphoreType.DMA((3,))],
        compiler_params=pltpu.CompilerParams(dimension_semantics=("arbitrary",)),
        name="moe_dispatch",
    )(runs, runs, tails, h, slots_t)


def _expert_block_kernel(be_ref, nu_ref, x_ref, win_ref, bin_ref, wout_ref, bout_ref, o_ref, win_s, wout_s):
    i = pl.program_id(0)
    used = i < nu_ref[0]
    new_expert = jnp.logical_or(i == 0, be_ref[i] != be_ref[jnp.maximum(i - 1, 0)])

    @pl.when(jnp.logical_and(used, new_expert))
    def _():
        win_s[...] = win_ref[...].astype(BF16)
        wout_s[...] = wout_ref[...].astype(BF16)

    @pl.when(used)
    def _():
        w = x_ref[...]
        x_lo = pltpu.bitcast(lax.shift_left(w, jnp.uint32(16)), F32).astype(BF16)
        x_hi = pltpu.bitcast(w & jnp.uint32(0xFFFF0000), F32).astype(BF16)
        gu = (jnp.dot(x_lo, win_s[:PACK_W, :], preferred_element_type=F32)
              + jnp.dot(x_hi, win_s[PACK_W:, :], preferred_element_type=F32) + bin_ref[...])
        gate = jnp.minimum(gu[:, :D_EXPERT], SWIGLU_LIMIT)
        up = jnp.clip(gu[:, D_EXPERT:], -SWIGLU_LIMIT, SWIGLU_LIMIT)
        act = (up + 1.0) * gate * jax.nn.sigmoid(SWIGLU_ALPHA * gate)
        o_ref[...] = jnp.dot(act.astype(BF16), wout_s[...], preferred_element_type=F32) + bout_ref[...]

    @pl.when(jnp.logical_not(used))
    def _():
        o_ref[...] = jnp.zeros_like(o_ref)


def expert_blocks(xs, block_e, n_used, layer, w_in, b_in, w_out, b_out):
    n_pad = xs.shape[0]
    dim = w_out.shape[-1]
    n_blocks = n_pad // MOE_BLOCK

    def row_map(i, be, nu):
        return (jnp.minimum(i, nu[0] - 1), 0)

    def exp_map(i, be, nu):
        return (layer, be[i], 0, 0)

    return pl.pallas_call(
        _expert_block_kernel,
        out_shape=jax.ShapeDtypeStruct((n_pad, dim), F32),
        grid_spec=pltpu.PrefetchScalarGridSpec(
            num_scalar_prefetch=2,
            grid=(n_blocks,),
            in_specs=[
                pl.BlockSpec((MOE_BLOCK, PACK_W), row_map),
                pl.BlockSpec((None, None, dim, 2 * D_EXPERT), exp_map),
                pl.BlockSpec((None, None, 1, 2 * D_EXPERT), exp_map),
                pl.BlockSpec((None, None, D_EXPERT, dim), exp_map),
                pl.BlockSpec((None, None, 1, dim), exp_map),
            ],
            out_specs=pl.BlockSpec((MOE_BLOCK, dim), lambda i, be, nu: (i, 0)),
            scratch_shapes=[pltpu.VMEM((dim, 2 * D_EXPERT), BF16), pltpu.VMEM((D_EXPERT, dim), BF16)],
        ),
        compiler_params=pltpu.CompilerParams(dimension_semantics=("arbitrary",)),
        name="expert_blocks",
    )(block_e, n_used, xs, w_in, b_in[:, :, None, :], w_out, b_out[:, :, None, :])


def _combine_kernel(runs_ref, runs_next_ref, ys_ref, rt_ref, x_ref, g_ref, o_ref, ybuf, sems):
    i = pl.program_id(0)
    cur = i % 2

    @pl.when(i == 0)
    def _():
        ybuf[...] = jnp.zeros_like(ybuf)
        _start_run_copies(runs_ref, ybuf.at[0], ys_ref, sems.at[0], False)

    @pl.when(i + 1 < pl.num_programs(0))
    def _():
        _start_run_copies(runs_next_ref, ybuf.at[1 - cur], ys_ref, sems.at[1 - cur], False)

    _wait_run_copies(runs_ref, ybuf.at[cur], ys_ref, sems.at[cur], False)
    rt = rt_ref[...]
    j = lax.broadcasted_iota(I32, (TOK_ROWS, RUN_ROWS), 1).astype(F32)
    wmat = jnp.zeros((TOK_ROWS, RUN_ROWS), F32)
    for k in range(TOP_K):
        wmat = jnp.where(j == rt[:, RT_SLOT + k:RT_SLOT + k + 1], rt[:, RT_WEIGHT + k:RT_WEIGHT + k + 1], wmat)
    w_hi = wmat.astype(BF16)
    w_lo = (wmat - w_hi.astype(F32)).astype(BF16)
    y = ybuf[cur].astype(BF16)
    m = jnp.dot(w_hi, y, preferred_element_type=F32) + jnp.dot(w_lo, y, preferred_element_type=F32)
    o_ref[...] = x_ref[...] + g_ref[...] * m


def combine(ys, route, runs, xn, mod3, n_ctx_tiles):
    bsz, n_rows, dim = xn.shape
    tiles = n_rows // TOK_ROWS
    tile = lambda w: pl.BlockSpec((None, TOK_ROWS, w), lambda i: (i // tiles, i % tiles, 0))
    g_spec = pl.BlockSpec((None, 1, dim), lambda i: (jnp.where(i % tiles < n_ctx_tiles, bsz, i // tiles), 0, 5))
    return pl.pallas_call(
        _combine_kernel,
        out_shape=jax.ShapeDtypeStruct(xn.shape, F32),
        grid=(bsz * tiles,),
        in_specs=[pl.BlockSpec((1, 3, LANE), lambda i: (i, 0, 0), memory_space=pltpu.SMEM),
                  pl.BlockSpec((1, 3, LANE), lambda i: (jnp.minimum(i + 1, bsz * tiles - 1), 0, 0),
                               memory_space=pltpu.SMEM),
                  pl.BlockSpec(memory_space=pl.ANY), tile(RT_ROWS), tile(dim), g_spec],
        out_specs=tile(dim),
        scratch_shapes=[pltpu.VMEM((2, RUN_ROWS, dim), F32), pltpu.SemaphoreType.DMA((2,))],
        compiler_params=pltpu.CompilerParams(dimension_semantics=("arbitrary",)),
        name="moe_combine",
    )(runs, runs, ys, route, xn, mod3)


def routed_experts(xn, h2, route_t, meta, tot, mod3, n_ctx_tiles, layer, w_in, b_in, w_out, b_out):
    bsz, n_rows, dim = xn.shape
    n_tok = bsz * n_rows
    runs, tails, block_e, n_used, n_blocks = expert_layout(meta, tot, n_tok)
    slots_t = route_t[:, RT_SLOT:RT_SLOT + SUBLANE, :].astype(I32)
    route = jnp.swapaxes(route_t, 1, 2).reshape(bsz, n_rows, RT_ROWS)
    xs = dispatch(h2.reshape(n_tok, dim), slots_t, runs, tails, n_blocks)
    ys = expert_blocks(xs, block_e, n_used, layer, w_in, b_in, w_out, b_out)
    return combine(ys, route, runs, xn, mod3, n_ctx_tiles)


def kernel(x, c, ctx, c_ctx, ada_w, ada_b, norm1_g, norm2_g, w_mix_in, w_mix_out,
           conv_dw, conv_dw_b, conv_ln_g, conv_ln_b, conv_pw, pool_w, pool_scale,
           ssm_a_re, ssm_a_im, ssm_log_dt, ssm_b_re, ssm_b_im, ssm_c_re, ssm_c_im,
           ssm_d, ssm_glu_w, ssm_glu_b,
           mla_q_a_g, mla_wq_b, mla_kv_a_g, mla_wkv_b, mla_q_g, mla_k_g,
           router_w, router_b, exp_w_in, exp_b_in, exp_w_out, exp_b_out):
    bsz, n_lat, dim = x.shape
    n_ctx = ctx.shape[1]
    xc = jnp.concatenate([ctx, x], axis=1)
    c_all = jnp.zeros((MOD_ROWS, dim), F32).at[:bsz].set(c).at[bsz].set(c_ctx)
    rope_tab = rope_lane_tables(n_ctx, n_lat)
    for l in range(DEPTH):
        last = l == DEPTH - 1
        mod = modulation(c_all, ada_w[l], ada_b[l])
        mod3 = mod.reshape(MOD_ROWS, 1, 6 * dim)

        conv_in, pool_in, ssm_in, ckv, qkp = mix_in(xc, mod3, norm1_g[l], mix_in_weights(w_mix_in[l]), n_ctx)
        ya, yp = local_mixers(conv_in, pool_in, n_ctx, conv_dw[l], conv_dw_b[l], conv_ln_g[l], conv_ln_b[l],
                              conv_pw[l], pool_w[l], pool_scale[l])
        ys = s5_mixer_pallas(ssm_in, n_ctx, ssm_a_re[l], ssm_a_im[l], ssm_log_dt[l],
                             ssm_b_re[l], ssm_b_im[l], ssm_c_re[l], ssm_c_im[l],
                             ssm_d[l], ssm_glu_w[l], ssm_glu_b[l])
        ym = mla_attention(ckv, qkp, rope_tab,
                           mla_layout_params(mla_q_a_g[l], mla_wq_b[l], mla_kv_a_g[l], mla_wkv_b[l],
                                             mla_q_g[l], mla_k_g[l]), n_ctx, not last)
        xn, h2, route, meta, tot = mix_out(ya, yp, ys, ym, xc, mod3, norm2_g[l], w_mix_out[l],
                                           router_w[l], router_b[l], n_ctx, last)
        xc = routed_experts(xn, h2, route, meta, tot, mod3, 0 if last else n_ctx // TOK_ROWS,
                            l, exp_w_in, exp_b_in, exp_w_out, exp_b_out)
    return xc
```

```python
import functools

import jax
import jax.numpy as jnp
import numpy as np
from jax import lax
from jax.experimental import pallas as pl
from jax.experimental.pallas import tpu as pltpu

D_MODEL = 1024
DEPTH = 2
GRID_W = 64
EPS = 1e-6

D_MIX = D_MODEL
W_CONV = D_MIX // 4
W_POOL = D_MIX // 4
W_SSM = D_MIX // 4
W_MLA = D_MIX - W_CONV - W_POOL - W_SSM
CONV_K = 31
POOL_WINDOWS = (2, 4, 8, 16)
POOL_CH = W_POOL // len(POOL_WINDOWS)
SSM_CH = 16
SSM_GROUPS = W_SSM // SSM_CH
SSM_STATE = 64
MLA_V = 64
MLA_HEADS = W_MLA // MLA_V
MLA_NOPE = 64
MLA_ROPE = 32
MLA_QK = MLA_NOPE + MLA_ROPE
MLA_Q_RANK = 192
MLA_KV_RANK = 128
ROPE_AXIS = MLA_ROPE // 2
ROPE_BASE = 10000.0
N_EXPERTS = 32
TOP_K = 4
D_EXPERT = D_MODEL
SWIGLU_LIMIT = 7.0
SWIGLU_ALPHA = 1.702
MOE_BLOCK = 512

OFF_POOL = 2 * W_CONV
OFF_Q = OFF_POOL + W_POOL
OFF_SSM = OFF_Q + MLA_Q_RANK
OFF_KV = OFF_SSM + W_SSM
OFF_KPE = OFF_KV + MLA_KV_RANK
N_IN = OFF_KPE + MLA_ROPE

F32 = jnp.float32
BF16 = jnp.bfloat16
LANE = 128
SUBLANE = 8
QKP_W = 256


MOD_ROWS = 40
MOD_COLS = 1024


def _mod_kernel(c_ref, w_ref, b_ref, o_ref):
    c = c_ref[...]
    s = (c * jax.nn.sigmoid(c)).astype(BF16)
    o_ref[...] = jnp.dot(s, w_ref[...].astype(BF16), preferred_element_type=F32) + b_ref[...]


def modulation(c_all, ada_w, ada_b):
    n_out = ada_w.shape[1]
    return pl.pallas_call(
        _mod_kernel,
        out_shape=jax.ShapeDtypeStruct((MOD_ROWS, n_out), F32),
        grid=(n_out // MOD_COLS,),
        in_specs=[pl.BlockSpec((MOD_ROWS, D_MODEL), lambda j: (0, 0)),
                  pl.BlockSpec((D_MODEL, MOD_COLS), lambda j: (0, j)),
                  pl.BlockSpec((1, MOD_COLS), lambda j: (0, j))],
        out_specs=pl.BlockSpec((MOD_ROWS, MOD_COLS), lambda j: (0, j)),
        compiler_params=pltpu.CompilerParams(dimension_semantics=("arbitrary",)),
        name="modulation",
    )(c_all, ada_w, ada_b.reshape(1, n_out))


TOK_ROWS = 256
IN_SPLITS = (2 * W_CONV, W_POOL, W_SSM, MLA_KV_RANK, QKP_W)


def _stream_specs(stream, nct, off):
    if not isinstance(stream, tuple):
        return [pl.BlockSpec((None, TOK_ROWS, stream.shape[-1]), lambda b, i: (b, i + off, 0))], [stream]
    ctx, x = stream
    dim = x.shape[-1]
    return ([pl.BlockSpec((None, TOK_ROWS, dim), lambda b, i: (b, jnp.minimum(i + off, nct - 1), 0)),
             pl.BlockSpec((None, TOK_ROWS, dim), lambda b, i: (b, jnp.maximum(i + off - nct, 0), 0))], [ctx, x])


def _stream_tile(refs, nct, tile):
    if len(refs) == 1:
        return refs[0][...]
    return jnp.where(tile < nct, refs[0][...], refs[1][...])


def _mix_in_kernel(*refs, n_stream, nct):
    sh_ref, sc_ref, g_ref, w_ref = refs[n_stream:n_stream + 4]
    outs = refs[n_stream + 4:]
    x = _stream_tile(refs[:n_stream], nct, pl.program_id(1))
    y = x * lax.rsqrt(jnp.mean(x * x, axis=-1, keepdims=True) + EPS) * g_ref[...]
    h = y * (1.0 + sc_ref[...]) + sh_ref[...]
    p = jnp.dot(h.astype(BF16), w_ref[...], preferred_element_type=F32)
    off = 0
    for o_ref, w in zip(outs, IN_SPLITS):
        o_ref[...] = p[:, off:off + w]
        off += w


def mix_in_weights(w_mix_in):
    cols = [w_mix_in[:, :OFF_POOL], w_mix_in[:, OFF_POOL:OFF_Q], w_mix_in[:, OFF_SSM:OFF_KV],
            w_mix_in[:, OFF_KV:OFF_KPE], w_mix_in[:, OFF_Q:OFF_SSM], w_mix_in[:, OFF_KPE:],
            jnp.zeros((D_MODEL, QKP_W - MLA_Q_RANK - MLA_ROPE), w_mix_in.dtype)]
    return jnp.concatenate(cols, axis=1).astype(BF16)


def mix_in(stream, n_all, mod3, norm_g, w_in_p, n_ctx):
    bsz, dim = (stream[0] if isinstance(stream, tuple) else stream).shape[0], norm_g.shape[-1]
    assert n_ctx % TOK_ROWS == 0 and n_all % TOK_ROWS == 0 and bsz < MOD_ROWS
    nct = n_ctx // TOK_ROWS
    tile = lambda w: pl.BlockSpec((None, TOK_ROWS, w), lambda b, i: (b, i, 0))
    stream_specs, stream_args = _stream_specs(stream, nct, 0)

    def mod_spec(chunk):
        return pl.BlockSpec((None, 1, dim), lambda b, i: (jnp.where(i < nct, bsz, b), 0, chunk))
    return pl.pallas_call(
        functools.partial(_mix_in_kernel, n_stream=len(stream_args), nct=nct),
        out_shape=tuple(jax.ShapeDtypeStruct((bsz, n_all, w), F32) for w in IN_SPLITS),
        grid=(bsz, n_all // TOK_ROWS),
        in_specs=stream_specs + [mod_spec(0), mod_spec(1),
                                 pl.BlockSpec((1, dim), lambda b, i: (0, 0)),
                                 pl.BlockSpec(w_in_p.shape, lambda b, i: (0, 0))],
        out_specs=tuple(tile(w) for w in IN_SPLITS),
        compiler_params=pltpu.CompilerParams(dimension_semantics=("arbitrary", "arbitrary")),
        name="mix_in",
    )(*stream_args, mod3, mod3, norm_g.reshape(1, dim), w_in_p)


LM_PAD = 16
LM_ROWS = 128
LM_DBL_ROWS = 32


def _local_mixers_kernel(cin_ref, pin_ref, dw_ref, dwb_ref, lng_ref, lnb_ref, pw_ref, pwb_ref, psc_ref,
                         ya_ref, yp_ref, upad, ppad, s2, s4, s8, phase, *, n_ctx):
    n_all = cin_ref.shape[0]
    n_lat = n_all - n_ctx
    n_pad = upad.shape[0]
    lat0 = n_ctx + 2 * LM_PAD
    zeros = jnp.zeros((LM_PAD, W_CONV), F32)
    for buf in (upad, ppad):
        buf[0:LM_PAD, :] = zeros
        buf[LM_PAD + n_ctx:lat0, :] = zeros
        buf[lat0 + n_lat:n_pad, :] = zeros

    def pad_row(i):
        return pl.multiple_of(i * LM_ROWS + jnp.where(i < n_ctx // LM_ROWS, LM_PAD, 2 * LM_PAD), 8)

    def fill(i, _):
        r0 = pl.multiple_of(i * LM_ROWS, LM_ROWS)
        c = cin_ref[pl.ds(r0, LM_ROWS), :]
        dst = pl.ds(pad_row(i), LM_ROWS)
        upad[dst, :] = c[:, :W_CONV] * jax.nn.sigmoid(c[:, W_CONV:])
        ppad[dst, :] = pin_ref[pl.ds(r0, LM_ROWS), :]
        return 0
    lax.fori_loop(0, n_all // LM_ROWS, fill, 0)

    def doubling(src, dst, back, fwd):
        dst[0:8, :] = zeros[:8]
        dst[n_pad - 8:n_pad, :] = zeros[:8]

        def body(i, _):
            r0 = pl.multiple_of(8 + i * LM_DBL_ROWS, 8)
            h = src[pl.ds(r0 - 8, LM_DBL_ROWS + 16), :]
            dst[pl.ds(r0, LM_DBL_ROWS), :] = (h[8 - back:8 - back + LM_DBL_ROWS, :]
                                              + h[8 + fwd:8 + fwd + LM_DBL_ROWS, :])
            return 0
        lax.fori_loop(0, (n_pad - 16) // LM_DBL_ROWS, body, 0)
    doubling(ppad, s2, 1, 0)
    doubling(s2, s4, 1, 1)
    doubling(s4, s8, 2, 2)

    lane = lax.broadcasted_iota(jnp.int32, (LM_ROWS, W_POOL), 1)
    grp = lane // POOL_CH
    back = jnp.where(grp == 0, 1, jnp.where(grp == 1, 2, jnp.where(grp == 2, 4, 8)))
    fwd = back - 1

    def tile(i, _):
        r0 = pl.multiple_of(i * LM_ROWS, LM_ROWS)
        p0 = pad_row(i)
        halo = upad[pl.ds(p0 - LM_PAD, LM_ROWS + 2 * LM_PAD), :]
        span = LM_ROWS + 2 * LM_PAD - SUBLANE
        for b in range(SUBLANE):
            phase[b] = halo[b:b + span, :]
        acc = jnp.zeros((LM_ROWS, W_CONV), F32)
        for k in range(CONV_K):
            o = LM_PAD + k - CONV_K // 2
            a = o // SUBLANE * SUBLANE
            acc = acc + phase[o % SUBLANE, a:a + LM_ROWS, :] * dw_ref[k:k + 1, :]
        acc = acc + dwb_ref[...]
        mu = jnp.mean(acc, axis=-1, keepdims=True)
        var = jnp.mean(jnp.square(acc - mu), axis=-1, keepdims=True)
        v = (acc - mu) * lax.rsqrt(var + EPS) * lng_ref[...] + lnb_ref[...]
        v = v * jax.nn.sigmoid(v)
        ya_ref[pl.ds(r0, LM_ROWS), :] = jnp.dot(v.astype(BF16), pw_ref[...],
                                                preferred_element_type=F32).astype(ya_ref.dtype)
        rows = pl.ds(p0, LM_ROWS)
        h8 = s8[pl.ds(p0 - 8, LM_ROWS + 16), :]
        s16 = h8[4:4 + LM_ROWS, :] + h8[12:12 + LM_ROWS, :]
        wsum = jnp.where(grp == 0, s2[rows, :], jnp.where(grp == 1, s4[rows, :],
                                                          jnp.where(grp == 2, s8[rows, :], s16)))
        seg_len = jnp.where(i < n_ctx // LM_ROWS, n_ctx, n_lat)
        t = (r0 - jnp.where(i < n_ctx // LM_ROWS, 0, n_ctx)) + lax.broadcasted_iota(jnp.int32, (LM_ROWS, W_POOL), 0)
        cnt = jnp.minimum(t + fwd, seg_len - 1) - jnp.maximum(t - back, 0) + 1
        pooled = wsum / cnt.astype(F32) - ppad[rows, :]
        y = jnp.dot(pooled.astype(BF16), pwb_ref[...], preferred_element_type=F32) * psc_ref[...]
        yp_ref[pl.ds(r0, LM_ROWS), :] = y.astype(yp_ref.dtype)
        return 0
    lax.fori_loop(0, n_all // LM_ROWS, tile, 0)


def local_mixers(conv_in, pool_in, n_ctx, conv_dw, conv_dw_b, ln_g, ln_b, conv_pw, pool_w, pool_scale):
    bsz, n_all, _ = conv_in.shape
    assert n_ctx % LM_ROWS == 0 and n_all % LM_ROWS == 0
    n_pad = n_all + 3 * LM_PAD
    assert (n_pad - 16) % LM_DBL_ROWS == 0
    dw = jnp.zeros((CONV_K + 1, W_CONV), F32).at[:CONV_K].set(conv_dw)
    eye = jnp.eye(len(POOL_WINDOWS), dtype=F32)
    pwb = jnp.einsum('gcd,gh->gchd', pool_w, eye).reshape(W_POOL, W_POOL).astype(BF16)
    row = lambda v: v.reshape(1, -1).astype(F32)
    params = [dw, row(conv_dw_b), row(ln_g), row(ln_b), conv_pw.astype(BF16), pwb, row(pool_scale)]

    def full(arr):
        return pl.BlockSpec(arr.shape, lambda b: (0,) * arr.ndim)
    seq = lambda w: pl.BlockSpec((None, n_all, w), lambda b: (b, 0, 0))
    out = jax.ShapeDtypeStruct((bsz, n_all, W_CONV), BF16)
    return pl.pallas_call(
        functools.partial(_local_mixers_kernel, n_ctx=n_ctx),
        out_shape=(out, out),
        grid=(bsz,),
        in_specs=[seq(2 * W_CONV), seq(W_POOL)] + [full(p) for p in params],
        out_specs=(seq(W_CONV), seq(W_POOL)),
        scratch_shapes=[pltpu.VMEM((n_pad, W_CONV), F32) for _ in range(5)]
        + [pltpu.VMEM((SUBLANE, LM_ROWS + 2 * LM_PAD - SUBLANE, W_CONV), F32)],
        compiler_params=pltpu.CompilerParams(dimension_semantics=("arbitrary",)),
        name="local_mixers",
    )(conv_in, pool_in, *params)


S5_NS = SSM_GROUPS * SSM_STATE
S5_TC = 128
S5_BG = 8
S5_LANES = 512


def _dot2(a, b):
    m = a.shape[0] // 2
    return jnp.concatenate([jnp.dot(a[:m], b, preferred_element_type=F32),
                            jnp.dot(a[m:], b, preferred_element_type=F32)], axis=0)


def _s5_scan_kernel(u_ref, perm_ref, a_ref, bd_ref, cd_ref, *rest, reverse, final):
    if final:
        yp_ref, permt_ref, d_ref, gw_ref, gb_ref, y_ref, bu_s, hh_s, h_s = rest
    else:
        y_ref, bu_s, hh_s, h_s = rest
    tc = u_ref.shape[1]
    rows = S5_BG * tc
    j = pl.program_id(1)

    @pl.when(j == 0)
    def _():
        h_s[...] = jnp.zeros_like(h_s)

    u = u_ref[...].reshape(rows, W_SSM)
    u_hi = u.astype(BF16)
    if final:
        u_lo = (u - u_hi.astype(F32)).astype(BF16)
        u_tm = _dot2(perm_ref[...], jnp.concatenate([u_hi, u_lo], axis=1))
        u_hi_tm = u_tm[:, :W_SSM].astype(BF16)
        u_f32_tm = u_tm[:, :W_SSM] + u_tm[:, W_SSM:]
    else:
        u_hi_tm = _dot2(perm_ref[...], u_hi).astype(BF16)
    bu_s[...] = jnp.dot(u_hi_tm, bd_ref[...], preferred_element_type=F32)

    for hf in range(S5_NS // S5_LANES):
        re_cols = pl.ds(hf * S5_LANES, S5_LANES)
        im_cols = pl.ds(S5_NS + hf * S5_LANES, S5_LANES)
        a_re = jnp.broadcast_to(a_ref[0:1, hf * S5_LANES:(hf + 1) * S5_LANES], (S5_BG, S5_LANES))
        a_im = jnp.broadcast_to(a_ref[1:2, hf * S5_LANES:(hf + 1) * S5_LANES], (S5_BG, S5_LANES))

        def step(i, carry):
            h_re, h_im = carry
            t = (tc - 1 - i) if reverse else i
            r8 = pl.ds(pl.multiple_of(t * S5_BG, S5_BG), S5_BG)
            n_re = a_re * h_re - a_im * h_im + bu_s[r8, re_cols]
            n_im = a_re * h_im + a_im * h_re + bu_s[r8, im_cols]
            hh_s[r8, re_cols] = n_re
            hh_s[r8, im_cols] = n_im
            return n_re, n_im

        h_re, h_im = lax.fori_loop(0, tc, step, (h_s[:, re_cols], h_s[:, im_cols]), unroll=8)
        h_s[:, re_cols] = h_re
        h_s[:, im_cols] = h_im

    y = _dot2(hh_s[...].astype(BF16), cd_ref[...])
    if final:
        y = y + yp_ref[...] + d_ref[...] * u_f32_tm
        z = jax.nn.gelu(y)
        y = z * jax.nn.sigmoid(_dot2(z.astype(BF16), gw_ref[...]) + gb_ref[...])
        y = _dot2(permt_ref[...], y.astype(BF16))
        y_ref[...] = y.astype(BF16).reshape(S5_BG, tc, W_SSM)
    else:
        y_ref[...] = y


def s5_direction(u_all, perm, a, bd, cd, n_ctx, reverse, final_args=None):
    bsz, n_all, _ = u_all.shape
    tc = S5_TC
    n_chunks = n_all // tc
    ctx_chunks = n_ctx // tc
    rows = S5_BG * tc
    assert n_all % tc == 0 and n_ctx % tc == 0 and bsz % S5_BG == 0

    if reverse:
        def chunk(j):
            return jnp.where(j < ctx_chunks, ctx_chunks - 1 - j, n_chunks - 1 + ctx_chunks - j)
    else:
        def chunk(j):
            return j
    seq_spec = pl.BlockSpec((S5_BG, tc, W_SSM), lambda g, j: (g, chunk(j), 0))
    part_spec = pl.BlockSpec((None, None, rows, W_SSM), lambda g, j: (g, chunk(j), 0, 0))

    def full(arr):
        return pl.BlockSpec(arr.shape, lambda g, j: (0,) * arr.ndim)
    in_specs = [seq_spec, full(perm), full(a), full(bd), full(cd)]
    args = [u_all, perm, a, bd, cd]
    if final_args is not None:
        y_prev, d, glu_w, glu_b = final_args
        perm_t = perm.T
        in_specs += [part_spec, full(perm_t), full(d), full(glu_w), full(glu_b)]
        args += [y_prev, perm_t, d, glu_w, glu_b]
        out_shape = jax.ShapeDtypeStruct(u_all.shape, BF16)
        out_spec = seq_spec
    else:
        out_shape = jax.ShapeDtypeStruct((bsz // S5_BG, n_chunks, rows, W_SSM), F32)
        out_spec = part_spec
    return pl.pallas_call(
        functools.partial(_s5_scan_kernel, reverse=reverse, final=final_args is not None),
        out_shape=out_shape,
        grid=(bsz // S5_BG, n_chunks),
        in_specs=in_specs,
        out_specs=out_spec,
        scratch_shapes=[pltpu.VMEM((rows, 2 * S5_NS), F32), pltpu.VMEM((rows, 2 * S5_NS), F32),
                        pltpu.VMEM((S5_BG, 2 * S5_NS), F32)],
        compiler_params=pltpu.CompilerParams(dimension_semantics=("arbitrary", "arbitrary")),
        name="s5_scan_bwd" if reverse else "s5_scan_fwd",
    )(*args)


def s5_time_major_perm(tc):
    r = jnp.arange(S5_BG * tc)
    src = (r % S5_BG) * tc + r // S5_BG
    return (src[:, None] == r[None, :]).astype(BF16)


def s5_params(a_re, a_im, log_dt, b_re, b_im, c_re, c_im):
    a_re = jnp.minimum(a_re.astype(F32), -1e-4)
    a_im = a_im.astype(F32)
    dt = jnp.exp(log_dt.astype(F32))[:, None]
    mag = jnp.exp(a_re * dt)
    ab_re = mag * jnp.cos(a_im * dt)
    ab_im = mag * jnp.sin(a_im * dt)
    den = a_re * a_re + a_im * a_im
    f_re = ((ab_re - 1.0) * a_re + ab_im * a_im) / den
    f_im = (ab_im * a_re - (ab_re - 1.0) * a_im) / den
    bb_re = f_re[..., None] * b_re - f_im[..., None] * b_im
    bb_im = f_re[..., None] * b_im + f_im[..., None] * b_re
    eye = jnp.eye(SSM_GROUPS, dtype=F32)

    def in_map(bb):
        return jnp.einsum('gpc,gh->gchp', bb, eye).reshape(W_SSM, S5_NS)

    def out_map(cc):
        return jnp.einsum('gcp,gh->gphc', cc, eye).reshape(S5_NS, W_SSM)
    a = jnp.stack([ab_re.reshape(-1), ab_im.reshape(-1)])
    bd = jnp.concatenate([in_map(bb_re), in_map(bb_im)], axis=1).astype(BF16)
    cd = jnp.concatenate([out_map(c_re.astype(F32)), -out_map(c_im.astype(F32))], axis=0).astype(BF16)
    return a, bd, cd


def s5_mixer_pallas(u_all, n_ctx, a_re, a_im, log_dt, b_re, b_im, c_re, c_im, d, glu_w, glu_b):
    pf = s5_params(a_re[0], a_im[0], log_dt[0], b_re[0], b_im[0], c_re[0], c_im[0])
    pb = s5_params(a_re[1], a_im[1], log_dt[1], b_re[1], b_im[1], c_re[1], c_im[1])
    perm = s5_time_major_perm(S5_TC)
    y_f = s5_direction(u_all, perm, *pf, n_ctx, False)
    return s5_direction(u_all, perm, *pb, n_ctx, True,
                        (y_f, d.reshape(1, W_SSM).astype(F32), glu_w.astype(BF16), glu_b.reshape(1, W_SSM)))


HEAD_W = LANE
ATT_PREP_ROWS = 256
ATT_Q_ROWS = 256


def _head_dim_order():
    half = ROPE_AXIS // 2
    r = MLA_NOPE
    first = list(range(r, r + half)) + list(range(r + 2 * half, r + 3 * half))
    second = list(range(r + half, r + 2 * half)) + list(range(r + 3 * half, r + 4 * half))
    split = LANE // 2 - MLA_ROPE // 2
    return np.array(first + list(range(split)) + second + list(range(split, MLA_NOPE)))


def _to_head_lanes(w):
    order = _head_dim_order()
    pad = [(0, 0)] * (w.ndim - 1) + [(0, HEAD_W - len(order))]
    return jnp.pad(jnp.take(w, order, axis=-1), pad)


def mla_layout_params(q_a_g, wq_b, kv_a_g, wkv_b, q_g, k_g):
    wq = _to_head_lanes(wq_b.reshape(MLA_Q_RANK, MLA_HEADS, MLA_QK))
    wq_p = jnp.pad(wq, ((0, QKP_W - MLA_Q_RANK), (0, 0), (0, 0))).reshape(QKP_W, MLA_HEADS * HEAD_W).astype(BF16)
    qa_g = jnp.pad(q_a_g, (0, QKP_W - MLA_Q_RANK)).reshape(1, QKP_W)
    wkv = wkv_b.reshape(MLA_KV_RANK, MLA_HEADS, MLA_NOPE + MLA_V)
    wk = jnp.pad(wkv[:, :, :MLA_NOPE], ((0, 0), (0, 0), (0, MLA_ROPE)))
    wkv_p = jnp.concatenate([_to_head_lanes(wk).reshape(MLA_KV_RANK, -1),
                             wkv[:, :, MLA_NOPE:].reshape(MLA_KV_RANK, -1)], axis=1).astype(BF16)
    order = _head_dim_order()
    place = np.zeros((QKP_W, MLA_HEADS, HEAD_W), np.float32)
    for lane_i, dim in enumerate(order):
        if dim >= MLA_NOPE:
            place[MLA_Q_RANK + dim - MLA_NOPE, :, lane_i] = 1.0
    place = jnp.asarray(place.reshape(QKP_W, MLA_HEADS * HEAD_W), BF16)
    return (wq_p, qa_g, wkv_p, kv_a_g.reshape(1, MLA_KV_RANK), place,
            _to_head_lanes(q_g.reshape(1, MLA_QK)), _to_head_lanes(k_g.reshape(1, MLA_QK)))


def rope_lane_tables(n_ctx, n_lat):
    t = jnp.arange(n_lat)
    row = (t // GRID_W).astype(F32)
    col = (t % GRID_W).astype(F32)
    inv = ROPE_BASE ** (-jnp.arange(0, ROPE_AXIS, 2, dtype=F32) / ROPE_AXIS)
    ang = jnp.concatenate([row[:, None] * inv, col[:, None] * inv], axis=1)
    cos, sin = jnp.cos(ang), jnp.sin(ang)
    half = MLA_ROPE // 2
    one = jnp.ones((n_lat, LANE // 2 - half), F32)
    zero = jnp.zeros((n_lat, LANE // 2 - half), F32)
    cmul = jnp.concatenate([cos, one, cos, one], axis=1)
    smul = jnp.concatenate([-sin, zero, sin, zero], axis=1)
    ctx = [jnp.ones((n_ctx, HEAD_W), F32), jnp.zeros((n_ctx, HEAD_W), F32)]
    return jnp.stack([jnp.concatenate([c, x], axis=0) for c, x in zip(ctx, (cmul, smul))])


def _mla_kernel(ckv_ref, qkp_ref, rope_ref, wq_ref, qag_ref, wkv_ref, kvag_ref, place_ref, qg_ref, kg_ref,
                o_ref, q_s, k_s, v_s, *, n_ctx, ctx_queries):
    n_all = ckv_ref.shape[0]

    def rope(x, r0):
        rows = pl.ds(r0, ATT_PREP_ROWS)
        swapped = pltpu.roll(x, HEAD_W // 2, axis=1)
        return x * rope_ref[0, rows, :] + swapped * rope_ref[1, rows, :]

    def head_norm(x, g):
        ss = jnp.sum(x * x, axis=-1, keepdims=True) * (1.0 / MLA_QK)
        return x * lax.rsqrt(ss + EPS) * g

    def prep(i, _):
        r0 = pl.multiple_of(i * ATT_PREP_ROWS, ATT_PREP_ROWS)
        rows = pl.ds(r0, ATT_PREP_ROWS)
        ckv = ckv_ref[rows, :]
        kvn = ckv * lax.rsqrt(jnp.mean(ckv * ckv, axis=-1, keepdims=True) + EPS) * kvag_ref[...]
        kv = jnp.dot(kvn.astype(BF16), wkv_ref[...], preferred_element_type=F32)
        qkp = qkp_ref[rows, :]
        p_hi = qkp.astype(BF16)
        p_lo = (qkp - p_hi.astype(F32)).astype(BF16)
        kpe = (jnp.dot(p_hi, place_ref[...], preferred_element_type=F32)
               + jnp.dot(p_lo, place_ref[...], preferred_element_type=F32))
        lane = lax.broadcasted_iota(jnp.int32, qkp.shape, 1)
        qsq = jnp.where(lane < MLA_Q_RANK, qkp * qkp, 0.0)
        qn = qkp * lax.rsqrt(jnp.sum(qsq, axis=-1, keepdims=True) * (1.0 / MLA_Q_RANK) + EPS) * qag_ref[...]
        q = jnp.dot(qn.astype(BF16), wq_ref[...], preferred_element_type=F32)
        for h in range(MLA_HEADS):
            blk = slice(h * HEAD_W, (h + 1) * HEAD_W)
            k_h = rope(head_norm(kv[:, blk] + kpe[:, blk], kg_ref[...]), r0)
            q_h = rope(head_norm(q[:, blk], qg_ref[...]), r0) * (MLA_QK ** -0.5)
            k_s[h, rows, :] = k_h.astype(BF16)
            q_s[h, rows, :] = q_h.astype(BF16)
        v_s[rows, :] = kv[:, MLA_HEADS * HEAD_W:].astype(BF16)
        return 0

    lax.fori_loop(0, n_all // ATT_PREP_ROWS, prep, 0)

    lane = lax.broadcasted_iota(jnp.int32, (ATT_Q_ROWS, LANE), 1)

    def attend_rows(r0, n_keys):
        outs = []
        for pair in range(MLA_HEADS // 2):
            v2 = v_s[pl.ds(0, n_keys), pair * LANE:(pair + 1) * LANE]
            o2 = []
            for h in (2 * pair, 2 * pair + 1):
                q = q_s[h, pl.ds(r0, ATT_Q_ROWS), :]
                k = k_s[h, pl.ds(0, n_keys), :]
                s = lax.dot_general(q, k, (((1,), (1,)), ((), ())), preferred_element_type=F32)
                p = jnp.exp(s - jnp.max(s, axis=-1, keepdims=True))
                den = jnp.sum(p, axis=-1, keepdims=True)
                o2.append(jnp.dot(p.astype(BF16), v2, preferred_element_type=F32) / den)
            outs.append(jnp.where(lane < MLA_V, o2[0], o2[1]))
        return jnp.concatenate(outs, axis=1)

    for i in range(n_ctx // ATT_Q_ROWS):
        r0 = i * ATT_Q_ROWS
        if ctx_queries:
            o_ref[pl.ds(r0, ATT_Q_ROWS), :] = attend_rows(r0, n_ctx).astype(o_ref.dtype)
        else:
            o_ref[pl.ds(r0, ATT_Q_ROWS), :] = jnp.zeros((ATT_Q_ROWS, W_MLA), o_ref.dtype)

    def q_tile(i, _):
        r0 = pl.multiple_of(n_ctx + i * ATT_Q_ROWS, ATT_Q_ROWS)
        o_ref[pl.ds(r0, ATT_Q_ROWS), :] = attend_rows(r0, n_all).astype(o_ref.dtype)
        return 0

    lax.fori_loop(0, (n_all - n_ctx) // ATT_Q_ROWS, q_tile, 0, unroll=2)


def mla_attention(ckv, qkp, rope_tab, params, n_ctx, ctx_queries):
    bsz, n_all, _ = ckv.shape
    assert n_all % ATT_PREP_ROWS == 0 and n_ctx % ATT_Q_ROWS == 0 and (n_all - n_ctx) % ATT_Q_ROWS == 0

    def full(arr):
        return pl.BlockSpec(arr.shape, lambda b: (0,) * arr.ndim)
    return pl.pallas_call(
        functools.partial(_mla_kernel, n_ctx=n_ctx, ctx_queries=ctx_queries),
        out_shape=jax.ShapeDtypeStruct((bsz, n_all, W_MLA), BF16),
        grid=(bsz,),
        in_specs=[pl.BlockSpec((None, n_all, MLA_KV_RANK), lambda b: (b, 0, 0)),
                  pl.BlockSpec((None, n_all, QKP_W), lambda b: (b, 0, 0)),
                  full(rope_tab)] + [full(p) for p in params],
        out_specs=pl.BlockSpec((None, n_all, W_MLA), lambda b: (b, 0, 0)),
        scratch_shapes=[pltpu.VMEM((MLA_HEADS, n_all, HEAD_W), BF16), pltpu.VMEM((MLA_HEADS, n_all, HEAD_W), BF16),
                        pltpu.VMEM((n_all, MLA_HEADS * MLA_V), BF16)],
        compiler_params=pltpu.CompilerParams(dimension_semantics=("arbitrary",)),
        name="mla_attention",
    )(ckv, qkp, rope_tab, *params)


RUN_ROWS = 1280
RUN_BITS = (256, 128, 64, 32, 16, 8)
PACK_W = D_MODEL // 2
RT_EXPERT, RT_WEIGHT, RT_SLOT = 0, TOP_K, 2 * TOP_K
RT_ROWS = 16
META_SRC, META_CNT, META_BASE = 0, 1, 2
I32 = jnp.int32
U32 = jnp.uint32


def _mix_out_kernel(*refs, n_stream, nct, off):
    (ya_ref, yp_ref, ys_ref, ym_ref, g1_ref, sh_ref, sc_ref, ng_ref, wo_ref, rwt_ref, rbt_ref,
     tri_ref, upper_ref, lower_ref, xo_ref, h_ref, rt_ref, meta_ref, tot_ref, carry_s) = refs[n_stream:]
    first = jnp.logical_and(pl.program_id(0) == 0, pl.program_id(1) == 0)

    @pl.when(first)
    def _():
        carry_s[...] = jnp.zeros_like(carry_s)

    acc = jnp.dot(ya_ref[...], wo_ref[0:W_CONV, :], preferred_element_type=F32)
    acc += jnp.dot(yp_ref[...], wo_ref[W_CONV:W_CONV + W_POOL, :], preferred_element_type=F32)
    acc += jnp.dot(ys_ref[...], wo_ref[W_CONV + W_POOL:D_MIX - W_MLA, :], preferred_element_type=F32)
    acc += jnp.dot(ym_ref[...], wo_ref[D_MIX - W_MLA:, :], preferred_element_type=F32)
    x = _stream_tile(refs[:n_stream], nct, pl.program_id(1) + off) + g1_ref[...] * acc
    xo_ref[...] = x
    y = x * lax.rsqrt(jnp.mean(x * x, axis=-1, keepdims=True) + EPS) * ng_ref[...]
    h = (y * (1.0 + sc_ref[...]) + sh_ref[...]).astype(BF16)
    h_ref[...] = h
    logits = lax.dot_general(rwt_ref[...], h, (((1,), (1,)), ((), ())), preferred_element_type=F32) + rbt_ref[...]
    eid = lax.broadcasted_iota(I32, logits.shape, 0)
    onehot = jnp.zeros(logits.shape, F32)
    vals, hits, rows = [], [], []
    for k in range(TOP_K):
        m = jnp.max(logits, axis=0, keepdims=True)
        idx = jnp.min(jnp.where(logits == m, eid, N_EXPERTS), axis=0, keepdims=True)
        hit = eid == idx
        logits = jnp.where(hit, -jnp.inf, logits)
        onehot = jnp.where(hit, 1.0, onehot)
        vals.append(m)
        hits.append(hit)
        rows.append(idx.astype(F32))
    ex = [jnp.exp(v - vals[0]) for v in vals]
    den = ex[0] + ex[1] + ex[2] + ex[3]
    rows += [e / den for e in ex]
    oh = onehot.astype(BF16)
    before = jnp.dot(oh, tri_ref[...], preferred_element_type=F32)
    cnt_col = jnp.sum(onehot, axis=1, keepdims=True)
    cnt8_col = jnp.floor((cnt_col + (SUBLANE - 1)) * (1.0 / SUBLANE)) * SUBLANE
    src_col = jnp.dot(lower_ref[...], jnp.broadcast_to(cnt8_col, (N_EXPERTS, LANE)).astype(BF16),
                      preferred_element_type=F32)[:, 0:1]
    slot_of = before + src_col
    for k in range(TOP_K):
        rows.append(jnp.sum(jnp.where(hits[k], slot_of, 0.0), axis=0, keepdims=True))
    rows += [jnp.zeros_like(rows[0])] * (RT_ROWS - len(rows))
    rt_ref[...] = jnp.concatenate(rows, axis=0)
    ones = jnp.ones((SUBLANE, TOK_ROWS), BF16)
    cnt = lax.dot_general(ones, oh, (((1,), (1,)), ((), ())), preferred_element_type=F32)
    cnt = jnp.concatenate([cnt, jnp.zeros((SUBLANE, LANE - N_EXPERTS), F32)], axis=1)
    cnt8 = jnp.floor((cnt + (SUBLANE - 1)) * (1.0 / SUBLANE)) * SUBLANE
    src = jnp.dot(cnt8.astype(BF16), upper_ref[...], preferred_element_type=F32)
    base = carry_s[...]
    row = lax.broadcasted_iota(I32, (SUBLANE, LANE), 0)
    meta = jnp.where(row == META_SRC, src, jnp.where(row == META_CNT, cnt8, jnp.where(row == META_BASE, base, 0.0)))
    meta_ref[...] = meta.astype(I32)
    carry_s[...] = base + cnt8
    tot_ref[...] = (base + cnt8).astype(I32)


def mix_out(ya, yp, ys, ym, stream, mod3, norm_g, w_out, router_w, router_b, n_ctx, skip_ctx):
    bsz, n_all, _ = ya.shape
    dim = norm_g.shape[-1]
    nct = n_ctx // TOK_ROWS
    off = nct if skip_ctx else 0
    n_out = n_all - off * TOK_ROWS
    tiles = n_out // TOK_ROWS
    tile_in = lambda w: pl.BlockSpec((None, TOK_ROWS, w), lambda b, i: (b, i + off, 0))
    tile_out = lambda w: pl.BlockSpec((None, TOK_ROWS, w), lambda b, i: (b, i, 0))
    stream_specs, stream_args = _stream_specs(stream, nct, off)

    def mod_spec(chunk):
        return pl.BlockSpec((None, 1, dim), lambda b, i: (jnp.where(i + off < nct, bsz, b), 0, chunk))
    rwt = router_w.T.astype(BF16)
    rbt = router_b.reshape(N_EXPERTS, 1).astype(F32)
    r = jnp.arange(TOK_ROWS)
    tri = (r[:, None] < r[None, :]).astype(BF16)
    q = jnp.arange(LANE)
    upper = (q[:, None] < q[None, :]).astype(BF16)
    e = jnp.arange(N_EXPERTS)
    lower = (e[None, :] < e[:, None]).astype(BF16)
    full = lambda a: pl.BlockSpec(a.shape, lambda b, i: (0,) * a.ndim)
    wo = w_out.astype(BF16)
    ng = norm_g.reshape(1, dim)
    return pl.pallas_call(
        functools.partial(_mix_out_kernel, n_stream=len(stream_args), nct=nct, off=off),
        out_shape=(jax.ShapeDtypeStruct((bsz, n_out, dim), F32), jax.ShapeDtypeStruct((bsz, n_out, dim), BF16),
                   jax.ShapeDtypeStruct((bsz * tiles, RT_ROWS, TOK_ROWS), F32),
                   jax.ShapeDtypeStruct((bsz * tiles, SUBLANE, LANE), I32),
                   jax.ShapeDtypeStruct((SUBLANE, LANE), I32)),
        grid=(bsz, tiles),
        in_specs=stream_specs + [tile_in(W_CONV), tile_in(W_POOL), tile_in(W_SSM), tile_in(W_MLA),
                                 mod_spec(2), mod_spec(3), mod_spec(4), full(ng), full(wo), full(rwt), full(rbt),
                                 full(tri), full(upper), full(lower)],
        out_specs=(tile_out(dim), tile_out(dim),
                   pl.BlockSpec((None, RT_ROWS, TOK_ROWS), lambda b, i: (b * tiles + i, 0, 0)),
                   pl.BlockSpec((None, SUBLANE, LANE), lambda b, i: (b * tiles + i, 0, 0)),
                   pl.BlockSpec((SUBLANE, LANE), lambda b, i: (0, 0))),
        scratch_shapes=[pltpu.VMEM((SUBLANE, LANE), F32)],
        compiler_params=pltpu.CompilerParams(dimension_semantics=("arbitrary", "arbitrary")),
        name="mix_out",
    )(*stream_args, ya, yp, ys, ym, mod3, mod3, mod3, ng, wo, rwt, rbt, tri, upper, lower)


def expert_layout(meta, tot, n_tok):
    n_tiles = meta.shape[0]
    tot8 = tot[0, :N_EXPERTS]
    region = (tot8 + MOE_BLOCK - 1) // MOE_BLOCK * MOE_BLOCK
    g_end = jnp.cumsum(region)
    g_start = g_end - region
    n_blocks = -(-(n_tok * TOP_K + n_tiles * N_EXPERTS * (SUBLANE - 1)) // MOE_BLOCK) + N_EXPERTS
    block_start = jnp.arange(n_blocks) * MOE_BLOCK
    block_e = jnp.minimum(jnp.sum(g_end[None, :] <= block_start[:, None], axis=1), N_EXPERTS - 1).astype(I32)
    n_used = (g_end[-1] // MOE_BLOCK).astype(I32).reshape(1)
    g_start_l = jnp.zeros((LANE,), I32).at[:N_EXPERTS].set(g_start.astype(I32))
    total = jnp.sum(meta[:, META_CNT, :N_EXPERTS], axis=1)
    cnt = meta[:, META_CNT].at[:, N_EXPERTS].set(total)
    runs = jnp.stack([meta[:, META_SRC], cnt, meta[:, META_BASE] + g_start_l], axis=1)
    tails = jnp.zeros((2, LANE), I32).at[0, :N_EXPERTS].set((g_start + tot8).astype(I32))
    tails = tails.at[1, :N_EXPERTS].set((region - tot8).astype(I32))
    return runs, tails, block_e, n_used, n_blocks


TILE_BITS = (1024, 512, 256, 128, 64, 32, 16, 8)


def _start_run_copies(runs_ref, local, remote, sem, to_remote):
    def per_expert(e, c):
        s = runs_ref[0, 0, e]
        n = runs_ref[0, 1, e]
        d = runs_ref[0, 2, e]
        for bit in RUN_BITS:
            take = (n & bit) != 0
            loc = local.at[pl.ds(pl.multiple_of(s, SUBLANE), bit), :]
            rem = remote.at[pl.ds(pl.multiple_of(d, SUBLANE), bit), :]
            cp = pltpu.make_async_copy(loc, rem, sem) if to_remote else pltpu.make_async_copy(rem, loc, sem)

            @pl.when(take)
            def _():
                cp.start()
            s = s + jnp.where(take, bit, 0)
            d = d + jnp.where(take, bit, 0)
        return c
    lax.fori_loop(0, N_EXPERTS, per_expert, 0)


def _wait_run_copies(runs_ref, local, remote, sem, to_remote):
    total = runs_ref[0, 1, N_EXPERTS]
    for bit in TILE_BITS:
        loc = local.at[pl.ds(0, bit), :]
        rem = remote.at[pl.ds(0, bit), :]
        cp = pltpu.make_async_copy(loc, rem, sem) if to_remote else pltpu.make_async_copy(rem, loc, sem)

        @pl.when((total & bit) != 0)
        def _():
            cp.wait()


def _dispatch_kernel(runs_ref, runs_prev_ref, tails_ref, h_ref, slot_ref, xs_ref, buf, zbuf, sems):
    i = pl.program_id(0)
    cur = i % 2

    @pl.when(i == 0)
    def _():
        zbuf[...] = jnp.zeros_like(zbuf)

        def tail_copies(wait):
            def per_expert(e, c):
                d = tails_ref[0, e]
                n = tails_ref[1, e]
                for bit in RUN_BITS:
                    take = (n & bit) != 0
                    cp = pltpu.make_async_copy(zbuf.at[pl.ds(0, bit), :],
                                               xs_ref.at[pl.ds(pl.multiple_of(d, SUBLANE), bit), :], sems.at[2])

                    @pl.when(take)
                    def _():
                        if wait:
                            cp.wait()
                        else:
                            cp.start()
                    d = d + jnp.where(take, bit, 0)
                return c
            lax.fori_loop(0, N_EXPERTS, per_expert, 0)
        tail_copies(False)
        tail_copies(True)

    j = lax.broadcasted_iota(I32, (RUN_ROWS, TOK_ROWS), 0)
    hit = j == slot_ref[0:1, :]
    for k in range(1, TOP_K):
        hit = jnp.logical_or(hit, j == slot_ref[k:k + 1, :])
    p = jnp.where(hit, 1.0, 0.0).astype(BF16)
    lo = jnp.dot(p, h_ref[:, :PACK_W], preferred_element_type=F32)
    hi = jnp.dot(p, h_ref[:, PACK_W:], preferred_element_type=F32)
    lo_bits = lax.shift_right_logical(pltpu.bitcast(lo, U32), jnp.uint32(16))
    hi_bits = pltpu.bitcast(hi, U32) & jnp.uint32(0xFFFF0000)
    buf[cur] = hi_bits | lo_bits
    _start_run_copies(runs_ref, buf.at[cur], xs_ref, sems.at[cur], True)

    @pl.when(i > 0)
    def _():
        _wait_run_copies(runs_prev_ref, buf.at[1 - cur], xs_ref, sems.at[1 - cur], True)

    @pl.when(i == pl.num_programs(0) - 1)
    def _():
        _wait_run_copies(runs_ref, buf.at[cur], xs_ref, sems.at[cur], True)


def dispatch(h, slots_t, runs, tails, n_blocks):
    n_tok, dim = h.shape
    n_tiles = n_tok // TOK_ROWS
    return pl.pallas_call(
        _dispatch_kernel,
        out_shape=jax.ShapeDtypeStruct((n_blocks * MOE_BLOCK, PACK_W), U32),
        grid=(n_tiles,),
        in_specs=[pl.BlockSpec((1, 3, LANE), lambda i: (i, 0, 0), memory_space=pltpu.SMEM),
                  pl.BlockSpec((1, 3, LANE), lambda i: (jnp.maximum(i - 1, 0), 0, 0), memory_space=pltpu.SMEM),
                  pl.BlockSpec((2, LANE), lambda i: (0, 0), memory_space=pltpu.SMEM),
                  pl.BlockSpec((TOK_ROWS, dim), lambda i: (i, 0)),
                  pl.BlockSpec((None, SUBLANE, TOK_ROWS), lambda i: (i, 0, 0))],
        out_specs=pl.BlockSpec(memory_space=pl.ANY),
        scratch_shapes=[pltpu.VMEM((2, RUN_ROWS, PACK_W), U32), pltpu.VMEM((RUN_BITS[0], PACK_W), U32),
                        pltpu.SemaphoreType.DMA((3,))],
        compiler_params=pltpu.CompilerParams(dimension_semantics=("arbitrary",)),
        name="moe_dispatch",
    )(runs, runs, tails, h, slots_t)


def _expert_block_kernel(be_ref, nu_ref, x_ref, win_ref, bin_ref, wout_ref, bout_ref, o_ref, win_s, wout_s):
    i = pl.program_id(0)
    used = i < nu_ref[0]
    new_expert = jnp.logical_or(i == 0, be_ref[i] != be_ref[jnp.maximum(i - 1, 0)])

    @pl.when(jnp.logical_and(used, new_expert))
    def _():
        win_s[...] = win_ref[...].astype(BF16)
        wout_s[...] = wout_ref[...].astype(BF16)

    @pl.when(used)
    def _():
        w = x_ref[...]
        x_lo = pltpu.bitcast(lax.shift_left(w, jnp.uint32(16)), F32).astype(BF16)
        x_hi = pltpu.bitcast(w & jnp.uint32(0xFFFF0000), F32).astype(BF16)
        gu = (jnp.dot(x_lo, win_s[:PACK_W, :], preferred_element_type=F32)
              + jnp.dot(x_hi, win_s[PACK_W:, :], preferred_element_type=F32) + bin_ref[...])
        gate = jnp.minimum(gu[:, :D_EXPERT], SWIGLU_LIMIT)
        up = jnp.clip(gu[:, D_EXPERT:], -SWIGLU_LIMIT, SWIGLU_LIMIT)
        act = (up + 1.0) * gate * jax.nn.sigmoid(SWIGLU_ALPHA * gate)
        o_ref[...] = jnp.dot(act.astype(BF16), wout_s[...], preferred_element_type=F32) + bout_ref[...]

    @pl.when(jnp.logical_not(used))
    def _():
        o_ref[...] = jnp.zeros_like(o_ref)


def expert_blocks(xs, block_e, n_used, layer, w_in, b_in, w_out, b_out):
    n_pad = xs.shape[0]
    dim = w_out.shape[-1]
    n_blocks = n_pad // MOE_BLOCK

    def row_map(i, be, nu):
        return (jnp.minimum(i, nu[0] - 1), 0)

    def exp_map(i, be, nu):
        return (layer, be[i], 0, 0)

    return pl.pallas_call(
        _expert_block_kernel,
        out_shape=jax.ShapeDtypeStruct((n_pad, dim), F32),
        grid_spec=pltpu.PrefetchScalarGridSpec(
            num_scalar_prefetch=2,
            grid=(n_blocks,),
            in_specs=[
                pl.BlockSpec((MOE_BLOCK, PACK_W), row_map),
                pl.BlockSpec((None, None, dim, 2 * D_EXPERT), exp_map),
                pl.BlockSpec((None, None, 1, 2 * D_EXPERT), exp_map),
                pl.BlockSpec((None, None, D_EXPERT, dim), exp_map),
                pl.BlockSpec((None, None, 1, dim), exp_map),
            ],
            out_specs=pl.BlockSpec((MOE_BLOCK, dim), lambda i, be, nu: (i, 0)),
            scratch_shapes=[pltpu.VMEM((dim, 2 * D_EXPERT), BF16), pltpu.VMEM((D_EXPERT, dim), BF16)],
        ),
        compiler_params=pltpu.CompilerParams(dimension_semantics=("arbitrary",)),
        name="expert_blocks",
    )(block_e, n_used, xs, w_in, b_in[:, :, None, :], w_out, b_out[:, :, None, :])


def _combine_kernel(runs_ref, runs_next_ref, ys_ref, rt_ref, x_ref, g_ref, o_ref, ybuf, sems):
    i = pl.program_id(0)
    cur = i % 2

    @pl.when(i == 0)
    def _():
        ybuf[...] = jnp.zeros_like(ybuf)
        _start_run_copies(runs_ref, ybuf.at[0], ys_ref, sems.at[0], False)

    @pl.when(i + 1 < pl.num_programs(0))
    def _():
        _start_run_copies(runs_next_ref, ybuf.at[1 - cur], ys_ref, sems.at[1 - cur], False)

    _wait_run_copies(runs_ref, ybuf.at[cur], ys_ref, sems.at[cur], False)
    rt = rt_ref[...]
    j = lax.broadcasted_iota(I32, (TOK_ROWS, RUN_ROWS), 1).astype(F32)
    wmat = jnp.zeros((TOK_ROWS, RUN_ROWS), F32)
    for k in range(TOP_K):
        wmat = jnp.where(j == rt[:, RT_SLOT + k:RT_SLOT + k + 1], rt[:, RT_WEIGHT + k:RT_WEIGHT + k + 1], wmat)
    w_hi = wmat.astype(BF16)
    w_lo = (wmat - w_hi.astype(F32)).astype(BF16)
    y = ybuf[cur].astype(BF16)
    m = jnp.dot(w_hi, y, preferred_element_type=F32) + jnp.dot(w_lo, y, preferred_element_type=F32)
    o_ref[...] = x_ref[...] + g_ref[...] * m


def combine(ys, route, runs, xn, mod3, n_ctx_tiles):
    bsz, n_rows, dim = xn.shape
    tiles = n_rows // TOK_ROWS
    tile = lambda w: pl.BlockSpec((None, TOK_ROWS, w), lambda i: (i // tiles, i % tiles, 0))
    g_spec = pl.BlockSpec((None, 1, dim), lambda i: (jnp.where(i % tiles < n_ctx_tiles, bsz, i // tiles), 0, 5))
    return pl.pallas_call(
        _combine_kernel,
        out_shape=jax.ShapeDtypeStruct(xn.shape, F32),
        grid=(bsz * tiles,),
        in_specs=[pl.BlockSpec((1, 3, LANE), lambda i: (i, 0, 0), memory_space=pltpu.SMEM),
                  pl.BlockSpec((1, 3, LANE), lambda i: (jnp.minimum(i + 1, bsz * tiles - 1), 0, 0),
                               memory_space=pltpu.SMEM),
                  pl.BlockSpec(memory_space=pl.ANY), tile(RT_ROWS), tile(dim), g_spec],
        out_specs=tile(dim),
        scratch_shapes=[pltpu.VMEM((2, RUN_ROWS, dim), F32), pltpu.SemaphoreType.DMA((2,))],
        compiler_params=pltpu.CompilerParams(dimension_semantics=("arbitrary",)),
        name="moe_combine",
    )(runs, runs, ys, route, xn, mod3)


def routed_experts(xn, h2, route_t, meta, tot, mod3, n_ctx_tiles, layer, w_in, b_in, w_out, b_out):
    bsz, n_rows, dim = xn.shape
    n_tok = bsz * n_rows
    runs, tails, block_e, n_used, n_blocks = expert_layout(meta, tot, n_tok)
    slots_t = route_t[:, RT_SLOT:RT_SLOT + SUBLANE, :].astype(I32)
    route = jnp.swapaxes(route_t, 1, 2).reshape(bsz, n_rows, RT_ROWS)
    xs = dispatch(h2.reshape(n_tok, dim), slots_t, runs, tails, n_blocks)
    ys = expert_blocks(xs, block_e, n_used, layer, w_in, b_in, w_out, b_out)
    return combine(ys, route, runs, xn, mod3, n_ctx_tiles)


def kernel(x, c, ctx, c_ctx, ada_w, ada_b, norm1_g, norm2_g, w_mix_in, w_mix_out,
           conv_dw, conv_dw_b, conv_ln_g, conv_ln_b, conv_pw, pool_w, pool_scale,
           ssm_a_re, ssm_a_im, ssm_log_dt, ssm_b_re, ssm_b_im, ssm_c_re, ssm_c_im,
           ssm_d, ssm_glu_w, ssm_glu_b,
           mla_q_a_g, mla_wq_b, mla_kv_a_g, mla_wkv_b, mla_q_g, mla_k_g,
           router_w, router_b, exp_w_in, exp_b_in, exp_w_out, exp_b_out):
    bsz, n_lat, dim = x.shape
    n_ctx = ctx.shape[1]
    xc = (ctx, x)
    c_all = jnp.zeros((MOD_ROWS, dim), F32).at[:bsz].set(c).at[bsz].set(c_ctx)
    rope_tab = rope_lane_tables(n_ctx, n_lat)
    for l in range(DEPTH):
        last = l == DEPTH - 1
        mod = modulation(c_all, ada_w[l], ada_b[l])
        mod3 = mod.reshape(MOD_ROWS, 1, 6 * dim)

        conv_in, pool_in, ssm_in, ckv, qkp = mix_in(xc, n_ctx + n_lat, mod3, norm1_g[l],
                                                    mix_in_weights(w_mix_in[l]), n_ctx)
        ya, yp = local_mixers(conv_in, pool_in, n_ctx, conv_dw[l], conv_dw_b[l], conv_ln_g[l], conv_ln_b[l],
                              conv_pw[l], pool_w[l], pool_scale[l])
        ys = s5_mixer_pallas(ssm_in, n_ctx, ssm_a_re[l], ssm_a_im[l], ssm_log_dt[l],
                             ssm_b_re[l], ssm_b_im[l], ssm_c_re[l], ssm_c_im[l],
                             ssm_d[l], ssm_glu_w[l], ssm_glu_b[l])
        ym = mla_attention(ckv, qkp, rope_tab,
                           mla_layout_params(mla_q_a_g[l], mla_wq_b[l], mla_kv_a_g[l], mla_wkv_b[l],
                                             mla_q_g[l], mla_k_g[l]), n_ctx, not last)
        xn, h2, route, meta, tot = mix_out(ya, yp, ys, ym, xc, mod3, norm2_g[l], w_mix_out[l],
                                           router_w[l], router_b[l], n_ctx, last)
        xc = routed_experts(xn, h2, route, meta, tot, mod3, 0 if last else n_ctx // TOK_ROWS,
                            l, exp_w_in, exp_b_in, exp_w_out, exp_b_out)
    return xc
```
